```python
import math
import jax, jax.numpy as jnp
from jax import lax
import numpy as np

D_MODEL = 1024
BATCH = 16
SEQ = 2048
DEPTH = 1

DIFF_HEADS = 4
DIFF_HEAD_DIM = 64
DIFF_QK = DIFF_HEADS * 2 * DIFF_HEAD_DIM
DIFF_V = DIFF_HEADS * 2 * DIFF_HEAD_DIM
RWKV_HEAD_DIM = 64
RWKV_WIDTH = D_MODEL // 2
RWKV_HEADS = RWKV_WIDTH // RWKV_HEAD_DIM
DECAY_RANK = 64
ICLR_RANK = 64
GATE_RANK = 128
FFN_HIDDEN = -(-8 * D_MODEL // (3 * 256)) * 256
NUM_BUCKETS = 32
MAX_DISTANCE = 128
Q_BLOCK = 128
LN_EPS = 1e-5
LNX_EPS = 64e-5
NEG_BIG = -1e30
DEEPNORM_ALPHA = (2.0 * DEPTH) ** 0.25
DEEPNORM_BETA = (8.0 * DEPTH) ** -0.25

RWKV_SHIFT_WIDTH = 3 * RWKV_WIDTH + DECAY_RANK + ICLR_RANK + GATE_RANK
IN_SIZES = (DIFF_QK, DIFF_QK, DIFF_V, RWKV_SHIFT_WIDTH, D_MODEL, D_MODEL)
IN_SPLIT = tuple(int(s) for s in np.cumsum(IN_SIZES)[:-1])
W_IN_COLS = int(sum(IN_SIZES))
RWKV_SIZES = (RWKV_WIDTH, RWKV_WIDTH, RWKV_WIDTH, DECAY_RANK, ICLR_RANK, GATE_RANK)
RWKV_SPLIT = tuple(int(s) for s in np.cumsum(RWKV_SIZES)[:-1])

kernel_name = 'hybrid_diffattn_rwkv7_gated_deepnorm'


def _layer_norm(x, g, b, eps=LN_EPS):
    xf = x.astype(jnp.float32)
    mu = jnp.mean(xf, -1, keepdims=True)
    var = jnp.mean(jnp.square(xf - mu), -1, keepdims=True)
    return ((xf - mu) * lax.rsqrt(var + eps) * g + b).astype(x.dtype)


def _rms_norm(x, g, eps=LN_EPS):
    xf = x.astype(jnp.float32)
    return (xf * lax.rsqrt(jnp.mean(xf * xf, -1, keepdims=True) + eps) * g).astype(x.dtype)


def _t5_bucket(dist):
    max_exact = NUM_BUCKETS // 2
    d = jnp.maximum(dist, 1).astype(jnp.float32)
    large = max_exact + (jnp.log(d / max_exact) / math.log(MAX_DISTANCE / max_exact)
                         * (NUM_BUCKETS - max_exact)).astype(jnp.int32)
    large = jnp.minimum(large, NUM_BUCKETS - 1)
    return jnp.where(dist < max_exact, dist, large)


def _diff_attention(q, k, v, lam, rel_bias):
    B, S = q.shape[0], q.shape[1]
    nblk = S // Q_BLOCK
    scale = DIFF_HEAD_DIM ** -0.5
    qb = q.reshape(B, nblk, Q_BLOCK, DIFF_HEADS, 2, DIFF_HEAD_DIM).transpose(1, 0, 2, 3, 4, 5)
    k_pos = jnp.arange(S, dtype=jnp.int32)

    def block(args):
        i, q_blk = args
        q_pos = i * Q_BLOCK + jnp.arange(Q_BLOCK, dtype=jnp.int32)
        dist = q_pos[:, None] - k_pos[None, :]
        bias = rel_bias[_t5_bucket(jnp.maximum(dist, 0))].astype(jnp.float32)
        logits = jnp.einsum('bqhcd,bkhcd->bhcqk', q_blk, k).astype(jnp.float32) * scale
        logits = logits + bias.transpose(2, 0, 1)[None, :, None]
        logits = jnp.where((dist >= 0)[None, None, None], logits, NEG_BIG)
        p = jax.nn.softmax(logits, axis=-1)
        p = p[:, :, 0] - lam * p[:, :, 1]
        return jnp.einsum('bhqk,bkhe->bqhe', p.astype(v.dtype), v)

    out = lax.map(block, (jnp.arange(nblk, dtype=jnp.int32), qb))
    return out.transpose(1, 0, 2, 3, 4).reshape(B, S, DIFF_HEADS, 2 * DIFF_HEAD_DIM)


def _diff_branch(p_q, p_k, p_v, lq1, lk1, lq2, lk2, subln_g, rel_bias, lambda_init):
    B, S = p_q.shape[0], p_q.shape[1]
    q = p_q.reshape(B, S, DIFF_HEADS, 2, DIFF_HEAD_DIM)
    k = p_k.reshape(B, S, DIFF_HEADS, 2, DIFF_HEAD_DIM)
    v = p_v.reshape(B, S, DIFF_HEADS, 2 * DIFF_HEAD_DIM)
    f32 = jnp.float32
    lam = (jnp.exp(jnp.sum(lq1.astype(f32) * lk1.astype(f32)))
           - jnp.exp(jnp.sum(lq2.astype(f32) * lk2.astype(f32))) + lambda_init)
    o = _diff_attention(q, k, v, lam, rel_bias)
    o = _rms_norm(o, subln_g) * (1.0 - lambda_init)
    return o.reshape(B, S, DIFF_HEADS * 2 * DIFF_HEAD_DIM)


def _rwkv7_scan(r, w, k, v, kk, a):
    B, S, H, N = r.shape

    def step(state, inp):
        r_t, w_t, k_t, v_t, kk_t, a_t = inp
        sa = jnp.einsum('bhij,bhj->bhi', state, -kk_t)
        state = (state * w_t[:, :, None, :]
                 + sa[..., None] * (kk_t * a_t)[:, :, None, :]
                 + v_t[..., None] * k_t[:, :, None, :])
        y = jnp.einsum('bhij,bhj->bhi', state, r_t)
        return state, y

    xs = tuple(t.transpose(1, 0, 2, 3) for t in (r, w, k, v, kk, a))
    state0 = jnp.zeros((B, H, N, N), jnp.float32)
    _, ys = lax.scan(step, state0, xs)
    return ys.transpose(1, 0, 2, 3)


def _rwkv_branch(p, mu, w0, w2, a0, a2, g2, k_k, k_a, r_k, lnx_g, lnx_b):
    B, S = p.shape[0], p.shape[1]
    f32 = jnp.float32
    p_prev = jnp.pad(p[:, :-1], ((0, 0), (1, 0), (0, 0)))
    p = p + (p_prev - p) * mu
    r, k, v, wl, al, gl = jnp.split(p, RWKV_SPLIT, axis=-1)
    w = -jax.nn.softplus(-(w0 + jnp.tanh(wl) @ w2)) - 0.5
    decay = jnp.exp(-jnp.exp(w.astype(f32)))
    a = jax.nn.sigmoid(a0 + al @ a2)
    g = jax.nn.sigmoid(gl) @ g2
    heads = lambda t: t.reshape(B, S, RWKV_HEADS, RWKV_HEAD_DIM)
    kk = heads(k * k_k).astype(f32)
    kk = kk * lax.rsqrt(jnp.maximum(jnp.sum(kk * kk, -1, keepdims=True), 1e-24))
    k = k * (1.0 + (a - 1.0) * k_a)
    r_h, k_h, v_h, a_h = heads(r), heads(k), heads(v), heads(a)
    y = _rwkv7_scan(r_h.astype(f32), heads(decay), k_h.astype(f32), v_h.astype(f32),
                    kk, a_h.astype(f32))
    mean = jnp.mean(y, -1, keepdims=True)
    var = jnp.mean(jnp.square(y - mean), -1, keepdims=True)
    y = ((y - mean) * lax.rsqrt(var + LNX_EPS)).reshape(B, S, RWKV_WIDTH) * lnx_g + lnx_b
    bonus = jnp.sum(r_h.astype(f32) * k_h.astype(f32) * r_k, -1, keepdims=True) * v_h.astype(f32)
    y = y + bonus.reshape(B, S, RWKV_WIDTH)
    return (y * g).astype(p.dtype)


def setup_inputs(seed: int = 0) -> dict:
    key = jax.random.key(seed)
    ks = iter(jax.random.split(key, 48))
    L, D, beta = DEPTH, D_MODEL, DEEPNORM_BETA

    def nrm(shape, scale):
        return scale * jax.random.normal(next(ks), shape, jnp.float32)

    x = nrm((BATCH, SEQ, D), 1.0)
    ln_in_g = 1.0 + nrm((D,), 0.02)
    ln_in_b = nrm((D,), 0.02)
    rel_bias = nrm((NUM_BUCKETS, DIFF_HEADS), 0.3)
    w_in = jnp.concatenate([
        nrm((L, D, DIFF_QK), D ** -0.5),
        nrm((L, D, DIFF_QK), D ** -0.5),
        nrm((L, D, DIFF_V), beta * D ** -0.5),
        nrm((L, D, 2 * RWKV_WIDTH), D ** -0.5),
        nrm((L, D, RWKV_WIDTH), beta * D ** -0.5),
        nrm((L, D, DECAY_RANK + ICLR_RANK + GATE_RANK), D ** -0.5),
        nrm((L, D, 2 * D), D ** -0.5),
    ], axis=-1)
    diff_lam_q1 = nrm((L, DIFF_HEAD_DIM), 0.1)
    diff_lam_k1 = nrm((L, DIFF_HEAD_DIM), 0.1)
    diff_lam_q2 = nrm((L, DIFF_HEAD_DIM), 0.1)
    diff_lam_k2 = nrm((L, DIFF_HEAD_DIM), 0.1)
    diff_subln_g = 1.0 + nrm((L, 2 * DIFF_HEAD_DIM), 0.02)
    rwkv_mu = jax.random.uniform(next(ks), (L, RWKV_SHIFT_WIDTH), jnp.float32)
    rwkv_w0 = jnp.broadcast_to(jnp.linspace(-6.0, -1.0, RWKV_WIDTH, dtype=jnp.float32),
                               (L, RWKV_WIDTH)) + nrm((L, RWKV_WIDTH), 0.1)
    rwkv_w2 = nrm((L, DECAY_RANK, RWKV_WIDTH), 0.5 * DECAY_RANK ** -0.5)
    rwkv_a0 = nrm((L, RWKV_WIDTH), 0.1)
    rwkv_a2 = nrm((L, ICLR_RANK, RWKV_WIDTH), 0.5 * ICLR_RANK ** -0.5)
    rwkv_g2 = nrm((L, GATE_RANK, RWKV_WIDTH), GATE_RANK ** -0.5)
    rwkv_k_k = 0.85 + nrm((L, RWKV_WIDTH), 0.02)
    rwkv_k_a = 1.0 + nrm((L, RWKV_WIDTH), 0.02)
    rwkv_r_k = nrm((L, RWKV_HEADS, RWKV_HEAD_DIM), 0.1)
    rwkv_lnx_g = 1.0 + nrm((L, RWKV_WIDTH), 0.02)
    rwkv_lnx_b = nrm((L, RWKV_WIDTH), 0.02)
    w_up_a = nrm((L, DIFF_V, D), beta * DIFF_V ** -0.5)
    w_up_b = nrm((L, RWKV_WIDTH, D), beta * RWKV_WIDTH ** -0.5)
    w_out = nrm((L, D, D), beta * D ** -0.5)
    ln1_g = 1.0 + nrm((L, D), 0.02)
    ln1_b = nrm((L, D), 0.02)
    ffn_w_gate = nrm((L, D, FFN_HIDDEN), beta * D ** -0.5)
    ffn_w_up = nrm((L, D, FFN_HIDDEN), beta * D ** -0.5)
    ffn_w_down = nrm((L, FFN_HIDDEN, D), beta * FFN_HIDDEN ** -0.5)
    ln2_g = 1.0 + nrm((L, D), 0.02)
    ln2_b = nrm((L, D), 0.02)
    return {'x': x, 'ln_in_g': ln_in_g, 'ln_in_b': ln_in_b, 'rel_bias': rel_bias,
            'w_in': w_in, 'diff_lam_q1': diff_lam_q1, 'diff_lam_k1': diff_lam_k1,
            'diff_lam_q2': diff_lam_q2, 'diff_lam_k2': diff_lam_k2,
            'diff_subln_g': diff_subln_g, 'rwkv_mu': rwkv_mu, 'rwkv_w0': rwkv_w0,
            'rwkv_w2': rwkv_w2, 'rwkv_a0': rwkv_a0, 'rwkv_a2': rwkv_a2, 'rwkv_g2': rwkv_g2,
            'rwkv_k_k': rwkv_k_k, 'rwkv_k_a': rwkv_k_a, 'rwkv_r_k': rwkv_r_k,
            'rwkv_lnx_g': rwkv_lnx_g, 'rwkv_lnx_b': rwkv_lnx_b, 'w_up_a': w_up_a,
            'w_up_b': w_up_b, 'w_out': w_out, 'ln1_g': ln1_g, 'ln1_b': ln1_b,
            'ffn_w_gate': ffn_w_gate, 'ffn_w_up': ffn_w_up, 'ffn_w_down': ffn_w_down,
            'ln2_g': ln2_g, 'ln2_b': ln2_b}


def reference(x, ln_in_g, ln_in_b, rel_bias, w_in, diff_lam_q1, diff_lam_k1, diff_lam_q2,
              diff_lam_k2, diff_subln_g, rwkv_mu, rwkv_w0, rwkv_w2, rwkv_a0, rwkv_a2, rwkv_g2,
              rwkv_k_k, rwkv_k_a, rwkv_r_k, rwkv_lnx_g, rwkv_lnx_b, w_up_a, w_up_b, w_out,
              ln1_g, ln1_b, ffn_w_gate, ffn_w_up, ffn_w_down, ln2_g, ln2_b):
    h = _layer_norm(x, ln_in_g, ln_in_b)
    for l in range(DEPTH):
        lambda_init = 0.8 - 0.6 * math.exp(-0.3 * l)
        proj = h @ w_in[l]
        p_q, p_k, p_v, p_rwkv, p_ga, p_gb = jnp.split(proj, IN_SPLIT, axis=-1)
        y_a = _diff_branch(p_q, p_k, p_v, diff_lam_q1[l], diff_lam_k1[l], diff_lam_q2[l],
                           diff_lam_k2[l], diff_subln_g[l], rel_bias, lambda_init)
        y_b = _rwkv_branch(p_rwkv, rwkv_mu[l], rwkv_w0[l], rwkv_w2[l], rwkv_a0[l],
                           rwkv_a2[l], rwkv_g2[l], rwkv_k_k[l], rwkv_k_a[l], rwkv_r_k[l],
                           rwkv_lnx_g[l], rwkv_lnx_b[l])
        merged = (jax.nn.sigmoid(p_ga) * (y_a @ w_up_a[l])
                  + jax.nn.sigmoid(p_gb) * (y_b @ w_up_b[l]))
        mix = merged @ w_out[l]
        h = _layer_norm(DEEPNORM_ALPHA * h + mix, ln1_g[l], ln1_b[l])
        ffn = (jax.nn.silu(h @ ffn_w_gate[l]) * (h @ ffn_w_up[l])) @ ffn_w_down[l]
        h = _layer_norm(DEEPNORM_ALPHA * h + ffn, ln2_g[l], ln2_b[l])
    return h
```

```python
import functools
import math

import jax
import jax.numpy as jnp
from jax import lax
from jax.experimental import pallas as pl
from jax.experimental.pallas import tpu as pltpu

F32 = jnp.float32
BF16 = jnp.bfloat16

DIFF_HEADS = 4
DIFF_HEAD_DIM = 64
HEAD_PAIR = 2 * DIFF_HEAD_DIM
RWKV_HEAD_DIM = 64
DECAY_RANK = 64
ICLR_RANK = 64
GATE_RANK = 128
NUM_BUCKETS = 32
MAX_DISTANCE = 128
LN_EPS = 1e-5
LNX_EPS = 64e-5
NEG_BIG = -1e30
DEPTH = 1
DEEPNORM_ALPHA = (2.0 * DEPTH) ** 0.25
LAMBDA_INIT = 0.8 - 0.6 * math.exp(-0.3 * 0)

LANES = 128
VMEM_LIMIT_BYTES = 56 * 1024 * 1024
ROW_TILE = 256
ATTN_TILE = 256
SCAN_CHUNK = 64
SCAN_TILE = 256


def _mm(a, b):
    return jnp.dot(a.astype(BF16), b.astype(BF16), preferred_element_type=F32)


def _mm_nt(a, b):
    return lax.dot_general(a.astype(BF16), b.astype(BF16), (((1,), (1,)), ((), ())),
                           preferred_element_type=F32)


def _mm_tn(a, b):
    return lax.dot_general(a.astype(BF16), b.astype(BF16), (((0,), (0,)), ((), ())),
                           preferred_element_type=F32)


def _split3(x):
    hi = x.astype(BF16)
    r1 = x - hi.astype(F32)
    mid = r1.astype(BF16)
    lo = (r1 - mid.astype(F32)).astype(BF16)
    return hi, mid, lo


def _sel_mm(sel, x):
    hi, mid, lo = _split3(x)
    d = functools.partial(jnp.dot, preferred_element_type=F32)
    return d(sel, hi) + d(sel, mid) + d(sel, lo)


def _mm_sel(x, sel):
    hi, mid, lo = _split3(x)
    d = functools.partial(jnp.dot, preferred_element_type=F32)
    return d(hi, sel) + d(mid, sel) + d(lo, sel)


def _layer_norm(x, g, b, eps=LN_EPS):
    mu = jnp.mean(x, -1, keepdims=True)
    xc = x - mu
    var = jnp.mean(xc * xc, -1, keepdims=True)
    return xc * lax.rsqrt(var + eps) * g + b


def _softplus(x):
    return jnp.maximum(x, 0.0) + jnp.log1p(jnp.exp(-jnp.abs(x)))


def _const_spec(shape):
    return pl.BlockSpec(shape, lambda *_: (0,) * len(shape), pipeline_mode=pl.Buffered(1))


def _inproj_kernel(x_ref, g_ref, b_ref, w_ref, q_ref, k_ref, v_ref, rw_ref, ga_ref, gb_ref,
                   *, splits):
    h = _layer_norm(x_ref[...], g_ref[...], b_ref[...]).astype(BF16)

    def proj(lo, hi):
        return jnp.dot(h, w_ref[:, lo:hi], preferred_element_type=F32)

    s_q, s_k, s_v, s_rw, s_ga = splits
    q_ref[...] = (proj(0, s_q) * (DIFF_HEAD_DIM ** -0.5)).astype(BF16)
    k_ref[...] = proj(s_q, s_k).astype(BF16)
    v_ref[...] = proj(s_k, s_v).astype(BF16)
    rw_ref[...] = proj(s_v, s_rw)
    ga_ref[...] = jax.nn.sigmoid(proj(s_rw, s_ga)).astype(BF16)
    gb_ref[...] = jax.nn.sigmoid(proj(s_ga, w_ref.shape[1])).astype(BF16)


def _inproj(x2, ln_g, ln_b, w_in_bf, sizes):
    T, D = x2.shape
    n_q, n_k, n_v, n_rw, n_ga, n_gb = sizes
    acc, cuts = 0, []
    for s in sizes[:-1]:
        acc += s
        cuts.append(acc)
    splits = tuple(cuts)
    tm = ROW_TILE
    row = lambda n: pl.BlockSpec((tm, n), lambda i: (i, 0))
    return pl.pallas_call(
        functools.partial(_inproj_kernel, splits=splits),
        grid=(T // tm,),
        in_specs=[row(D), _const_spec((1, D)), _const_spec((1, D)), _const_spec(w_in_bf.shape)],
        out_specs=[row(n_q), row(n_k), row(n_v), row(n_rw), row(n_ga), row(n_gb)],
        out_shape=[jax.ShapeDtypeStruct((T, n_q), BF16), jax.ShapeDtypeStruct((T, n_k), BF16),
                   jax.ShapeDtypeStruct((T, n_v), BF16), jax.ShapeDtypeStruct((T, n_rw), F32),
                   jax.ShapeDtypeStruct((T, n_ga), BF16), jax.ShapeDtypeStruct((T, n_gb), BF16)],
        compiler_params=pltpu.CompilerParams(dimension_semantics=("parallel",),
                                             vmem_limit_bytes=VMEM_LIMIT_BYTES),
        name="ln_inproj",
    )(x2, ln_g.reshape(1, D), ln_b.reshape(1, D), w_in_bf)


def _attn_kernel(q_ref, k_ref, v_ref, bdiag_ref, bnear_ref, bfar_ref, lam_ref, g_ref, o_ref,
                 *, nq, t):
    lane = lax.broadcasted_iota(jnp.int32, (t, HEAD_PAIR), 1)
    lo = lane < DIFF_HEAD_DIM
    lv = lam_ref[...]
    lam = (jnp.exp(jnp.sum(lv[0:1] * lv[1:2], keepdims=True))
           - jnp.exp(jnp.sum(lv[2:3] * lv[3:4], keepdims=True)) + LAMBDA_INIT)
    bfar = bfar_ref[...][:, 0:1]
    zero = jnp.zeros((), BF16)

    def tile(qs, kblk, vblk, bias, carry):
        m, l, acc = carry
        s = lax.dot_general(qs, kblk, (((1,), (1,)), ((), ())), preferred_element_type=F32)
        s = s + bias
        m_new = jnp.maximum(m, jnp.max(s, -1, keepdims=True))
        alpha = jnp.exp(m - m_new)
        p = jnp.exp(s - m_new)
        l = alpha * l + jnp.sum(p, -1, keepdims=True)
        acc = alpha * acc + jnp.dot(p.astype(BF16), vblk, preferred_element_type=F32)
        return m_new, l, acc

    for qi in range(nq):
        q = q_ref[qi * t:(qi + 1) * t, :]
        qs = jnp.concatenate([jnp.where(lo, q, zero), jnp.where(lo, zero, q)], axis=0)
        carry = (jnp.full((2 * t, 1), -3e38, F32), jnp.zeros((2 * t, 1), F32),
                 jnp.zeros((2 * t, HEAD_PAIR), F32))
        if qi >= 2:
            def far(ki, c, qs=qs):
                r0 = pl.multiple_of(ki * t, t)
                return tile(qs, k_ref[pl.ds(r0, t), :], v_ref[pl.ds(r0, t), :], bfar, c)
            carry = lax.fori_loop(0, qi - 1, far, carry)
        if qi >= 1:
            sl = slice((qi - 1) * t, qi * t)
            carry = tile(qs, k_ref[sl, :], v_ref[sl, :], bnear_ref[...], carry)
        sl = slice(qi * t, (qi + 1) * t)
        m, l, acc = tile(qs, k_ref[sl, :], v_ref[sl, :], bdiag_ref[...], carry)
        o = acc / l
        o = o[:t] - lam * o[t:]
        ms = jnp.mean(o * o, -1, keepdims=True)
        o = o * lax.rsqrt(ms + LN_EPS) * g_ref[...] * (1.0 - LAMBDA_INIT)
        o_ref[sl, :] = o.astype(o_ref.dtype)


def _t5_bucket_table(n):
    dist = jnp.arange(n, dtype=jnp.int32)
    max_exact = NUM_BUCKETS // 2
    d = jnp.maximum(dist, 1).astype(F32)
    large = max_exact + (jnp.log(d / max_exact) / math.log(MAX_DISTANCE / max_exact)
                         * (NUM_BUCKETS - max_exact)).astype(jnp.int32)
    large = jnp.minimum(large, NUM_BUCKETS - 1)
    return jnp.where(dist < max_exact, dist, large)


def _attention(q, k, v, rel_bias, lam_vecs, subln_g):
    B, S, _ = q.shape
    t = ATTN_TILE
    assert S % t == 0 and t >= MAX_DISTANCE
    nq = S // t
    bias_by_dist = rel_bias[_t5_bucket_table(2 * t)].T.astype(F32)
    i = jnp.arange(t)[:, None]
    j = jnp.arange(t)[None, :]
    dd = i - j
    bdiag = jnp.where(dd >= 0, bias_by_dist[:, jnp.maximum(dd, 0)], NEG_BIG)
    bnear = bias_by_dist[:, dd + t]
    bdiag = jnp.concatenate([bdiag, bdiag], axis=1)
    bnear = jnp.concatenate([bnear, bnear], axis=1)
    bfar = jnp.broadcast_to(rel_bias[NUM_BUCKETS - 1][:, None, None], (DIFF_HEADS, 1, LANES))
    seq = pl.BlockSpec((None, S, HEAD_PAIR), lambda b, h: (b, 0, h))
    per_head = lambda r, c: pl.BlockSpec((None, r, c), lambda b, h: (h, 0, 0))
    return pl.pallas_call(
        functools.partial(_attn_kernel, nq=nq, t=t),
        grid=(B, DIFF_HEADS),
        in_specs=[seq, seq, seq, per_head(2 * t, t), per_head(2 * t, t), per_head(1, LANES),
                  pl.BlockSpec((4, DIFF_HEAD_DIM), lambda b, h: (0, 0)),
                  pl.BlockSpec((1, HEAD_PAIR), lambda b, h: (0, 0))],
        out_specs=seq,
        out_shape=jax.ShapeDtypeStruct((B, S, DIFF_HEADS * HEAD_PAIR), BF16),
        compiler_params=pltpu.CompilerParams(dimension_semantics=("parallel", "parallel"),
                                             vmem_limit_bytes=VMEM_LIMIT_BYTES),
        name="diff_attention",
    )(q, k, v, bdiag, bnear, bfar, lam_vecs, subln_g.reshape(1, HEAD_PAIR))


def _rwkv_kernel(rw_ref, mu_ref, w0_ref, w2_ref, a0_ref, a2_ref, g2_ref, kk_ref, ka_ref, rk_ref,
                 lg_ref, lb_ref, tri_ref, blk_ref, hsum_ref, o_ref,
                 h_s, prev_s, ah_s, rh_s, bt_s, kt_s, bb_s, kb_s, v_s, gl_s, y_s, bonus_s, gate_s,
                 *, width, n_pairs):
    L = SCAN_CHUNK
    TL = rw_ref.shape[0]
    j = pl.program_id(1)

    @pl.when(j == 0)
    def _():
        h_s[...] = jnp.zeros_like(h_s)
        prev_s[...] = jnp.zeros_like(prev_s)

    p = rw_ref[...]
    row = lax.broadcasted_iota(jnp.int32, (TL, 1), 0)
    p_prev = jnp.where(row == 0, prev_s[0:1, :], pltpu.roll(p, 1, 0))
    prev_s[0:1, :] = p[TL - 1:TL, :]
    ps = p + (p_prev - p) * mu_ref[...]
    r = ps[:, 0:width]
    k = ps[:, width:2 * width]
    v = ps[:, 2 * width:3 * width]
    lr = ps[:, 3 * width:3 * width + DECAY_RANK + ICLR_RANK]
    gl = ps[:, 3 * width + DECAY_RANK + ICLR_RANK:]
    hsum = hsum_ref[...]
    z = w0_ref[...] + _mm(jnp.tanh(lr), w2_ref[...])
    logw = -jnp.exp(-_softplus(-z) - 0.5)
    iclr = jax.nn.sigmoid(a0_ref[...] + _mm(lr, a2_ref[...]))
    gate_s[...] = _mm(jax.nn.sigmoid(gl), g2_ref[...])
    kk = k * kk_ref[...]
    kk = kk * lax.rsqrt(jnp.maximum(_mm_sel(kk * kk, hsum), 1e-24))
    k2 = k * (1.0 + (iclr - 1.0) * ka_ref[...])
    bonus_s[...] = _mm_sel(r * k2 * rk_ref[...], hsum) * v
    c = _sel_mm(tri_ref[...], logw)
    c_end = _sel_mm(blk_ref[...], logw)
    b = kk * iclr
    inv_g = jnp.exp(-c)
    to_end = jnp.exp(c_end - c)
    ah_s[...] = -kk * jnp.exp(c - logw)
    rh_s[...] = r * jnp.exp(c)
    bt_s[...] = b * inv_g
    kt_s[...] = k2 * inv_g
    bb_s[...] = b * to_end
    kb_s[...] = k2 * to_end
    v_s[...] = v
    gl_s[...] = jnp.exp(c_end)

    lane = lax.broadcasted_iota(jnp.int32, (L, LANES), 1)
    m0 = lane < RWKV_HEAD_DIM
    m1 = jnp.logical_not(m0)
    ri = lax.broadcasted_iota(jnp.int32, (2 * L, LANES), 0)
    ci = lax.broadcasted_iota(jnp.int32, (2 * L, LANES), 1)
    same_head = (ri // L) == (ci // L)
    p_mask = same_head & ((ci % L) < (ri % L))
    eye = ri == ci
    rl = lax.broadcasted_iota(jnp.int32, (L, LANES), 0)
    tril2 = (lane % L) <= rl
    stril2 = (lane % L) < rl
    eye_f = jnp.where(eye, 1.0, 0.0).astype(F32)
    zeros = jnp.zeros((L, LANES), F32)

    def chunk(ci_, carry):
        r0 = pl.multiple_of(ci_ * L, L)
        rows = pl.ds(r0, L)
        for pi in range(n_pairs):
            ls = slice(pi * LANES, (pi + 1) * LANES)
            ah, rh, bt, kt = ah_s[rows, ls], rh_s[rows, ls], bt_s[rows, ls], kt_s[rows, ls]
            bb, kb, vv = bb_s[rows, ls], kb_s[rows, ls], v_s[rows, ls]
            g_end = gl_s[rows, ls][0:1, :]
            sel = lambda m, x: jnp.where(m, x, zeros)
            g0 = _mm_nt(jnp.concatenate([sel(m0, ah), sel(m0, rh)], 0), jnp.concatenate([bt, kt], 0))
            g1 = _mm_nt(jnp.concatenate([sel(m1, ah), sel(m1, rh)], 0), jnp.concatenate([kt, bt], 0))
            gt0, gt1 = sel(stril2, g0[:L]), sel(stril2, g1[:L])
            gb0, gb1 = sel(tril2, g0[L:]), sel(tril2, g1[L:])
            pm = jnp.where(p_mask, jnp.concatenate([g0[:L], g1[:L]], 0), 0.0)
            tinv = eye_f + pm
            pw = pm
            for _ in range(int(math.log2(L)) - 1):
                pw = _mm(pw, pw)
                tinv = tinv + _mm(pw, tinv)
            v0, v1 = sel(m0, vv), sel(m1, vv)
            makv = _mm(jnp.concatenate([sel(m1, gt0), sel(m0, gt1)], 1),
                       jnp.concatenate([v0, v0, v1, v1], 0))
            x = jnp.concatenate([jnp.concatenate([sel(m0, ah), sel(m1, ah)], 0),
                                 jnp.concatenate([sel(m0, makv), sel(m1, makv)], 0)], 1)
            w = _mm(tinv, x)
            w1 = w[:L, :LANES] + w[L:, :LANES]
            w2 = w[:L, LANES:] + w[L:, LANES:]
            h = h_s[pi]
            u = _mm(w1, h) + w2
            s0 = jnp.concatenate([sel(m0, u), v0], 0)
            s1 = jnp.concatenate([v1, sel(m1, u)], 0)
            y = _mm(jnp.concatenate([rh, gb0, gb1], 1), jnp.concatenate([h, s0, s1], 0))
            hn = _mm_tn(jnp.concatenate([bb, kb], 0), jnp.concatenate([u, vv], 0))
            g_col = jnp.sum(jnp.where(eye, jnp.broadcast_to(g_end, (2 * L, LANES)), 0.0),
                            axis=1, keepdims=True)
            h_s[pi] = g_col * h + jnp.where(same_head, hn, 0.0)
            y_s[rows, ls] = y
        return carry

    lax.fori_loop(0, TL // L, chunk, 0)

    y = y_s[...]
    inv_n = 1.0 / RWKV_HEAD_DIM
    mean = _mm_sel(y, hsum) * inv_n
    yc = y - mean
    var = _mm_sel(yc * yc, hsum) * inv_n
    yn = yc * lax.rsqrt(var + LNX_EPS) * lg_ref[...] + lb_ref[...]
    o_ref[...] = ((yn + bonus_s[...]) * gate_s[...]).astype(o_ref.dtype)


def _rwkv(rw, mu, w0, w2, a0, a2, g2, k_k, k_a, r_k, lnx_g, lnx_b):
    B, S, n_rw = rw.shape
    width = w0.shape[-1]
    n_pairs = width // LANES
    L, TL = SCAN_CHUNK, SCAN_TILE
    assert S % TL == 0 and n_rw == 3 * width + DECAY_RANK + ICLR_RANK + GATE_RANK
    assert DECAY_RANK + ICLR_RANK == LANES
    ti = jnp.arange(TL)
    same_chunk = (ti[:, None] // L) == (ti[None, :] // L)
    tri = (same_chunk & (ti[None, :] <= ti[:, None])).astype(BF16)
    blk = same_chunk.astype(BF16)
    ch = jnp.arange(width) // RWKV_HEAD_DIM
    hsum = (ch[:, None] == ch[None, :]).astype(BF16)
    w2p = jnp.concatenate([w2, jnp.zeros((ICLR_RANK, width), w2.dtype)], 0).astype(BF16)
    a2p = jnp.concatenate([jnp.zeros((DECAY_RANK, width), a2.dtype), a2], 0).astype(BF16)
    vec = lambda a: a.reshape(1, -1).astype(F32)
    tile = pl.BlockSpec((None, TL, n_rw), lambda b, j: (b, j, 0))
    feat = lambda: pltpu.VMEM((TL, width), F32)
    return pl.pallas_call(
        functools.partial(_rwkv_kernel, width=width, n_pairs=n_pairs),
        grid=(B, S // TL),
        in_specs=[tile, _const_spec((1, n_rw)), _const_spec((1, width)), _const_spec((LANES, width)),
                  _const_spec((1, width)), _const_spec((LANES, width)), _const_spec((GATE_RANK, width)),
                  _const_spec((1, width)), _const_spec((1, width)), _const_spec((1, width)),
                  _const_spec((1, width)), _const_spec((1, width)),
                  _const_spec((TL, TL)), _const_spec((TL, TL)), _const_spec((width, width))],
        out_specs=pl.BlockSpec((None, TL, width), lambda b, j: (b, j, 0)),
        out_shape=jax.ShapeDtypeStruct((B, S, width), BF16),
        scratch_shapes=[pltpu.VMEM((n_pairs, LANES, LANES), F32), pltpu.VMEM((8, n_rw), F32)]
                       + [feat() for _ in range(11)],
        compiler_params=pltpu.CompilerParams(dimension_semantics=("parallel", "arbitrary"),
                                             vmem_limit_bytes=VMEM_LIMIT_BYTES),
        name="rwkv7_chunked",
    )(rw, vec(mu), vec(w0), w2p, vec(a0), a2p, g2.astype(BF16), vec(k_k), vec(k_a), vec(r_k),
      vec(lnx_g), vec(lnx_b), tri, blk, hsum)


def _post_kernel(x_ref, lng_ref, lnb_ref, ya_ref, yb_ref, ga_ref, gb_ref, wua_ref, wub_ref, wo_ref,
                 l1g_ref, l1b_ref, wg_ref, wu_ref, wd_ref, l2g_ref, l2b_ref, o_ref, *, ffn_chunk):
    dot = functools.partial(jnp.dot, preferred_element_type=F32)
    h = _layer_norm(x_ref[...], lng_ref[...], lnb_ref[...])
    merged = (ga_ref[...].astype(F32) * dot(ya_ref[...], wua_ref[...])
              + gb_ref[...].astype(F32) * dot(yb_ref[...], wub_ref[...]))
    mix = dot(merged.astype(BF16), wo_ref[...])
    h1 = _layer_norm(DEEPNORM_ALPHA * h + mix, l1g_ref[...], l1b_ref[...])
    h1b = h1.astype(BF16)
    hidden = wg_ref.shape[1]
    ffn = jnp.zeros_like(h1)
    for c0 in range(0, hidden, ffn_chunk):
        gate = dot(h1b, wg_ref[:, c0:c0 + ffn_chunk])
        up = dot(h1b, wu_ref[:, c0:c0 + ffn_chunk])
        act = (gate * jax.nn.sigmoid(gate) * up).astype(BF16)
        ffn = ffn + dot(act, wd_ref[c0:c0 + ffn_chunk, :])
    o_ref[...] = _layer_norm(DEEPNORM_ALPHA * h1 + ffn, l2g_ref[...], l2b_ref[...])


def _post(x2, ln_g, ln_b, ya, yb, ga, gb, w_up_a, w_up_b, w_out, ln1_g, ln1_b,
          w_gate, w_up, w_down, ln2_g, ln2_b):
    T, D = x2.shape
    hidden = w_gate.shape[1]
    ffn_chunk = hidden // 2
    assert ffn_chunk % LANES == 0
    tm = ROW_TILE
    row = lambda n: pl.BlockSpec((tm, n), lambda i: (i, 0))
    vec = lambda a: a.reshape(1, D)
    cs = lambda a: _const_spec(a.shape)
    wts = [w.astype(BF16) for w in (w_up_a, w_up_b, w_out, w_gate, w_up, w_down)]
    wua, wub, wo, wg, wu, wd = wts
    return pl.pallas_call(
        functools.partial(_post_kernel, ffn_chunk=ffn_chunk),
        grid=(T // tm,),
        in_specs=[row(D), _const_spec((1, D)), _const_spec((1, D)), row(ya.shape[1]), row(yb.shape[1]),
                  row(D), row(D), cs(wua), cs(wub), cs(wo), _const_spec((1, D)), _const_spec((1, D)),
                  cs(wg), cs(wu), cs(wd), _const_spec((1, D)), _const_spec((1, D))],
        out_specs=row(D),
        out_shape=jax.ShapeDtypeStruct((T, D), F32),
        compiler_params=pltpu.CompilerParams(dimension_semantics=("parallel",),
                                             vmem_limit_bytes=VMEM_LIMIT_BYTES),
        name="merge_ffn",
    )(x2, vec(ln_g), vec(ln_b), ya, yb, ga, gb, wua, wub, wo, vec(ln1_g), vec(ln1_b),
      wg, wu, wd, vec(ln2_g), vec(ln2_b))


def kernel(x, ln_in_g, ln_in_b, rel_bias, w_in, diff_lam_q1, diff_lam_k1, diff_lam_q2, diff_lam_k2, diff_subln_g, rwkv_mu, rwkv_w0, rwkv_w2, rwkv_a0, rwkv_a2, rwkv_g2, rwkv_k_k, rwkv_k_a, rwkv_r_k, rwkv_lnx_g, rwkv_lnx_b, w_up_a, w_up_b, w_out, ln1_g, ln1_b, ffn_w_gate, ffn_w_up, ffn_w_down, ln2_g, ln2_b):
    B, S, D = x.shape
    T = B * S
    width = rwkv_w0.shape[-1]
    n_qk = DIFF_HEADS * HEAD_PAIR
    n_rw = 3 * width + DECAY_RANK + ICLR_RANK + GATE_RANK
    sizes = (n_qk, n_qk, n_qk, n_rw, D, D)
    assert w_in.shape[0] == 1 and w_in.shape[2] == sum(sizes), "single-layer model expected"
    x2 = x.reshape(T, D)
    q, k, v, rw, ga, gb = _inproj(x2, ln_in_g, ln_in_b, w_in[0].astype(BF16), sizes)
    lam_vecs = jnp.concatenate([diff_lam_q1, diff_lam_k1, diff_lam_q2, diff_lam_k2], 0).astype(F32)
    ya = _attention(q.reshape(B, S, n_qk), k.reshape(B, S, n_qk), v.reshape(B, S, n_qk),
                    rel_bias, lam_vecs, diff_subln_g[0])
    yb = _rwkv(rw.reshape(B, S, n_rw), rwkv_mu[0], rwkv_w0[0], rwkv_w2[0], rwkv_a0[0], rwkv_a2[0],
               rwkv_g2[0], rwkv_k_k[0], rwkv_k_a[0], rwkv_r_k[0], rwkv_lnx_g[0], rwkv_lnx_b[0])
    out = _post(x2, ln_in_g, ln_in_b, ya.reshape(T, n_qk), yb.reshape(T, width), ga, gb,
                w_up_a[0], w_up_b[0], w_out[0], ln1_g[0], ln1_b[0],
                ffn_w_gate[0], ffn_w_up[0], ffn_w_down[0], ln2_g[0], ln2_b[0])
    return out.reshape(B, S, D)
```

```python
import functools
import math

import jax
import jax.numpy as jnp
from jax import lax
from jax.experimental import pallas as pl
from jax.experimental.pallas import tpu as pltpu

F32 = jnp.float32
BF16 = jnp.bfloat16

DIFF_HEADS = 4
DIFF_HEAD_DIM = 64
HEAD_PAIR = 2 * DIFF_HEAD_DIM
RWKV_HEAD_DIM = 64
DECAY_RANK = 64
ICLR_RANK = 64
GATE_RANK = 128
NUM_BUCKETS = 32
MAX_DISTANCE = 128
LN_EPS = 1e-5
LNX_EPS = 64e-5
NEG_BIG = -1e30
DEPTH = 1
DEEPNORM_ALPHA = (2.0 * DEPTH) ** 0.25
LAMBDA_INIT = 0.8 - 0.6 * math.exp(-0.3 * 0)

LANES = 128
VMEM_LIMIT_BYTES = 56 * 1024 * 1024
ROW_TILE = 256
ATTN_TILE = 256
SCAN_CHUNK = 64
SCAN_TILE = 256


def _mm(a, b):
    return jnp.dot(a.astype(BF16), b.astype(BF16), preferred_element_type=F32)


def _mm_nt(a, b):
    return lax.dot_general(a.astype(BF16), b.astype(BF16), (((1,), (1,)), ((), ())),
                           preferred_element_type=F32)


def _mm_tn(a, b):
    return lax.dot_general(a.astype(BF16), b.astype(BF16), (((0,), (0,)), ((), ())),
                           preferred_element_type=F32)


def _split2(x):
    hi = x.astype(BF16)
    lo = (x - hi.astype(F32)).astype(BF16)
    return hi, lo


def _sel_mm(sel, x):
    hi, lo = _split2(x)
    d = functools.partial(jnp.dot, preferred_element_type=F32)
    return d(sel, hi) + d(sel, lo)


def _mm_sel(x, sel):
    hi, lo = _split2(x)
    d = functools.partial(jnp.dot, preferred_element_type=F32)
    return d(hi, sel) + d(lo, sel)


def _layer_norm(x, g, b, eps=LN_EPS):
    mu = jnp.mean(x, -1, keepdims=True)
    xc = x - mu
    var = jnp.mean(xc * xc, -1, keepdims=True)
    return xc * lax.rsqrt(var + eps) * g + b


def _softplus(x):
    return jnp.maximum(x, 0.0) + jnp.log1p(jnp.exp(-jnp.abs(x)))


def _const_spec(shape):
    return pl.BlockSpec(shape, lambda *_: (0,) * len(shape), pipeline_mode=pl.Buffered(1))


def _inproj_kernel(x_ref, g_ref, b_ref, w_ref, q_ref, k_ref, v_ref, rw_ref, ga_ref, gb_ref,
                   *, splits):
    h = _layer_norm(x_ref[...], g_ref[...], b_ref[...]).astype(BF16)

    def proj(lo, hi):
        return jnp.dot(h, w_ref[:, lo:hi], preferred_element_type=F32)

    s_q, s_k, s_v, s_rw, s_ga = splits
    q_ref[...] = (proj(0, s_q) * (DIFF_HEAD_DIM ** -0.5)).astype(BF16)
    k_ref[...] = proj(s_q, s_k).astype(BF16)
    v_ref[...] = proj(s_k, s_v).astype(BF16)
    rw_ref[...] = proj(s_v, s_rw)
    ga_ref[...] = jax.nn.sigmoid(proj(s_rw, s_ga)).astype(BF16)
    gb_ref[...] = jax.nn.sigmoid(proj(s_ga, w_ref.shape[1])).astype(BF16)


def _inproj(x2, ln_g, ln_b, w_in_bf, sizes):
    T, D = x2.shape
    n_q, n_k, n_v, n_rw, n_ga, n_gb = sizes
    acc, cuts = 0, []
    for s in sizes[:-1]:
        acc += s
        cuts.append(acc)
    splits = tuple(cuts)
    tm = ROW_TILE
    row = lambda n: pl.BlockSpec((tm, n), lambda i: (i, 0))
    return pl.pallas_call(
        functools.partial(_inproj_kernel, splits=splits),
        grid=(T // tm,),
        in_specs=[row(D), _const_spec((1, D)), _const_spec((1, D)), _const_spec(w_in_bf.shape)],
        out_specs=[row(n_q), row(n_k), row(n_v), row(n_rw), row(n_ga), row(n_gb)],
        out_shape=[jax.ShapeDtypeStruct((T, n_q), BF16), jax.ShapeDtypeStruct((T, n_k), BF16),
                   jax.ShapeDtypeStruct((T, n_v), BF16), jax.ShapeDtypeStruct((T, n_rw), F32),
                   jax.ShapeDtypeStruct((T, n_ga), BF16), jax.ShapeDtypeStruct((T, n_gb), BF16)],
        compiler_params=pltpu.CompilerParams(dimension_semantics=("parallel",),
                                             vmem_limit_bytes=VMEM_LIMIT_BYTES),
        name="ln_inproj",
    )(x2, ln_g.reshape(1, D), ln_b.reshape(1, D), w_in_bf)


def _attn_kernel(q_ref, k_ref, v_ref, bdiag_ref, bnear_ref, bfar_ref, lam_ref, g_ref, o_ref,
                 *, nq, t):
    lane = lax.broadcasted_iota(jnp.int32, (t, HEAD_PAIR), 1)
    lo = lane < DIFF_HEAD_DIM
    lv = lam_ref[...]
    lam = (jnp.exp(jnp.sum(lv[0:1] * lv[1:2], keepdims=True))
           - jnp.exp(jnp.sum(lv[2:3] * lv[3:4], keepdims=True)) + LAMBDA_INIT)
    bfar = bfar_ref[...][:, 0:1]
    zero = jnp.zeros((), BF16)

    def tile(qs, kblk, vblk, bias, carry):
        m, l, acc = carry
        s = lax.dot_general(qs, kblk, (((1,), (1,)), ((), ())), preferred_element_type=F32)
        s = s + bias
        m_new = jnp.maximum(m, jnp.max(s, -1, keepdims=True))
        alpha = jnp.exp(m - m_new)
        p = jnp.exp(s - m_new)
        l = alpha * l + jnp.sum(p, -1, keepdims=True)
        acc = alpha * acc + jnp.dot(p.astype(BF16), vblk, preferred_element_type=F32)
        return m_new, l, acc

    for qi in range(nq):
        q = q_ref[qi * t:(qi + 1) * t, :]
        qs = jnp.concatenate([jnp.where(lo, q, zero), jnp.where(lo, zero, q)], axis=0)
        carry = (jnp.full((2 * t, 1), -3e38, F32), jnp.zeros((2 * t, 1), F32),
                 jnp.zeros((2 * t, HEAD_PAIR), F32))
        if qi >= 2:
            def far(ki, c, qs=qs):
                r0 = pl.multiple_of(ki * t, t)
                return tile(qs, k_ref[pl.ds(r0, t), :], v_ref[pl.ds(r0, t), :], bfar, c)
            carry = lax.fori_loop(0, qi - 1, far, carry)
        if qi >= 1:
            sl = slice((qi - 1) * t, qi * t)
            carry = tile(qs, k_ref[sl, :], v_ref[sl, :], bnear_ref[...], carry)
        sl = slice(qi * t, (qi + 1) * t)
        m, l, acc = tile(qs, k_ref[sl, :], v_ref[sl, :], bdiag_ref[...], carry)
        o = acc / l
        o = o[:t] - lam * o[t:]
        ms = jnp.mean(o * o, -1, keepdims=True)
        o = o * lax.rsqrt(ms + LN_EPS) * g_ref[...] * (1.0 - LAMBDA_INIT)
        o_ref[sl, :] = o.astype(o_ref.dtype)


def _t5_bucket(dist):
    max_exact = NUM_BUCKETS // 2
    d = jnp.maximum(dist, 1).astype(F32)
    large = max_exact + (jnp.log(d / max_exact) / math.log(MAX_DISTANCE / max_exact)
                         * (NUM_BUCKETS - max_exact)).astype(jnp.int32)
    large = jnp.minimum(large, NUM_BUCKETS - 1)
    return jnp.where(dist < max_exact, dist, large)


def _attention(q, k, v, rel_bias, lam_vecs, subln_g):
    B, S, _ = q.shape
    t = ATTN_TILE
    assert S % t == 0 and t >= MAX_DISTANCE
    nq = S // t
    dd = jnp.arange(t, dtype=jnp.int32)[:, None] - jnp.arange(t, dtype=jnp.int32)[None, :]
    buckets = jnp.arange(NUM_BUCKETS, dtype=jnp.int32)

    def bias_tile(dist):
        onehot = (_t5_bucket(dist)[..., None] == buckets).astype(F32)
        return jnp.einsum("ijb,bh->hij", onehot, rel_bias.astype(F32),
                          precision=lax.Precision.HIGHEST)

    bdiag = jnp.where(dd >= 0, bias_tile(jnp.maximum(dd, 0)), NEG_BIG)
    bnear = bias_tile(dd + t)
    bdiag = jnp.concatenate([bdiag, bdiag], axis=1)
    bnear = jnp.concatenate([bnear, bnear], axis=1)
    bfar = jnp.broadcast_to(rel_bias[NUM_BUCKETS - 1][:, None, None], (DIFF_HEADS, 1, LANES))
    seq = pl.BlockSpec((None, S, HEAD_PAIR), lambda b, h: (b, 0, h))
    per_head = lambda r, c: pl.BlockSpec((None, r, c), lambda b, h: (h, 0, 0))
    return pl.pallas_call(
        functools.partial(_attn_kernel, nq=nq, t=t),
        grid=(B, DIFF_HEADS),
        in_specs=[seq, seq, seq, per_head(2 * t, t), per_head(2 * t, t), per_head(1, LANES),
                  pl.BlockSpec((4, DIFF_HEAD_DIM), lambda b, h: (0, 0)),
                  pl.BlockSpec((1, HEAD_PAIR), lambda b, h: (0, 0))],
        out_specs=seq,
        out_shape=jax.ShapeDtypeStruct((B, S, DIFF_HEADS * HEAD_PAIR), BF16),
        compiler_params=pltpu.CompilerParams(dimension_semantics=("parallel", "parallel"),
                                             vmem_limit_bytes=VMEM_LIMIT_BYTES),
        name="diff_attention",
    )(q, k, v, bdiag, bnear, bfar, lam_vecs, subln_g.reshape(1, HEAD_PAIR))


def _rwkv_kernel(rw_ref, mu_ref, w0_ref, w2_ref, a0_ref, a2_ref, g2_ref, kk_ref, ka_ref, rk_ref,
                 lg_ref, lb_ref, tri_ref, hsum_ref, o_ref,
                 h_s, prev_s, ah_s, rh_s, bt_s, kt_s, bb_s, kb_s, v_s, gl_s, y_s, bonus_s, gate_s,
                 *, width, n_pairs):
    L = SCAN_CHUNK
    TL = rw_ref.shape[0]
    j = pl.program_id(1)

    @pl.when(j == 0)
    def _():
        h_s[...] = jnp.zeros_like(h_s)
        prev_s[...] = jnp.zeros_like(prev_s)

    p = rw_ref[...]
    row = lax.broadcasted_iota(jnp.int32, (TL, 1), 0)
    p_prev = jnp.where(row == 0, prev_s[0:1, :], pltpu.roll(p, 1, 0))
    prev_s[0:1, :] = p[TL - 1:TL, :]
    ps = p + (p_prev - p) * mu_ref[...]
    r = ps[:, 0:width]
    k = ps[:, width:2 * width]
    v = ps[:, 2 * width:3 * width]
    lr = ps[:, 3 * width:3 * width + DECAY_RANK + ICLR_RANK]
    gl = ps[:, 3 * width + DECAY_RANK + ICLR_RANK:]
    hsum = hsum_ref[...]
    z = w0_ref[...] + _mm(jnp.tanh(lr), w2_ref[...])
    logw = -jnp.exp(-_softplus(-z) - 0.5)
    iclr = jax.nn.sigmoid(a0_ref[...] + _mm(lr, a2_ref[...]))
    gate_s[...] = _mm(jax.nn.sigmoid(gl), g2_ref[...])
    kk = k * kk_ref[...]
    kk = kk * lax.rsqrt(jnp.maximum(_mm_sel(kk * kk, hsum), 1e-24))
    k2 = k * (1.0 + (iclr - 1.0) * ka_ref[...])
    bonus_s[...] = _mm_sel(r * k2 * rk_ref[...], hsum) * v
    c = _sel_mm(tri_ref[...], logw)
    c_end = jnp.concatenate(
        [jnp.broadcast_to(c[i * L + L - 1:i * L + L, :], (L, width)) for i in range(TL // L)], 0)
    b = kk * iclr
    inv_g = jnp.exp(-c)
    to_end = jnp.exp(c_end - c)
    ah_s[...] = -kk * jnp.exp(c - logw)
    rh_s[...] = r * jnp.exp(c)
    bt_s[...] = b * inv_g
    kt_s[...] = k2 * inv_g
    bb_s[...] = b * to_end
    kb_s[...] = k2 * to_end
    v_s[...] = v
    gl_s[...] = jnp.exp(c_end)

    lane = lax.broadcasted_iota(jnp.int32, (L, LANES), 1)
    m0 = lane < RWKV_HEAD_DIM
    m1 = jnp.logical_not(m0)
    ri = lax.broadcasted_iota(jnp.int32, (2 * L, LANES), 0)
    ci = lax.broadcasted_iota(jnp.int32, (2 * L, LANES), 1)
    same_head = (ri // L) == (ci // L)
    p_mask = same_head & ((ci % L) < (ri % L))
    eye = ri == ci
    rl = lax.broadcasted_iota(jnp.int32, (L, LANES), 0)
    tril2 = (lane % L) <= rl
    stril2 = (lane % L) < rl
    eye_f = jnp.where(eye, 1.0, 0.0).astype(F32)
    zeros = jnp.zeros((L, LANES), F32)

    sel = lambda m, x: jnp.where(m, x, zeros)
    cat0 = lambda *xs: jnp.concatenate(xs, 0)
    n_chunks = TL // L
    blocks = [(c, pi) for c in range(n_chunks) for pi in range(n_pairs)]
    ld = lambda ref, c, pi: ref[c * L:(c + 1) * L, pi * LANES:(pi + 1) * LANES]

    ah = [ld(ah_s, *b_) for b_ in blocks]
    rh = [ld(rh_s, *b_) for b_ in blocks]
    bt = [ld(bt_s, *b_) for b_ in blocks]
    kt = [ld(kt_s, *b_) for b_ in blocks]
    vv = [ld(v_s, *b_) for b_ in blocks]
    ah0 = [sel(m0, x) for x in ah]
    ah1 = [sel(m1, x) for x in ah]
    v0 = [sel(m0, x) for x in vv]
    v1 = [sel(m1, x) for x in vv]
    g0 = [_mm_nt(cat0(ah0[i], sel(m0, rh[i])), cat0(bt[i], kt[i])) for i in range(len(blocks))]
    g1 = [_mm_nt(cat0(ah1[i], sel(m1, rh[i])), cat0(kt[i], bt[i])) for i in range(len(blocks))]
    gb0 = [sel(tril2, x[L:]) for x in g0]
    gb1 = [sel(tril2, x[L:]) for x in g1]
    pw = [jnp.where(p_mask, cat0(x0[:L], x1[:L]), 0.0) for x0, x1 in zip(g0, g1)]
    tinv = [eye_f + x for x in pw]
    for _ in range(int(math.log2(L)) - 1):
        pwb = [x.astype(BF16) for x in pw]
        pw = [jnp.dot(x, x, preferred_element_type=F32) for x in pwb]
        tinv = [t + _mm(x, t) for x, t in zip(pw, tinv)]
    makv = [_mm(jnp.concatenate([sel(m1 & stril2, x0[:L]), sel(m0 & stril2, x1[:L])], 1),
                cat0(a, a, b_, b_)) for x0, x1, a, b_ in zip(g0, g1, v0, v1)]
    w = [_mm(t, jnp.concatenate([cat0(a0_, a1_), cat0(sel(m0, mk), sel(m1, mk))], 1))
         for t, a0_, a1_, mk in zip(tinv, ah0, ah1, makv)]
    w1 = [x[:L, :LANES] + x[L:, :LANES] for x in w]
    w2 = [x[:L, LANES:] + x[L:, LANES:] for x in w]

    hs = [h_s[pi] for pi in range(n_pairs)]
    for c in range(n_chunks):
        for pi in range(n_pairs):
            i = c * n_pairs + pi
            h = hs[pi]
            u = _mm(w1[i], h) + w2[i]
            s0 = cat0(sel(m0, u), v0[i])
            s1 = cat0(v1[i], sel(m1, u))
            y = _mm(jnp.concatenate([rh[i], gb0[i], gb1[i]], 1), cat0(h, s0, s1))
            hn = _mm_tn(cat0(ld(bb_s, c, pi), ld(kb_s, c, pi)), cat0(u, vv[i]))
            g_end = ld(gl_s, c, pi)[0:1, :]
            g_col = jnp.sum(jnp.where(eye, jnp.broadcast_to(g_end, (2 * L, LANES)), 0.0),
                            axis=1, keepdims=True)
            hs[pi] = g_col * h + jnp.where(same_head, hn, 0.0)
            y_s[c * L:(c + 1) * L, pi * LANES:(pi + 1) * LANES] = y
    for pi in range(n_pairs):
        h_s[pi] = hs[pi]

    y = y_s[...]
    inv_n = 1.0 / RWKV_HEAD_DIM
    mean = _mm_sel(y, hsum) * inv_n
    yc = y - mean
    var = _mm_sel(yc * yc, hsum) * inv_n
    yn = yc * lax.rsqrt(var + LNX_EPS) * lg_ref[...] + lb_ref[...]
    o_ref[...] = ((yn + bonus_s[...]) * gate_s[...]).astype(o_ref.dtype)


def _rwkv(rw, mu, w0, w2, a0, a2, g2, k_k, k_a, r_k, lnx_g, lnx_b):
    B, S, n_rw = rw.shape
    width = w0.shape[-1]
    n_pairs = width // LANES
    L, TL = SCAN_CHUNK, SCAN_TILE
    assert S % TL == 0 and n_rw == 3 * width + DECAY_RANK + ICLR_RANK + GATE_RANK
    assert DECAY_RANK + ICLR_RANK == LANES
    ti = jnp.arange(TL)
    same_chunk = (ti[:, None] // L) == (ti[None, :] // L)
    tri = (same_chunk & (ti[None, :] <= ti[:, None])).astype(BF16)
    ch =jnp.arange(width) // RWKV_HEAD_DIM
    hsum = (ch[:, None] == ch[None, :]).astype(BF16)
    w2p = jnp.concatenate([w2, jnp.zeros((ICLR_RANK, width), w2.dtype)], 0).astype(BF16)
    a2p = jnp.concatenate([jnp.zeros((DECAY_RANK, width), a2.dtype), a2], 0).astype(BF16)
    vec = lambda a: a.reshape(1, -1).astype(F32)
    tile = pl.BlockSpec((None, TL, n_rw), lambda b, j: (b, j, 0))
    feat = lambda: pltpu.VMEM((TL, width), F32)
    return pl.pallas_call(
        functools.partial(_rwkv_kernel, width=width, n_pairs=n_pairs),
        grid=(B, S // TL),
        in_specs=[tile, _const_spec((1, n_rw)), _const_spec((1, width)), _const_spec((LANES, width)),
                  _const_spec((1, width)), _const_spec((LANES, width)), _const_spec((GATE_RANK, width)),
                  _const_spec((1, width)), _const_spec((1, width)), _const_spec((1, width)),
                  _const_spec((1, width)), _const_spec((1, width)),
                  _const_spec((TL, TL)), _const_spec((width, width))],
        out_specs=pl.BlockSpec((None, TL, width), lambda b, j: (b, j, 0)),
        out_shape=jax.ShapeDtypeStruct((B, S, width), BF16),
        scratch_shapes=[pltpu.VMEM((n_pairs, LANES, LANES), F32), pltpu.VMEM((8, n_rw), F32)]
                       + [feat() for _ in range(11)],
        compiler_params=pltpu.CompilerParams(dimension_semantics=("parallel", "arbitrary"),
                                             vmem_limit_bytes=VMEM_LIMIT_BYTES),
        name="rwkv7_chunked",
    )(rw, vec(mu), vec(w0), w2p, vec(a0), a2p, g2.astype(BF16), vec(k_k), vec(k_a), vec(r_k),
      vec(lnx_g), vec(lnx_b), tri, hsum)


def _post_kernel(x_ref, lng_ref, lnb_ref, ya_ref, yb_ref, ga_ref, gb_ref, wua_ref, wub_ref, wo_ref,
                 l1g_ref, l1b_ref, wg_ref, wu_ref, wd_ref, l2g_ref, l2b_ref, o_ref, *, ffn_chunk):
    dot = functools.partial(jnp.dot, preferred_element_type=F32)
    h = _layer_norm(x_ref[...], lng_ref[...], lnb_ref[...])
    merged = (ga_ref[...].astype(F32) * dot(ya_ref[...], wua_ref[...])
              + gb_ref[...].astype(F32) * dot(yb_ref[...], wub_ref[...]))
    mix = dot(merged.astype(BF16), wo_ref[...])
    h1 = _layer_norm(DEEPNORM_ALPHA * h + mix, l1g_ref[...], l1b_ref[...])
    h1b = h1.astype(BF16)
    hidden = wg_ref.shape[1]
    ffn = jnp.zeros_like(h1)
    for c0 in range(0, hidden, ffn_chunk):
        gate = dot(h1b, wg_ref[:, c0:c0 + ffn_chunk])
        up = dot(h1b, wu_ref[:, c0:c0 + ffn_chunk])
        act = (gate * jax.nn.sigmoid(gate) * up).astype(BF16)
        ffn = ffn + dot(act, wd_ref[c0:c0 + ffn_chunk, :])
    o_ref[...] = _layer_norm(DEEPNORM_ALPHA * h1 + ffn, l2g_ref[...], l2b_ref[...])


def _post(x2, ln_g, ln_b, ya, yb, ga, gb, w_up_a, w_up_b, w_out, ln1_g, ln1_b,
          w_gate, w_up, w_down, ln2_g, ln2_b):
    T, D = x2.shape
    hidden = w_gate.shape[1]
    ffn_chunk = hidden // 2
    assert ffn_chunk % LANES == 0
    tm = ROW_TILE
    row = lambda n: pl.BlockSpec((tm, n), lambda i: (i, 0))
    vec = lambda a: a.reshape(1, D)
    cs = lambda a: _const_spec(a.shape)
    wts = [w.astype(BF16) for w in (w_up_a, w_up_b, w_out, w_gate, w_up, w_down)]
    wua, wub, wo, wg, wu, wd = wts
    return pl.pallas_call(
        functools.partial(_post_kernel, ffn_chunk=ffn_chunk),
        grid=(T // tm,),
        in_specs=[row(D), _const_spec((1, D)), _const_spec((1, D)), row(ya.shape[1]), row(yb.shape[1]),
                  row(D), row(D), cs(wua), cs(wub), cs(wo), _const_spec((1, D)), _const_spec((1, D)),
                  cs(wg), cs(wu), cs(wd), _const_spec((1, D)), _const_spec((1, D))],
        out_specs=row(D),
        out_shape=jax.ShapeDtypeStruct((T, D), F32),
        compiler_params=pltpu.CompilerParams(dimension_semantics=("parallel",),
                                             vmem_limit_bytes=VMEM_LIMIT_BYTES),
        name="merge_ffn",
    )(x2, vec(ln_g), vec(ln_b), ya, yb, ga, gb, wua, wub, wo, vec(ln1_g), vec(ln1_b),
      wg, wu, wd, vec(ln2_g), vec(ln2_b))


def kernel(x, ln_in_g, ln_in_b, rel_bias, w_in, diff_lam_q1, diff_lam_k1, diff_lam_q2, diff_lam_k2, diff_subln_g, rwkv_mu, rwkv_w0, rwkv_w2, rwkv_a0, rwkv_a2, rwkv_g2, rwkv_k_k, rwkv_k_a, rwkv_r_k, rwkv_lnx_g, rwkv_lnx_b, w_up_a, w_up_b, w_out, ln1_g, ln1_b, ffn_w_gate, ffn_w_up, ffn_w_down, ln2_g, ln2_b):
    B, S, D = x.shape
    T = B * S
    width = rwkv_w0.shape[-1]
    n_qk = DIFF_HEADS * HEAD_PAIR
    n_rw = 3 * width + DECAY_RANK + ICLR_RANK + GATE_RANK
    sizes = (n_qk, n_qk, n_qk, n_rw, D, D)
    assert w_in.shape[0] == 1 and w_in.shape[2] == sum(sizes), "single-layer model expected"
    x2 = x.reshape(T, D)
    q, k, v, rw, ga, gb = _inproj(x2, ln_in_g, ln_in_b, w_in[0].astype(BF16), sizes)
    lam_vecs = jnp.concatenate([diff_lam_q1, diff_lam_k1, diff_lam_q2, diff_lam_k2], 0).astype(F32)
    ya = _attention(q.reshape(B, S, n_qk), k.reshape(B, S, n_qk), v.reshape(B, S, n_qk),
                    rel_bias, lam_vecs, diff_subln_g[0])
    yb = _rwkv(rw.reshape(B, S, n_rw), rwkv_mu[0], rwkv_w0[0], rwkv_w2[0], rwkv_a0[0], rwkv_a2[0],
               rwkv_g2[0], rwkv_k_k[0], rwkv_k_a[0], rwkv_r_k[0], rwkv_lnx_g[0], rwkv_lnx_b[0])
    out = _post(x2, ln_in_g, ln_in_b, ya.reshape(T, n_qk), yb.reshape(T, width), ga, gb,
                w_up_a[0], w_up_b[0], w_out[0], ln1_g[0], ln1_b[0],
                ffn_w_gate[0], ffn_w_up[0], ffn_w_down[0], ln2_g[0], ln2_b[0])
    return out.reshape(B, S, D)
```

```python
import functools
import math

import jax
import jax.numpy as jnp
from jax import lax
from jax.experimental import pallas as pl
from jax.experimental.pallas import tpu as pltpu

F32 = jnp.float32
BF16 = jnp.bfloat16

DIFF_HEADS = 4
DIFF_HEAD_DIM = 64
HEAD_PAIR = 2 * DIFF_HEAD_DIM
RWKV_HEAD_DIM = 64
DECAY_RANK = 64
ICLR_RANK = 64
GATE_RANK = 128
NUM_BUCKETS = 32
MAX_DISTANCE = 128
LN_EPS = 1e-5
LNX_EPS = 64e-5
NEG_BIG = -1e30
DEPTH = 1
DEEPNORM_ALPHA = (2.0 * DEPTH) ** 0.25
LAMBDA_INIT = 0.8 - 0.6 * math.exp(-0.3 * 0)
LOG2_E = math.log2(math.e)

LANES = 128
VMEM_LIMIT_BYTES = 56 * 1024 * 1024
ROW_TILE = 256
ATTN_TILE = 256
SCAN_CHUNK = 64
SCAN_TILE = 256


def _mm(a, b):
    return jnp.dot(a.astype(BF16), b.astype(BF16), preferred_element_type=F32)


def _mm_nt(a, b):
    return lax.dot_general(a.astype(BF16), b.astype(BF16), (((1,), (1,)), ((), ())),
                           preferred_element_type=F32)


def _mm_tn(a, b):
    return lax.dot_general(a.astype(BF16), b.astype(BF16), (((0,), (0,)), ((), ())),
                           preferred_element_type=F32)


def _split2(x):
    hi = x.astype(BF16)
    lo = (x - hi.astype(F32)).astype(BF16)
    return hi, lo


def _sel_mm(sel, x):
    hi, lo = _split2(x)
    d = functools.partial(jnp.dot, preferred_element_type=F32)
    return d(sel, hi) + d(sel, lo)


def _head_sum(x, sel):
    return jnp.dot(x.astype(BF16), sel, preferred_element_type=F32)


def _layer_norm(x, g, b, eps=LN_EPS):
    mu = jnp.mean(x, -1, keepdims=True)
    xc = x - mu
    var = jnp.mean(xc * xc, -1, keepdims=True)
    return xc * lax.rsqrt(var + eps) * g + b


def _softplus(x):
    return jnp.maximum(x, 0.0) + jnp.log1p(jnp.exp(-jnp.abs(x)))


def _const_spec(shape):
    return pl.BlockSpec(shape, lambda *_: (0,) * len(shape), pipeline_mode=pl.Buffered(1))


def _inproj_kernel(x_ref, g_ref, b_ref, w_ref, q_ref, k_ref, v_ref, rw_ref, ga_ref, gb_ref,
                   *, splits):
    h = _layer_norm(x_ref[...], g_ref[...], b_ref[...]).astype(BF16)

    def proj(lo, hi):
        return jnp.dot(h, w_ref[:, lo:hi], preferred_element_type=F32)

    s_q, s_k, s_v, s_rw, s_ga = splits
    q_ref[...] = (proj(0, s_q) * (DIFF_HEAD_DIM ** -0.5 * LOG2_E)).astype(BF16)
    k_ref[...] = proj(s_q, s_k).astype(BF16)
    v_ref[...] = proj(s_k, s_v).astype(BF16)
    rw_ref[...] = proj(s_v, s_rw)
    ga_ref[...] = jax.nn.sigmoid(proj(s_rw, s_ga)).astype(BF16)
    gb_ref[...] = jax.nn.sigmoid(proj(s_ga, w_ref.shape[1])).astype(BF16)


def _inproj(x2, ln_g, ln_b, w_in_bf, sizes):
    T, D = x2.shape
    n_q, n_k, n_v, n_rw, n_ga, n_gb = sizes
    acc, cuts = 0, []
    for s in sizes[:-1]:
        acc += s
        cuts.append(acc)
    splits = tuple(cuts)
    tm = ROW_TILE
    row = lambda n: pl.BlockSpec((tm, n), lambda i: (i, 0))
    return pl.pallas_call(
        functools.partial(_inproj_kernel, splits=splits),
        grid=(T // tm,),
        in_specs=[row(D), _const_spec((1, D)), _const_spec((1, D)), _const_spec(w_in_bf.shape)],
        out_specs=[row(n_q), row(n_k), row(n_v), row(n_rw), row(n_ga), row(n_gb)],
        out_shape=[jax.ShapeDtypeStruct((T, n_q), BF16), jax.ShapeDtypeStruct((T, n_k), BF16),
                   jax.ShapeDtypeStruct((T, n_v), BF16), jax.ShapeDtypeStruct((T, n_rw), F32),
                   jax.ShapeDtypeStruct((T, n_ga), BF16), jax.ShapeDtypeStruct((T, n_gb), BF16)],
        compiler_params=pltpu.CompilerParams(dimension_semantics=("parallel",),
                                             vmem_limit_bytes=VMEM_LIMIT_BYTES),
        name="ln_inproj",
    )(x2, ln_g.reshape(1, D), ln_b.reshape(1, D), w_in_bf)


def _attn_kernel(q_ref, k_ref, v_ref, bdiag_ref, bnear_ref, bfar_ref, lam_ref, g_ref, o_ref,
                 *, nq, t):
    lane = lax.broadcasted_iota(jnp.int32, (t, HEAD_PAIR), 1)
    lo = lane < DIFF_HEAD_DIM
    lv = lam_ref[...]
    lam = (jnp.exp(jnp.sum(lv[0:1] * lv[1:2], keepdims=True))
           - jnp.exp(jnp.sum(lv[2:3] * lv[3:4], keepdims=True)) + LAMBDA_INIT)
    bfar = bfar_ref[...][:, 0:1]
    zero = jnp.zeros((), BF16)
    rowmax = lambda a: jnp.max(a, -1, keepdims=True)
    rowsum = lambda a: jnp.sum(a, -1, keepdims=True)

    def logits(qi):
        q = q_ref[qi * t:(qi + 1) * t, :]
        qs = jnp.concatenate([jnp.where(lo, q, zero), jnp.where(lo, zero, q)], axis=0)
        return lax.dot_general(qs, k_ref[0:(qi + 1) * t, :], (((1,), (1,)), ((), ())),
                               preferred_element_type=F32)

    def softmax(qi, s):
        s_diag = s[:, qi * t:] + bdiag_ref[...]
        m = rowmax(s_diag)
        if qi >= 1:
            s_near = s[:, (qi - 1) * t:qi * t] + bnear_ref[...]
            m = jnp.maximum(m, rowmax(s_near))
        if qi >= 2:
            s_far = s[:, :(qi - 1) * t]
            m = jnp.maximum(m, rowmax(s_far) + bfar)
        parts = []
        if qi >= 2:
            parts.append(jnp.exp2(s_far - (m - bfar)))
        if qi >= 1:
            parts.append(jnp.exp2(s_near - m))
        parts.append(jnp.exp2(s_diag - m))
        l = sum(rowsum(p) for p in parts)
        w1 = 1.0 / l[:t]
        w2 = lam / l[t:]
        return jnp.concatenate([(x[:t] * w1 - x[t:] * w2).astype(BF16) for x in parts], axis=1)

    def values(qi, p):
        o = jnp.dot(p, v_ref[0:(qi + 1) * t, :], preferred_element_type=F32)
        ms = jnp.mean(o * o, -1, keepdims=True)
        o = o * lax.rsqrt(ms + LN_EPS) * g_ref[...] * (1.0 - LAMBDA_INIT)
        o_ref[qi * t:(qi + 1) * t, :] = o.astype(o_ref.dtype)

    s_q, p_q = {}, {}
    for step in range(nq + 2):
        if step < nq:
            s_q[step] = logits(step)
        if 0 <= step - 1 < nq:
            p_q[step - 1] = softmax(step - 1, s_q.pop(step - 1))
        if 0 <= step - 2 < nq:
            values(step - 2, p_q.pop(step - 2))


def _t5_bucket(dist):
    max_exact = NUM_BUCKETS // 2
    d = jnp.maximum(dist, 1).astype(F32)
    large = max_exact + (jnp.log(d / max_exact) / math.log(MAX_DISTANCE / max_exact)
                         * (NUM_BUCKETS - max_exact)).astype(jnp.int32)
    large = jnp.minimum(large, NUM_BUCKETS - 1)
    return jnp.where(dist < max_exact, dist, large)


def _attention(q, k, v, rel_bias, lam_vecs, subln_g):
    B, S, _ = q.shape
    t = ATTN_TILE
    assert S % t == 0 and t >= MAX_DISTANCE
    nq = S // t
    dd = jnp.arange(t, dtype=jnp.int32)[:, None] - jnp.arange(t, dtype=jnp.int32)[None, :]
    buckets = jnp.arange(NUM_BUCKETS, dtype=jnp.int32)

    def bias_tile(dist):
        onehot = (_t5_bucket(dist)[..., None] == buckets).astype(F32)
        return jnp.einsum("ijb,bh->hij", onehot, rel_bias.astype(F32),
                          precision=lax.Precision.HIGHEST)

    bdiag = jnp.where(dd >= 0, bias_tile(jnp.maximum(dd, 0)) * LOG2_E, NEG_BIG)
    bnear = bias_tile(dd + t) * LOG2_E
    bdiag = jnp.concatenate([bdiag, bdiag], axis=1)
    bnear = jnp.concatenate([bnear, bnear], axis=1)
    bfar = jnp.broadcast_to((rel_bias[NUM_BUCKETS - 1] * LOG2_E)[:, None, None],
                            (DIFF_HEADS, 1, LANES))
    seq = pl.BlockSpec((None, S, HEAD_PAIR), lambda b, h: (b, 0, h))
    per_head = lambda r, c: pl.BlockSpec((None, r, c), lambda b, h: (h, 0, 0))
    return pl.pallas_call(
        functools.partial(_attn_kernel, nq=nq, t=t),
        grid=(B, DIFF_HEADS),
        in_specs=[seq, seq, seq, per_head(2 * t, t), per_head(2 * t, t), per_head(1, LANES),
                  pl.BlockSpec((4, DIFF_HEAD_DIM), lambda b, h: (0, 0)),
                  pl.BlockSpec((1, HEAD_PAIR), lambda b, h: (0, 0))],
        out_specs=seq,
        out_shape=jax.ShapeDtypeStruct((B, S, DIFF_HEADS * HEAD_PAIR), BF16),
        compiler_params=pltpu.CompilerParams(dimension_semantics=("parallel", "parallel"),
                                             vmem_limit_bytes=VMEM_LIMIT_BYTES),
        name="diff_attention",
    )(q, k, v, bdiag, bnear, bfar, lam_vecs, subln_g.reshape(1, HEAD_PAIR))


def _rwkv_kernel(rw_ref, mu_ref, w0_ref, w2_ref, a0_ref, a2_ref, g2_ref, kk_ref, ka_ref, rk_ref,
                 lg_ref, lb_ref, tri_ref, hsum_ref, o_ref,
                 h_s, prev_s, ah_s, rh_s, bt_s, kt_s, bb_s, kb_s, v_s, gl_s, y_s, bonus_s, gate_s,
                 *, width, n_pairs):
    L = SCAN_CHUNK
    TL = rw_ref.shape[0]
    j = pl.program_id(1)

    @pl.when(j == 0)
    def _():
        h_s[...] = jnp.zeros_like(h_s)
        prev_s[...] = jnp.zeros_like(prev_s)

    p = rw_ref[...]
    row = lax.broadcasted_iota(jnp.int32, (TL, 1), 0)
    p_prev = jnp.where(row == 0, prev_s[0:1, :], pltpu.roll(p, 1, 0))
    prev_s[0:1, :] = p[TL - 1:TL, :]
    ps = p + (p_prev - p) * mu_ref[...]
    r = ps[:, 0:width]
    k = ps[:, width:2 * width]
    v = ps[:, 2 * width:3 * width]
    lr = ps[:, 3 * width:3 * width + DECAY_RANK + ICLR_RANK]
    gl = ps[:, 3 * width + DECAY_RANK + ICLR_RANK:]
    hsum = hsum_ref[...]
    z = w0_ref[...] + _mm(jnp.tanh(lr), w2_ref[...])
    logw = -jnp.exp(-_softplus(-z) - 0.5)
    iclr = jax.nn.sigmoid(a0_ref[...] + _mm(lr, a2_ref[...]))
    gate_s[...] = _mm(jax.nn.sigmoid(gl), g2_ref[...])
    kk = k * kk_ref[...]
    kk = kk * lax.rsqrt(jnp.maximum(_head_sum(kk * kk, hsum), 1e-24))
    k2 = k * (1.0 + (iclr - 1.0) * ka_ref[...])
    bonus_s[...] = _head_sum(r * k2 * rk_ref[...], hsum) * v
    c = _sel_mm(tri_ref[...], logw)
    c_end = jnp.concatenate(
        [jnp.broadcast_to(c[i * L + L - 1:i * L + L, :], (L, width)) for i in range(TL // L)], 0)
    b = kk * iclr
    inv_g = jnp.exp(-c)
    to_end = jnp.exp(c_end - c)
    ah_s[...] = -kk * jnp.exp(c - logw)
    rh_s[...] = r * jnp.exp(c)
    bt_s[...] = b * inv_g
    kt_s[...] = k2 * inv_g
    bb_s[...] = b * to_end
    kb_s[...] = k2 * to_end
    v_s[...] = v
    gl_s[...] = jnp.exp(c_end)

    lane = lax.broadcasted_iota(jnp.int32, (L, LANES), 1)
    m0 = lane < RWKV_HEAD_DIM
    m1 = jnp.logical_not(m0)
    ri = lax.broadcasted_iota(jnp.int32, (2 * L, LANES), 0)
    ci = lax.broadcasted_iota(jnp.int32, (2 * L, LANES), 1)
    same_head = (ri // L) == (ci // L)
    p_mask = same_head & ((ci % L) < (ri % L))
    eye = ri == ci
    rl = lax.broadcasted_iota(jnp.int32, (L, LANES), 0)
    tril2 = (lane % L) <= rl
    stril2 = (lane % L) < rl
    eye_f = jnp.where(eye, 1.0, 0.0).astype(F32)
    zeros = jnp.zeros((L, LANES), F32)

    sel = lambda m, x: jnp.where(m, x, zeros)
    cat0 = lambda *xs: jnp.concatenate(xs, 0)
    n_chunks = TL // L
    blocks = [(c, pi) for c in range(n_chunks) for pi in range(n_pairs)]
    ld = lambda ref, c, pi: ref[c * L:(c + 1) * L, pi * LANES:(pi + 1) * LANES]

    ah = [ld(ah_s, *b_) for b_ in blocks]
    rh = [ld(rh_s, *b_) for b_ in blocks]
    bt = [ld(bt_s, *b_) for b_ in blocks]
    kt = [ld(kt_s, *b_) for b_ in blocks]
    vv = [ld(v_s, *b_) for b_ in blocks]
    ah0 = [sel(m0, x) for x in ah]
    ah1 = [sel(m1, x) for x in ah]
    v0 = [sel(m0, x) for x in vv]
    v1 = [sel(m1, x) for x in vv]
    g0 = [_mm_nt(cat0(ah0[i], sel(m0, rh[i])), cat0(bt[i], kt[i])) for i in range(len(blocks))]
    g1 = [_mm_nt(cat0(ah1[i], sel(m1, rh[i])), cat0(kt[i], bt[i])) for i in range(len(blocks))]
    gb0 = [sel(tril2, x[L:]) for x in g0]
    gb1 = [sel(tril2, x[L:]) for x in g1]
    pw = [jnp.where(p_mask, cat0(x0[:L], x1[:L]), 0.0) for x0, x1 in zip(g0, g1)]
    tinv = [eye_f + x for x in pw]
    for _ in range(int(math.log2(L)) - 1):
        pwb = [x.astype(BF16) for x in pw]
        pw = [jnp.dot(x, x, preferred_element_type=F32) for x in pwb]
        tinv = [t + _mm(x, t) for x, t in zip(pw, tinv)]
    makv = [_mm(jnp.concatenate([sel(m1 & stril2, x0[:L]), sel(m0 & stril2, x1[:L])], 1),
                cat0(a, a, b_, b_)) for x0, x1, a, b_ in zip(g0, g1, v0, v1)]
    w = [_mm(t, jnp.concatenate([cat0(a0_, a1_), cat0(sel(m0, mk), sel(m1, mk))], 1))
         for t, a0_, a1_, mk in zip(tinv, ah0, ah1, makv)]
    w1 = [x[:L, :LANES] + x[L:, :LANES] for x in w]
    w2 = [x[:L, LANES:] + x[L:, LANES:] for x in w]

    hs = [h_s[pi] for pi in range(n_pairs)]
    for c in range(n_chunks):
        for pi in range(n_pairs):
            i = c * n_pairs + pi
            h = hs[pi]
            u = _mm(w1[i], h) + w2[i]
            s0 = cat0(sel(m0, u), v0[i])
            s1 = cat0(v1[i], sel(m1, u))
            y = _mm(jnp.concatenate([rh[i], gb0[i], gb1[i]], 1), cat0(h, s0, s1))
            hn = _mm_tn(cat0(ld(bb_s, c, pi), ld(kb_s, c, pi)), cat0(u, vv[i]))
            g_end = ld(gl_s, c, pi)[0:1, :]
            g_col = jnp.sum(jnp.where(eye, jnp.broadcast_to(g_end, (2 * L, LANES)), 0.0),
                            axis=1, keepdims=True)
            hs[pi] = g_col * h + jnp.where(same_head, hn, 0.0)
            y_s[c * L:(c + 1) * L, pi * LANES:(pi + 1) * LANES] = y
    for pi in range(n_pairs):
        h_s[pi] = hs[pi]

    y = y_s[...]
    inv_n = 1.0 / RWKV_HEAD_DIM
    mean = _head_sum(y, hsum) * inv_n
    yc = y - mean
    var = _head_sum(yc * yc, hsum) * inv_n
    yn = yc * lax.rsqrt(var + LNX_EPS) * lg_ref[...] + lb_ref[...]
    o_ref[...] = ((yn + bonus_s[...]) * gate_s[...]).astype(o_ref.dtype)


def _rwkv(rw, mu, w0, w2, a0, a2, g2, k_k, k_a, r_k, lnx_g, lnx_b):
    B, S, n_rw = rw.shape
    width = w0.shape[-1]
    n_pairs = width // LANES
    L, TL = SCAN_CHUNK, SCAN_TILE
    assert S % TL == 0 and n_rw == 3 * width + DECAY_RANK + ICLR_RANK + GATE_RANK
    assert DECAY_RANK + ICLR_RANK == LANES
    ti = jnp.arange(TL)
    same_chunk = (ti[:, None] // L) == (ti[None, :] // L)
    tri = (same_chunk & (ti[None, :] <= ti[:, None])).astype(BF16)
    ch =jnp.arange(width) // RWKV_HEAD_DIM
    hsum = (ch[:, None] == ch[None, :]).astype(BF16)
    w2p = jnp.concatenate([w2, jnp.zeros((ICLR_RANK, width), w2.dtype)], 0).astype(BF16)
    a2p = jnp.concatenate([jnp.zeros((DECAY_RANK, width), a2.dtype), a2], 0).astype(BF16)
    vec = lambda a: a.reshape(1, -1).astype(F32)
    tile = pl.BlockSpec((None, TL, n_rw), lambda b, j: (b, j, 0))
    feat = lambda: pltpu.VMEM((TL, width), F32)
    return pl.pallas_call(
        functools.partial(_rwkv_kernel, width=width, n_pairs=n_pairs),
        grid=(B, S // TL),
        in_specs=[tile, _const_spec((1, n_rw)), _const_spec((1, width)), _const_spec((LANES, width)),
                  _const_spec((1, width)), _const_spec((LANES, width)), _const_spec((GATE_RANK, width)),
                  _const_spec((1, width)), _const_spec((1, width)), _const_spec((1, width)),
                  _const_spec((1, width)), _const_spec((1, width)),
                  _const_spec((TL, TL)), _const_spec((width, width))],
        out_specs=pl.BlockSpec((None, TL, width), lambda b, j: (b, j, 0)),
        out_shape=jax.ShapeDtypeStruct((B, S, width), BF16),
        scratch_shapes=[pltpu.VMEM((n_pairs, LANES, LANES), F32), pltpu.VMEM((8, n_rw), F32)]
                       + [feat() for _ in range(11)],
        compiler_params=pltpu.CompilerParams(dimension_semantics=("parallel", "arbitrary"),
                                             vmem_limit_bytes=VMEM_LIMIT_BYTES),
        name="rwkv7_chunked",
    )(rw, vec(mu), vec(w0), w2p, vec(a0), a2p, g2.astype(BF16), vec(k_k), vec(k_a), vec(r_k),
      vec(lnx_g), vec(lnx_b), tri, hsum)


def _post_kernel(x_ref, lng_ref, lnb_ref, ya_ref, yb_ref, ga_ref, gb_ref, wua_ref, wub_ref, wo_ref,
                 l1g_ref, l1b_ref, wg_ref, wu_ref, wd_ref, l2g_ref, l2b_ref, o_ref, *, ffn_chunk):
    dot = functools.partial(jnp.dot, preferred_element_type=F32)
    h = _layer_norm(x_ref[...], lng_ref[...], lnb_ref[...])
    merged = (ga_ref[...].astype(F32) * dot(ya_ref[...], wua_ref[...])
              + gb_ref[...].astype(F32) * dot(yb_ref[...], wub_ref[...]))
    mix = dot(merged.astype(BF16), wo_ref[...])
    h1 = _layer_norm(DEEPNORM_ALPHA * h + mix, l1g_ref[...], l1b_ref[...])
    h1b = h1.astype(BF16)
    hidden = wg_ref.shape[1]
    ffn = jnp.zeros_like(h1)
    for c0 in range(0, hidden, ffn_chunk):
        gate = dot(h1b, wg_ref[:, c0:c0 + ffn_chunk])
        up = dot(h1b, wu_ref[:, c0:c0 + ffn_chunk])
        act = (gate * jax.nn.sigmoid(gate) * up).astype(BF16)
        ffn = ffn + dot(act, wd_ref[c0:c0 + ffn_chunk, :])
    o_ref[...] = _layer_norm(DEEPNORM_ALPHA * h1 + ffn, l2g_ref[...], l2b_ref[...])


def _post(x2, ln_g, ln_b, ya, yb, ga, gb, w_up_a, w_up_b, w_out, ln1_g, ln1_b,
          w_gate, w_up, w_down, ln2_g, ln2_b):
    T, D = x2.shape
    hidden = w_gate.shape[1]
    ffn_chunk = hidden // 2
    assert ffn_chunk % LANES == 0
    tm = ROW_TILE
    row = lambda n: pl.BlockSpec((tm, n), lambda i: (i, 0))
    vec = lambda a: a.reshape(1, D)
    cs = lambda a: _const_spec(a.shape)
    wts = [w.astype(BF16) for w in (w_up_a, w_up_b, w_out, w_gate, w_up, w_down)]
    wua, wub, wo, wg, wu, wd = wts
    return pl.pallas_call(
        functools.partial(_post_kernel, ffn_chunk=ffn_chunk),
        grid=(T // tm,),
        in_specs=[row(D), _const_spec((1, D)), _const_spec((1, D)), row(ya.shape[1]), row(yb.shape[1]),
                  row(D), row(D), cs(wua), cs(wub), cs(wo), _const_spec((1, D)), _const_spec((1, D)),
                  cs(wg), cs(wu), cs(wd), _const_spec((1, D)), _const_spec((1, D))],
        out_specs=row(D),
        out_shape=jax.ShapeDtypeStruct((T, D), F32),
        compiler_params=pltpu.CompilerParams(dimension_semantics=("parallel",),
                                             vmem_limit_bytes=VMEM_LIMIT_BYTES),
        name="merge_ffn",
    )(x2, vec(ln_g), vec(ln_b), ya, yb, ga, gb, wua, wub, wo, vec(ln1_g), vec(ln1_b),
      wg, wu, wd, vec(ln2_g), vec(ln2_b))


def kernel(x, ln_in_g, ln_in_b, rel_bias, w_in, diff_lam_q1, diff_lam_k1, diff_lam_q2, diff_lam_k2, diff_subln_g, rwkv_mu, rwkv_w0, rwkv_w2, rwkv_a0, rwkv_a2, rwkv_g2, rwkv_k_k, rwkv_k_a, rwkv_r_k, rwkv_lnx_g, rwkv_lnx_b, w_up_a, w_up_b, w_out, ln1_g, ln1_b, ffn_w_gate, ffn_w_up, ffn_w_down, ln2_g, ln2_b):
    B, S, D = x.shape
    T = B * S
    width = rwkv_w0.shape[-1]
    n_qk = DIFF_HEADS * HEAD_PAIR
    n_rw = 3 * width + DECAY_RANK + ICLR_RANK + GATE_RANK
    sizes = (n_qk, n_qk, n_qk, n_rw, D, D)
    assert w_in.shape[0] == 1 and w_in.shape[2] == sum(sizes), "single-layer model expected"
    x2 = x.reshape(T, D)
    q, k, v, rw, ga, gb = _inproj(x2, ln_in_g, ln_in_b, w_in[0].astype(BF16), sizes)
    lam_vecs = jnp.concatenate([diff_lam_q1, diff_lam_k1, diff_lam_q2, diff_lam_k2], 0).astype(F32)
    ya = _attention(q.reshape(B, S, n_qk), k.reshape(B, S, n_qk), v.reshape(B, S, n_qk),
                    rel_bias, lam_vecs, diff_subln_g[0])
    yb = _rwkv(rw.reshape(B, S, n_rw), rwkv_mu[0], rwkv_w0[0], rwkv_w2[0], rwkv_a0[0], rwkv_a2[0],
               rwkv_g2[0], rwkv_k_k[0], rwkv_k_a[0], rwkv_r_k[0], rwkv_lnx_g[0], rwkv_lnx_b[0])
    out = _post(x2, ln_in_g, ln_in_b, ya.reshape(T, n_qk), yb.reshape(T, width), ga, gb,
                w_up_a[0], w_up_b[0], w_out[0], ln1_g[0], ln1_b[0],
                ffn_w_gate[0], ffn_w_up[0], ffn_w_down[0], ln2_g[0], ln2_b[0])
    return out.reshape(B, S, D)
```

```python
import functools
import math

import jax
import jax.numpy as jnp
from jax import lax
from jax.experimental import pallas as pl
from jax.experimental.pallas import tpu as pltpu

F32 = jnp.float32
BF16 = jnp.bfloat16

DIFF_HEADS = 4
DIFF_HEAD_DIM = 64
HEAD_PAIR = 2 * DIFF_HEAD_DIM
RWKV_HEAD_DIM = 64
DECAY_RANK = 64
ICLR_RANK = 64
GATE_RANK = 128
NUM_BUCKETS = 32
MAX_DISTANCE = 128
LN_EPS = 1e-5
LNX_EPS = 64e-5
NEG_BIG = -1e30
DEPTH = 1
DEEPNORM_ALPHA = (2.0 * DEPTH) ** 0.25
LAMBDA_INIT = 0.8 - 0.6 * math.exp(-0.3 * 0)
LOG2_E = math.log2(math.e)

LANES = 128
VMEM_LIMIT_BYTES = 56 * 1024 * 1024
ROW_TILE = 256
ATTN_TILE = 256
SCAN_CHUNK = 64
SCAN_TILE = 512


def _mm(a, b):
    return jnp.dot(a.astype(BF16), b.astype(BF16), preferred_element_type=F32)


def _mm_nt(a, b):
    return lax.dot_general(a.astype(BF16), b.astype(BF16), (((1,), (1,)), ((), ())),
                           preferred_element_type=F32)


def _mm_tn(a, b):
    return lax.dot_general(a.astype(BF16), b.astype(BF16), (((0,), (0,)), ((), ())),
                           preferred_element_type=F32)


def _split2(x):
    hi = x.astype(BF16)
    lo = (x - hi.astype(F32)).astype(BF16)
    return hi, lo


def _sel_mm(sel, x):
    hi, lo = _split2(x)
    d = functools.partial(jnp.dot, preferred_element_type=F32)
    return d(sel, hi) + d(sel, lo)


def _head_sum(x, sel):
    xb = x.astype(BF16)
    return jnp.concatenate(
        [jnp.dot(xb[:, c0:c0 + LANES], sel, preferred_element_type=F32)
         for c0 in range(0, x.shape[1], LANES)], axis=1)


def _layer_norm(x, g, b, eps=LN_EPS):
    mu = jnp.mean(x, -1, keepdims=True)
    xc = x - mu
    var = jnp.mean(xc * xc, -1, keepdims=True)
    return xc * lax.rsqrt(var + eps) * g + b


def _softplus(x):
    return jnp.maximum(x, 0.0) + jnp.log1p(jnp.exp(-jnp.abs(x)))


def _const_spec(shape):
    return pl.BlockSpec(shape, lambda *_: (0,) * len(shape), pipeline_mode=pl.Buffered(1))


def _inproj_kernel(x_ref, g_ref, b_ref, w_ref, q_ref, k_ref, v_ref, rw_ref, ga_ref, gb_ref,
                   *, splits):
    h = _layer_norm(x_ref[...], g_ref[...], b_ref[...]).astype(BF16)

    def proj(lo, hi):
        return jnp.dot(h, w_ref[:, lo:hi], preferred_element_type=F32)

    s_q, s_k, s_v, s_rw, s_ga = splits
    q_ref[...] = (proj(0, s_q) * (DIFF_HEAD_DIM ** -0.5 * LOG2_E)).astype(BF16)
    k_ref[...] = proj(s_q, s_k).astype(BF16)
    v_ref[...] = proj(s_k, s_v).astype(BF16)
    rw_ref[...] = proj(s_v, s_rw)
    ga_ref[...] = jax.nn.sigmoid(proj(s_rw, s_ga)).astype(BF16)
    gb_ref[...] = jax.nn.sigmoid(proj(s_ga, w_ref.shape[1])).astype(BF16)


def _inproj(x2, ln_g, ln_b, w_in_bf, sizes):
    T, D = x2.shape
    n_q, n_k, n_v, n_rw, n_ga, n_gb = sizes
    acc, cuts = 0, []
    for s in sizes[:-1]:
        acc += s
        cuts.append(acc)
    splits = tuple(cuts)
    tm = ROW_TILE
    row = lambda n: pl.BlockSpec((tm, n), lambda i: (i, 0))
    return pl.pallas_call(
        functools.partial(_inproj_kernel, splits=splits),
        grid=(T // tm,),
        in_specs=[row(D), _const_spec((1, D)), _const_spec((1, D)), _const_spec(w_in_bf.shape)],
        out_specs=[row(n_q), row(n_k), row(n_v), row(n_rw), row(n_ga), row(n_gb)],
        out_shape=[jax.ShapeDtypeStruct((T, n_q), BF16), jax.ShapeDtypeStruct((T, n_k), BF16),
                   jax.ShapeDtypeStruct((T, n_v), BF16), jax.ShapeDtypeStruct((T, n_rw), F32),
                   jax.ShapeDtypeStruct((T, n_ga), BF16), jax.ShapeDtypeStruct((T, n_gb), BF16)],
        compiler_params=pltpu.CompilerParams(dimension_semantics=("parallel",),
                                             vmem_limit_bytes=VMEM_LIMIT_BYTES),
        name="ln_inproj",
    )(x2, ln_g.reshape(1, D), ln_b.reshape(1, D), w_in_bf)


def _attn_kernel(q_ref, k_ref, v_ref, bdiag_ref, bnear_ref, bfar_ref, lam_ref, g_ref, o_ref,
                 *, nq, t):
    lane = lax.broadcasted_iota(jnp.int32, (t, HEAD_PAIR), 1)
    lo = lane < DIFF_HEAD_DIM
    lv = lam_ref[...]
    lam = (jnp.exp(jnp.sum(lv[0:1] * lv[1:2], keepdims=True))
           - jnp.exp(jnp.sum(lv[2:3] * lv[3:4], keepdims=True)) + LAMBDA_INIT)
    bfar = bfar_ref[...][:, 0:1]
    zero = jnp.zeros((), BF16)
    rowmax = lambda a: jnp.max(a, -1, keepdims=True)
    rowsum = lambda a: jnp.sum(a, -1, keepdims=True)

    def logits(qi):
        q = q_ref[qi * t:(qi + 1) * t, :]
        qs = jnp.concatenate([jnp.where(lo, q, zero), jnp.where(lo, zero, q)], axis=0)
        return lax.dot_general(qs, k_ref[0:(qi + 1) * t, :], (((1,), (1,)), ((), ())),
                               preferred_element_type=F32)

    def softmax(qi, s):
        s_diag = s[:, qi * t:] + bdiag_ref[...]
        m = rowmax(s_diag)
        if qi >= 1:
            s_near = s[:, (qi - 1) * t:qi * t] + bnear_ref[...]
            m = jnp.maximum(m, rowmax(s_near))
        if qi >= 2:
            s_far = s[:, :(qi - 1) * t]
            m = jnp.maximum(m, rowmax(s_far) + bfar)
        parts = []
        if qi >= 2:
            parts.append(jnp.exp2(s_far - (m - bfar)))
        if qi >= 1:
            parts.append(jnp.exp2(s_near - m))
        parts.append(jnp.exp2(s_diag - m))
        l = sum(rowsum(p) for p in parts)
        w1 = 1.0 / l[:t]
        w2 = lam / l[t:]
        return jnp.concatenate([(x[:t] * w1 - x[t:] * w2).astype(BF16) for x in parts], axis=1)

    def values(qi, p):
        o = jnp.dot(p, v_ref[0:(qi + 1) * t, :], preferred_element_type=F32)
        ms = jnp.mean(o * o, -1, keepdims=True)
        o = o * lax.rsqrt(ms + LN_EPS) * g_ref[...] * (1.0 - LAMBDA_INIT)
        o_ref[qi * t:(qi + 1) * t, :] = o.astype(o_ref.dtype)

    s_q, p_q = {}, {}
    for step in range(nq + 2):
        if step < nq:
            s_q[step] = logits(step)
        if 0 <= step - 1 < nq:
            p_q[step - 1] = softmax(step - 1, s_q.pop(step - 1))
        if 0 <= step - 2 < nq:
            values(step - 2, p_q.pop(step - 2))


def _t5_bucket(dist):
    max_exact = NUM_BUCKETS // 2
    d = jnp.maximum(dist, 1).astype(F32)
    large = max_exact + (jnp.log(d / max_exact) / math.log(MAX_DISTANCE / max_exact)
                         * (NUM_BUCKETS - max_exact)).astype(jnp.int32)
    large = jnp.minimum(large, NUM_BUCKETS - 1)
    return jnp.where(dist < max_exact, dist, large)


def _attention(q, k, v, rel_bias, lam_vecs, subln_g):
    B, S, _ = q.shape
    t = ATTN_TILE
    assert S % t == 0 and t >= MAX_DISTANCE
    nq = S // t
    dd = jnp.arange(t, dtype=jnp.int32)[:, None] - jnp.arange(t, dtype=jnp.int32)[None, :]
    buckets = jnp.arange(NUM_BUCKETS, dtype=jnp.int32)

    def bias_tile(dist):
        onehot = (_t5_bucket(dist)[..., None] == buckets).astype(F32)
        return jnp.einsum("ijb,bh->hij", onehot, rel_bias.astype(F32),
                          precision=lax.Precision.HIGHEST)

    bdiag = jnp.where(dd >= 0, bias_tile(jnp.maximum(dd, 0)) * LOG2_E, NEG_BIG)
    bnear = bias_tile(dd + t) * LOG2_E
    bdiag = jnp.concatenate([bdiag, bdiag], axis=1)
    bnear = jnp.concatenate([bnear, bnear], axis=1)
    bfar = jnp.broadcast_to((rel_bias[NUM_BUCKETS - 1] * LOG2_E)[:, None, None],
                            (DIFF_HEADS, 1, LANES))
    seq = pl.BlockSpec((None, S, HEAD_PAIR), lambda b, h: (b, 0, h))
    per_head = lambda r, c: pl.BlockSpec((None, r, c), lambda b, h: (h, 0, 0))
    return pl.pallas_call(
        functools.partial(_attn_kernel, nq=nq, t=t),
        grid=(B, DIFF_HEADS),
        in_specs=[seq, seq, seq, per_head(2 * t, t), per_head(2 * t, t), per_head(1, LANES),
                  pl.BlockSpec((4, DIFF_HEAD_DIM), lambda b, h: (0, 0)),
                  pl.BlockSpec((1, HEAD_PAIR), lambda b, h: (0, 0))],
        out_specs=seq,
        out_shape=jax.ShapeDtypeStruct((B, S, DIFF_HEADS * HEAD_PAIR), BF16),
        compiler_params=pltpu.CompilerParams(dimension_semantics=("parallel", "parallel"),
                                             vmem_limit_bytes=VMEM_LIMIT_BYTES),
        name="diff_attention",
    )(q, k, v, bdiag, bnear, bfar, lam_vecs, subln_g.reshape(1, HEAD_PAIR))


def _rwkv_kernel(rw_ref, mu_ref, w0_ref, w2_ref, a0_ref, a2_ref, g2_ref, kk_ref, ka_ref, rk_ref,
                 lg_ref, lb_ref, tri_ref, hsum_ref, o_ref,
                 h_s, prev_s, ah_s, rh_s, bt_s, kt_s, bb_s, kb_s, v_s, gl_s, y_s, bonus_s, gate_s,
                 *, width, n_pairs):
    L = SCAN_CHUNK
    TL = rw_ref.shape[0]
    j = pl.program_id(1)

    @pl.when(j == 0)
    def _():
        h_s[...] = jnp.zeros_like(h_s)
        prev_s[...] = jnp.zeros_like(prev_s)

    HT = TL // 2
    n_chunks = HT // L
    n_blocks = n_chunks * n_pairs

    lane = lax.broadcasted_iota(jnp.int32, (L, LANES), 1)
    m0 = lane < RWKV_HEAD_DIM
    m1 = jnp.logical_not(m0)
    ri = lax.broadcasted_iota(jnp.int32, (2 * L, LANES), 0)
    ci = lax.broadcasted_iota(jnp.int32, (2 * L, LANES), 1)
    same_head = (ri // L) == (ci // L)
    p_mask = same_head & ((ci % L) < (ri % L))
    eye = ri == ci
    rl = lax.broadcasted_iota(jnp.int32, (L, LANES), 0)
    tril2 = (lane % L) <= rl
    stril2 = (lane % L) < rl
    eye_f = jnp.where(eye, 1.0, 0.0).astype(F32)
    zeros = jnp.zeros((L, LANES), F32)

    sel = lambda m, x: jnp.where(m, x, zeros)
    cat0 = lambda *xs: jnp.concatenate(xs, 0)
    cat1 = lambda *xs: jnp.concatenate(xs, 1)
    row_in_half = lax.broadcasted_iota(jnp.int32, (HT, 1), 0)
    st = [dict(), dict()]
    hs = [h_s[pi] for pi in range(n_pairs)]

    def rows_of(t):
        return slice(t * HT, (t + 1) * HT)

    def block_ld(ref, t, i):
        c, pi = divmod(i, n_pairs)
        r0 = t * HT + c * L
        return ref[r0:r0 + L, pi * LANES:(pi + 1) * LANES]

    def prep_a(t):
        d, rows = st[t], rows_of(t)
        p = rw_ref[rows, :]
        before = prev_s[0:1, :] if t == 0 else rw_ref[t * HT - 1:t * HT, :]
        p_prev = jnp.where(row_in_half == 0, before, pltpu.roll(p, 1, 0))
        ps = p + (p_prev - p) * mu_ref[...]
        d["r"], d["k"], d["v"] = ps[:, 0:width], ps[:, width:2 * width], ps[:, 2 * width:3 * width]
        lr = ps[:, 3 * width:3 * width + DECAY_RANK + ICLR_RANK]
        gl = ps[:, 3 * width + DECAY_RANK + ICLR_RANK:]
        d["z"] = w0_ref[...] + _mm(jnp.tanh(lr), w2_ref[...])
        d["ai"] = a0_ref[...] + _mm(lr, a2_ref[...])
        gate_s[rows, :] = _mm(jax.nn.sigmoid(gl), g2_ref[...])
        d["kk"] = d["k"] * kk_ref[...]
        d["ss"] = _head_sum(d["kk"] * d["kk"], hsum_ref[...])

    def prep_b(t):
        d, rows = st[t], rows_of(t)
        d["logw"] = -jnp.exp(-_softplus(-d.pop("z")) - 0.5)
        iclr = jax.nn.sigmoid(d.pop("ai"))
        kk = d["kk"] * lax.rsqrt(jnp.maximum(d.pop("ss"), 1e-24))
        d["kk"] = kk
        d["b"] = kk * iclr
        d["k2"] = d.pop("k") * (1.0 + (iclr - 1.0) * ka_ref[...])
        bonus_s[rows, :] = _head_sum(d["r"] * d["k2"] * rk_ref[...], hsum_ref[...]) * d["v"]
        d["c"] = _sel_mm(tri_ref[...], d["logw"])

    def prep_c(t):
        d, rows = st[t], rows_of(t)
        c = d.pop("c")
        c_end = jnp.concatenate(
            [jnp.broadcast_to(c[i * L + L - 1:i * L + L, :], (L, width)) for i in range(n_chunks)], 0)
        inv_g = jnp.exp(-c)
        to_end = jnp.exp(c_end - c)
        b, k2 = d.pop("b"), d.pop("k2")
        ah_s[rows, :] = -d.pop("kk") * jnp.exp(c - d.pop("logw"))
        rh_s[rows, :] = d.pop("r") * jnp.exp(c)
        bt_s[rows, :] = b * inv_g
        kt_s[rows, :] = k2 * inv_g
        bb_s[rows, :] = b * to_end
        kb_s[rows, :] = k2 * to_end
        v_s[rows, :] = d.pop("v")
        gl_s[rows, :] = jnp.exp(c_end)

    def gram(t):
        d = st[t]
        ah = [block_ld(ah_s, t, i) for i in range(n_blocks)]
        rh = [block_ld(rh_s, t, i) for i in range(n_blocks)]
        bt = [block_ld(bt_s, t, i) for i in range(n_blocks)]
        kt = [block_ld(kt_s, t, i) for i in range(n_blocks)]
        d["ah0"] = [sel(m0, x) for x in ah]
        d["ah1"] = [sel(m1, x) for x in ah]
        d["rh"] = rh
        g0 = [_mm_nt(cat0(d["ah0"][i], sel(m0, rh[i])), cat0(bt[i], kt[i])) for i in range(n_blocks)]
        g1 = [_mm_nt(cat0(d["ah1"][i], sel(m1, rh[i])), cat0(kt[i], bt[i])) for i in range(n_blocks)]
        d["gb"] = [cat1(sel(tril2, x0[L:]), sel(tril2, x1[L:])) for x0, x1 in zip(g0, g1)]
        d["gt"] = [cat1(sel(m1 & stril2, x0[:L]), sel(m0 & stril2, x1[:L])) for x0, x1 in zip(g0, g1)]
        d["pw"] = [jnp.where(p_mask, cat0(x0[:L], x1[:L]), 0.0) for x0, x1 in zip(g0, g1)]
        d["tinv"] = [eye_f + x for x in d["pw"]]

    def neumann(t):
        d = st[t]
        pwb = [x.astype(BF16) for x in d["pw"]]
        d["pw"] = [jnp.dot(x, x, preferred_element_type=F32) for x in pwb]
        d["tinv"] = [tm + _mm(x, tm) for x, tm in zip(d["pw"], d["tinv"])]

    def solve(t):
        d = st[t]
        d.pop("pw")
        vv = [block_ld(v_s, t, i) for i in range(n_blocks)]
        d["v0"] = [sel(m0, x) for x in vv]
        d["v1"] = [sel(m1, x) for x in vv]
        d["vv"] = vv
        makv = [_mm(g, cat0(a, a, b_, b_)) for g, a, b_ in zip(d.pop("gt"), d["v0"], d["v1"])]
        w = [_mm(tm, cat1(cat0(a0_, a1_), cat0(sel(m0, mk), sel(m1, mk))))
             for tm, a0_, a1_, mk in zip(d.pop("tinv"), d.pop("ah0"), d.pop("ah1"), makv)]
        d["w1"] = [x[:L, :LANES] + x[L:, :LANES] for x in w]
        d["w2"] = [x[:L, LANES:] + x[L:, LANES:] for x in w]

    def fold(t):
        d = st[t]
        lhs, cm, y0, gcol = [], [], [], []
        for i in range(n_blocks):
            w1, w2, vv = d["w1"][i], d["w2"][i], d["vv"][i]
            ac = _mm_tn(cat0(block_ld(bb_s, t, i), block_ld(kb_s, t, i)),
                        cat1(cat0(w1, zeros), cat0(w2, vv)))
            ry0 = _mm(d["gb"][i], cat0(cat1(sel(m0, w1), sel(m0, w2)), cat1(zeros, d["v0"][i]),
                                       cat1(zeros, d["v1"][i]), cat1(sel(m1, w1), sel(m1, w2))))
            a_mat = jnp.where(same_head, ac[:, :LANES], 0.0)
            cm.append(jnp.where(same_head, ac[:, LANES:], 0.0))
            lhs.append(cat0(a_mat, d["rh"][i] + ry0[:, :LANES]).astype(BF16))
            y0.append(ry0[:, LANES:])
            g_end = block_ld(gl_s, t, i)[0:1, :]
            gcol.append(jnp.sum(jnp.where(eye, jnp.broadcast_to(g_end, (2 * L, LANES)), 0.0),
                                axis=1, keepdims=True))
        for key in ("w1", "w2", "vv", "v0", "v1", "gb", "rh"):
            d.pop(key)
        d.update(lhs=lhs, cm=cm, y0=y0, gcol=gcol)

    def scan(t, c):
        d = st[t]
        r0 = t * HT + c * L
        for pi in range(n_pairs):
            i = c * n_pairs + pi
            h = hs[pi]
            out = jnp.dot(d["lhs"][i], h.astype(BF16), preferred_element_type=F32)
            hs[pi] = d["gcol"][i] * h + out[:2 * L] + d["cm"][i]
            y_s[r0:r0 + L, pi * LANES:(pi + 1) * LANES] = out[2 * L:] + d["y0"][i]

    def finish(t):
        rows = rows_of(t)
        y = y_s[rows, :]
        inv_n = 1.0 / RWKV_HEAD_DIM
        mean = _head_sum(y, hsum_ref[...]) * inv_n
        yc = y - mean
        var = _head_sum(yc * yc, hsum_ref[...]) * inv_n
        yn = yc * lax.rsqrt(var + LNX_EPS) * lg_ref[...] + lb_ref[...]
        o_ref[rows, :] = ((yn + bonus_s[rows, :]) * gate_s[rows, :]).astype(o_ref.dtype)

    n_rounds = int(math.log2(L)) - 1
    prep_a(0); prep_b(0); prep_c(0)
    prep_a(1)
    gram(0)
    neumann(0)
    prep_b(1)
    neumann(0); neumann(0)
    prep_c(1)
    for _ in range(n_rounds - 3):
        neumann(0)
    solve(0); fold(0)
    gram(1)
    for c in range(n_chunks):
        if c < n_rounds:
            neumann(1)
        scan(0, c)
    for _ in range(n_rounds - n_chunks):
        neumann(1)
    finish(0)
    solve(1); fold(1)
    for c in range(n_chunks):
        scan(1, c)
    finish(1)
    prev_s[0:1, :] = rw_ref[TL - 1:TL, :]
    for pi in range(n_pairs):
        h_s[pi] = hs[pi]


def _rwkv(rw, mu, w0, w2, a0, a2, g2, k_k, k_a, r_k, lnx_g, lnx_b):
    B, S, n_rw = rw.shape
    width = w0.shape[-1]
    n_pairs = width // LANES
    L, TL = SCAN_CHUNK, SCAN_TILE
    assert S % TL == 0 and n_rw == 3 * width + DECAY_RANK + ICLR_RANK + GATE_RANK
    assert DECAY_RANK + ICLR_RANK == LANES
    ti = jnp.arange(TL // 2)
    same_chunk = (ti[:, None] // L) == (ti[None, :] // L)
    tri = (same_chunk & (ti[None, :] <= ti[:, None])).astype(BF16)
    ch = jnp.arange(LANES) // RWKV_HEAD_DIM
    hsum = (ch[:, None] == ch[None, :]).astype(BF16)
    w2p = jnp.concatenate([w2, jnp.zeros((ICLR_RANK, width), w2.dtype)], 0).astype(BF16)
    a2p = jnp.concatenate([jnp.zeros((DECAY_RANK, width), a2.dtype), a2], 0).astype(BF16)
    vec = lambda a: a.reshape(1, -1).astype(F32)
    tile = pl.BlockSpec((None, TL, n_rw), lambda b, j: (b, j, 0))
    feat = lambda: pltpu.VMEM((TL, width), F32)
    return pl.pallas_call(
        functools.partial(_rwkv_kernel, width=width, n_pairs=n_pairs),
        grid=(B, S // TL),
        in_specs=[tile, _const_spec((1, n_rw)), _const_spec((1, width)), _const_spec((LANES, width)),
                  _const_spec((1, width)), _const_spec((LANES, width)), _const_spec((GATE_RANK, width)),
                  _const_spec((1, width)), _const_spec((1, width)), _const_spec((1, width)),
                  _const_spec((1, width)), _const_spec((1, width)),
                  _const_spec((TL // 2, TL // 2)), _const_spec((LANES, LANES))],
        out_specs=pl.BlockSpec((None, TL, width), lambda b, j: (b, j, 0)),
        out_shape=jax.ShapeDtypeStruct((B, S, width), BF16),
        scratch_shapes=[pltpu.VMEM((n_pairs, LANES, LANES), F32), pltpu.VMEM((8, n_rw), F32)]
                       + [feat() for _ in range(11)],
        compiler_params=pltpu.CompilerParams(dimension_semantics=("parallel", "arbitrary"),
                                             vmem_limit_bytes=VMEM_LIMIT_BYTES),
        name="rwkv7_chunked",
    )(rw, vec(mu), vec(w0), w2p, vec(a0), a2p, g2.astype(BF16), vec(k_k), vec(k_a), vec(r_k),
      vec(lnx_g), vec(lnx_b), tri, hsum)


def _post_kernel(x_ref, lng_ref, lnb_ref, ya_ref, yb_ref, ga_ref, gb_ref, wua_ref, wub_ref, wo_ref,
                 l1g_ref, l1b_ref, wg_ref, wu_ref, wd_ref, l2g_ref, l2b_ref, o_ref, *, ffn_chunk):
    dot = functools.partial(jnp.dot, preferred_element_type=F32)
    h = _layer_norm(x_ref[...], lng_ref[...], lnb_ref[...])
    merged = (ga_ref[...].astype(F32) * dot(ya_ref[...], wua_ref[...])
              + gb_ref[...].astype(F32) * dot(yb_ref[...], wub_ref[...]))
    mix = dot(merged.astype(BF16), wo_ref[...])
    h1 = _layer_norm(DEEPNORM_ALPHA * h + mix, l1g_ref[...], l1b_ref[...])
    h1b = h1.astype(BF16)
    hidden = wg_ref.shape[1]
    ffn = jnp.zeros_like(h1)
    for c0 in range(0, hidden, ffn_chunk):
        gate = dot(h1b, wg_ref[:, c0:c0 + ffn_chunk])
        up = dot(h1b, wu_ref[:, c0:c0 + ffn_chunk])
        act = (gate * jax.nn.sigmoid(gate) * up).astype(BF16)
        ffn = ffn + dot(act, wd_ref[c0:c0 + ffn_chunk, :])
    o_ref[...] = _layer_norm(DEEPNORM_ALPHA * h1 + ffn, l2g_ref[...], l2b_ref[...])


def _post(x2, ln_g, ln_b, ya, yb, ga, gb, w_up_a, w_up_b, w_out, ln1_g, ln1_b,
          w_gate, w_up, w_down, ln2_g, ln2_b):
    T, D = x2.shape
    hidden = w_gate.shape[1]
    ffn_chunk = hidden // 2
    assert ffn_chunk % LANES == 0
    tm = ROW_TILE
    row = lambda n: pl.BlockSpec((tm, n), lambda i: (i, 0))
    vec = lambda a: a.reshape(1, D)
    cs = lambda a: _const_spec(a.shape)
    wts = [w.astype(BF16) for w in (w_up_a, w_up_b, w_out, w_gate, w_up, w_down)]
    wua, wub, wo, wg, wu, wd = wts
    return pl.pallas_call(
        functools.partial(_post_kernel, ffn_chunk=ffn_chunk),
        grid=(T // tm,),
        in_specs=[row(D), _const_spec((1, D)), _const_spec((1, D)), row(ya.shape[1]), row(yb.shape[1]),
                  row(D), row(D), cs(wua), cs(wub), cs(wo), _const_spec((1, D)), _const_spec((1, D)),
                  cs(wg), cs(wu), cs(wd), _const_spec((1, D)), _const_spec((1, D))],
        out_specs=row(D),
        out_shape=jax.ShapeDtypeStruct((T, D), F32),
        compiler_params=pltpu.CompilerParams(dimension_semantics=("parallel",),
                                             vmem_limit_bytes=VMEM_LIMIT_BYTES),
        name="merge_ffn",
    )(x2, vec(ln_g), vec(ln_b), ya, yb, ga, gb, wua, wub, wo, vec(ln1_g), vec(ln1_b),
      wg, wu, wd, vec(ln2_g), vec(ln2_b))


def kernel(x, ln_in_g, ln_in_b, rel_bias, w_in, diff_lam_q1, diff_lam_k1, diff_lam_q2, diff_lam_k2, diff_subln_g, rwkv_mu, rwkv_w0, rwkv_w2, rwkv_a0, rwkv_a2, rwkv_g2, rwkv_k_k, rwkv_k_a, rwkv_r_k, rwkv_lnx_g, rwkv_lnx_b, w_up_a, w_up_b, w_out, ln1_g, ln1_b, ffn_w_gate, ffn_w_up, ffn_w_down, ln2_g, ln2_b):
    B, S, D = x.shape
    T = B * S
    width = rwkv_w0.shape[-1]
    n_qk = DIFF_HEADS * HEAD_PAIR
    n_rw = 3 * width + DECAY_RANK + ICLR_RANK + GATE_RANK
    sizes = (n_qk, n_qk, n_qk, n_rw, D, D)
    assert w_in.shape[0] == 1 and w_in.shape[2] == sum(sizes), "single-layer model expected"
    x2 = x.reshape(T, D)
    q, k, v, rw, ga, gb = _inproj(x2, ln_in_g, ln_in_b, w_in[0].astype(BF16), sizes)
    lam_vecs = jnp.concatenate([diff_lam_q1, diff_lam_k1, diff_lam_q2, diff_lam_k2], 0).astype(F32)
    ya = _attention(q.reshape(B, S, n_qk), k.reshape(B, S, n_qk), v.reshape(B, S, n_qk),
                    rel_bias, lam_vecs, diff_subln_g[0])
    yb = _rwkv(rw.reshape(B, S, n_rw), rwkv_mu[0], rwkv_w0[0], rwkv_w2[0], rwkv_a0[0], rwkv_a2[0],
               rwkv_g2[0], rwkv_k_k[0], rwkv_k_a[0], rwkv_r_k[0], rwkv_lnx_g[0], rwkv_lnx_b[0])
    out = _post(x2, ln_in_g, ln_in_b, ya.reshape(T, n_qk), yb.reshape(T, width), ga, gb,
                w_up_a[0], w_up_b[0], w_out[0], ln1_g[0], ln1_b[0],
                ffn_w_gate[0], ffn_w_up[0], ffn_w_down[0], ln2_g[0], ln2_b[0])
    return out.reshape(B, S, D)
```

```python
import functools
import math

import jax
import jax.numpy as jnp
from jax import lax
from jax.experimental import pallas as pl
from jax.experimental.pallas import tpu as pltpu

F32 = jnp.float32
BF16 = jnp.bfloat16

DIFF_HEADS = 4
DIFF_HEAD_DIM = 64
HEAD_PAIR = 2 * DIFF_HEAD_DIM
RWKV_HEAD_DIM = 64
DECAY_RANK = 64
ICLR_RANK = 64
GATE_RANK = 128
NUM_BUCKETS = 32
MAX_DISTANCE = 128
LN_EPS = 1e-5
LNX_EPS = 64e-5
NEG_BIG = -1e30
DEPTH = 1
DEEPNORM_ALPHA = (2.0 * DEPTH) ** 0.25
LAMBDA_INIT = 0.8 - 0.6 * math.exp(-0.3 * 0)
LOG2_E = math.log2(math.e)

LANES = 128
MXU_DEPTH = 256
VMEM_LIMIT_BYTES = 56 * 1024 * 1024
ROW_TILE = 256
ATTN_TILE = 256
SCAN_CHUNK = 64
SCAN_TILE = 512


def _mm(a, b):
    return jnp.dot(a.astype(BF16), b.astype(BF16), preferred_element_type=F32)


def _mm_nt(a, b):
    return lax.dot_general(a.astype(BF16), b.astype(BF16), (((1,), (1,)), ((), ())),
                           preferred_element_type=F32)


def _mm_tn(a, b):
    return lax.dot_general(a.astype(BF16), b.astype(BF16), (((0,), (0,)), ((), ())),
                           preferred_element_type=F32)


def _split2(x):
    hi = x.astype(BF16)
    lo = (x - hi.astype(F32)).astype(BF16)
    return hi, lo


def _sel_mm(sel, x):
    hi, lo = _split2(x)
    d = functools.partial(jnp.dot, preferred_element_type=F32)
    return d(sel, hi) + d(sel, lo)


def _head_sum(x, sel):
    xb = x.astype(BF16)
    return jnp.concatenate(
        [jnp.dot(xb[:, c0:c0 + LANES], sel, preferred_element_type=F32)
         for c0 in range(0, x.shape[1], LANES)], axis=1)


def _layer_norm(x, g, b, eps=LN_EPS):
    mu = jnp.mean(x, -1, keepdims=True)
    xc = x - mu
    var = jnp.mean(xc * xc, -1, keepdims=True)
    return xc * lax.rsqrt(var + eps) * g + b


def _softplus(x):
    return jnp.maximum(x, 0.0) + jnp.log1p(jnp.exp(-jnp.abs(x)))


def _const_spec(shape):
    return pl.BlockSpec(shape, lambda *_: (0,) * len(shape), pipeline_mode=pl.Buffered(1))


def _inproj_kernel(x_ref, g_ref, b_ref, w_ref, q_ref, k_ref, v_ref, rw_ref, ga_ref, gb_ref,
                   h_even, h_odd, *, splits):
    i = pl.program_id(0)

    @pl.when(i == 0)
    def _():
        h_even[...] = jnp.zeros_like(h_even)

    def step(h_prev, h_next):
        h_next[...] = _layer_norm(x_ref[...], g_ref[...], b_ref[...]).astype(BF16)

        def proj(lo, hi):
            return jnp.dot(h_prev[...], w_ref[:, lo:hi], preferred_element_type=F32)

        s_q, s_k, s_v, s_rw, s_ga = splits
        ga_ref[...] = jax.nn.sigmoid(proj(s_rw, s_ga)).astype(BF16)
        gb_ref[...] = jax.nn.sigmoid(proj(s_ga, w_ref.shape[1])).astype(BF16)
        q_ref[...] = (proj(0, s_q) * (DIFF_HEAD_DIM ** -0.5 * LOG2_E)).astype(BF16)
        k_ref[...] = proj(s_q, s_k).astype(BF16)
        v_ref[...] = proj(s_k, s_v).astype(BF16)
        rw_ref[...] = proj(s_v, s_rw)

    @pl.when(i % 2 == 0)
    def _():
        step(h_even, h_odd)

    @pl.when(i % 2 == 1)
    def _():
        step(h_odd, h_even)


def _inproj(x2, ln_g, ln_b, w_in_bf, sizes):
    T, D = x2.shape
    n_q, n_k, n_v, n_rw, n_ga, n_gb = sizes
    acc, cuts = 0, []
    for s in sizes[:-1]:
        acc += s
        cuts.append(acc)
    splits = tuple(cuts)
    tm = ROW_TILE
    n_tiles = T // tm
    row_in = pl.BlockSpec((tm, D), lambda i: (jnp.minimum(i, n_tiles - 1), 0))
    row = lambda n: pl.BlockSpec((tm, n), lambda i: (jnp.maximum(i - 1, 0), 0))
    return pl.pallas_call(
        functools.partial(_inproj_kernel, splits=splits),
        grid=(n_tiles + 1,),
        in_specs=[row_in, _const_spec((1, D)), _const_spec((1, D)), _const_spec(w_in_bf.shape)],
        out_specs=[row(n_q), row(n_k), row(n_v), row(n_rw), row(n_ga), row(n_gb)],
        out_shape=[jax.ShapeDtypeStruct((T, n_q), BF16), jax.ShapeDtypeStruct((T, n_k), BF16),
                   jax.ShapeDtypeStruct((T, n_v), BF16), jax.ShapeDtypeStruct((T, n_rw), F32),
                   jax.ShapeDtypeStruct((T, n_ga), BF16), jax.ShapeDtypeStruct((T, n_gb), BF16)],
        scratch_shapes=[pltpu.VMEM((tm, D), BF16), pltpu.VMEM((tm, D), BF16)],
        compiler_params=pltpu.CompilerParams(dimension_semantics=("arbitrary",),
                                             vmem_limit_bytes=VMEM_LIMIT_BYTES),
        name="ln_inproj",
    )(x2, ln_g.reshape(1, D), ln_b.reshape(1, D), w_in_bf)


def _attn_kernel(q_ref, k_ref, v_ref, bdiag_ref, bnear_ref, bfar_ref, lam_ref, g_ref, o_ref,
                 *, nq, t):
    lane = lax.broadcasted_iota(jnp.int32, (t, HEAD_PAIR), 1)
    lo = lane < DIFF_HEAD_DIM
    lv = lam_ref[...]
    lam = (jnp.exp(jnp.sum(lv[0:1] * lv[1:2], keepdims=True))
           - jnp.exp(jnp.sum(lv[2:3] * lv[3:4], keepdims=True)) + LAMBDA_INIT)
    bfar = bfar_ref[...][:, 0:1]
    zero = jnp.zeros((), BF16)
    rowmax = lambda a: jnp.max(a, -1, keepdims=True)
    rowsum = lambda a: jnp.sum(a, -1, keepdims=True)

    def logits(qi):
        q = q_ref[qi * t:(qi + 1) * t, :]
        qs = jnp.concatenate([jnp.where(lo, q, zero), jnp.where(lo, zero, q)], axis=0)
        return lax.dot_general(qs, k_ref[0:(qi + 1) * t, :], (((1,), (1,)), ((), ())),
                               preferred_element_type=F32)

    def softmax(qi, s):
        s_diag = s[:, qi * t:] + bdiag_ref[...]
        m = rowmax(s_diag)
        if qi >= 1:
            s_near = s[:, (qi - 1) * t:qi * t] + bnear_ref[...]
            m = jnp.maximum(m, rowmax(s_near))
        if qi >= 2:
            s_far = s[:, :(qi - 1) * t]
            m = jnp.maximum(m, rowmax(s_far) + bfar)
        parts = []
        if qi >= 2:
            parts.append(jnp.exp2(s_far - (m - bfar)))
        if qi >= 1:
            parts.append(jnp.exp2(s_near - m))
        parts.append(jnp.exp2(s_diag - m))
        l = sum(rowsum(p) for p in parts)
        w1 = 1.0 / l[:t]
        w2 = lam / l[t:]
        return jnp.concatenate([(x[:t] * w1 - x[t:] * w2).astype(BF16) for x in parts], axis=1)

    def values(qi, p):
        o = jnp.dot(p, v_ref[0:(qi + 1) * t, :], preferred_element_type=F32)
        ms = jnp.mean(o * o, -1, keepdims=True)
        o = o * lax.rsqrt(ms + LN_EPS) * g_ref[...] * (1.0 - LAMBDA_INIT)
        o_ref[qi * t:(qi + 1) * t, :] = o.astype(o_ref.dtype)

    s_q, p_q = {}, {}
    for step in range(nq + 2):
        if step < nq:
            s_q[step] = logits(step)
        if 0 <= step - 1 < nq:
            p_q[step - 1] = softmax(step - 1, s_q.pop(step - 1))
        if 0 <= step - 2 < nq:
            values(step - 2, p_q.pop(step - 2))


def _t5_bucket(dist):
    max_exact = NUM_BUCKETS // 2
    d = jnp.maximum(dist, 1).astype(F32)
    large = max_exact + (jnp.log(d / max_exact) / math.log(MAX_DISTANCE / max_exact)
                         * (NUM_BUCKETS - max_exact)).astype(jnp.int32)
    large = jnp.minimum(large, NUM_BUCKETS - 1)
    return jnp.where(dist < max_exact, dist, large)


def _attention(q, k, v, rel_bias, lam_vecs, subln_g):
    B, S, _ = q.shape
    t = ATTN_TILE
    assert S % t == 0 and t >= MAX_DISTANCE
    nq = S // t
    dd = jnp.arange(t, dtype=jnp.int32)[:, None] - jnp.arange(t, dtype=jnp.int32)[None, :]
    buckets = jnp.arange(NUM_BUCKETS, dtype=jnp.int32)

    def bias_tile(dist):
        onehot = (_t5_bucket(dist)[..., None] == buckets).astype(F32)
        return jnp.einsum("ijb,bh->hij", onehot, rel_bias.astype(F32),
                          precision=lax.Precision.HIGHEST)

    bdiag = jnp.where(dd >= 0, bias_tile(jnp.maximum(dd, 0)) * LOG2_E, NEG_BIG)
    bnear = bias_tile(dd + t) * LOG2_E
    bdiag = jnp.concatenate([bdiag, bdiag], axis=1)
    bnear = jnp.concatenate([bnear, bnear], axis=1)
    bfar = jnp.broadcast_to((rel_bias[NUM_BUCKETS - 1] * LOG2_E)[:, None, None],
                            (DIFF_HEADS, 1, LANES))
    seq = pl.BlockSpec((None, S, HEAD_PAIR), lambda b, h: (b, 0, h))
    per_head = lambda r, c: pl.BlockSpec((None, r, c), lambda b, h: (h, 0, 0))
    return pl.pallas_call(
        functools.partial(_attn_kernel, nq=nq, t=t),
        grid=(B, DIFF_HEADS),
        in_specs=[seq, seq, seq, per_head(2 * t, t), per_head(2 * t, t), per_head(1, LANES),
                  pl.BlockSpec((4, DIFF_HEAD_DIM), lambda b, h: (0, 0)),
                  pl.BlockSpec((1, HEAD_PAIR), lambda b, h: (0, 0))],
        out_specs=seq,
        out_shape=jax.ShapeDtypeStruct((B, S, DIFF_HEADS * HEAD_PAIR), BF16),
        compiler_params=pltpu.CompilerParams(dimension_semantics=("parallel", "parallel"),
                                             vmem_limit_bytes=VMEM_LIMIT_BYTES),
        name="diff_attention",
    )(q, k, v, bdiag, bnear, bfar, lam_vecs, subln_g.reshape(1, HEAD_PAIR))


def _rwkv_kernel(rw_ref, mu_ref, w0_ref, w2_ref, a0_ref, a2_ref, g2_ref, kk_ref, ka_ref, rk_ref,
                 lg_ref, lb_ref, tri_ref, hsum_ref, o_ref,
                 h_s, prev_s, ah_s, rh_s, bt_s, kt_s, bb_s, kb_s, v_s, gl_s, y_s, bonus_s, gate_s,
                 *, width, n_pairs):
    L = SCAN_CHUNK
    TL = rw_ref.shape[0]
    j = pl.program_id(1)

    @pl.when(j == 0)
    def _():
        h_s[...] = jnp.zeros_like(h_s)
        prev_s[...] = jnp.zeros_like(prev_s)

    HT = TL // 2
    n_chunks = HT // L
    n_blocks = n_chunks * n_pairs

    lane = lax.broadcasted_iota(jnp.int32, (L, LANES), 1)
    m0 = lane < RWKV_HEAD_DIM
    m1 = jnp.logical_not(m0)
    ri = lax.broadcasted_iota(jnp.int32, (2 * L, LANES), 0)
    ci = lax.broadcasted_iota(jnp.int32, (2 * L, LANES), 1)
    same_head = (ri // L) == (ci // L)
    p_mask = same_head & ((ci % L) < (ri % L))
    eye = ri == ci
    rl = lax.broadcasted_iota(jnp.int32, (L, LANES), 0)
    tril2 = (lane % L) <= rl
    stril2 = (lane % L) < rl
    eye_f = jnp.where(eye, 1.0, 0.0).astype(F32)
    zeros = jnp.zeros((L, LANES), F32)

    sel = lambda m, x: jnp.where(m, x, zeros)
    cat0 = lambda *xs: jnp.concatenate(xs, 0)
    cat1 = lambda *xs: jnp.concatenate(xs, 1)
    row_in_half = lax.broadcasted_iota(jnp.int32, (HT, 1), 0)
    st = [dict(), dict()]
    hs = [h_s[pi] for pi in range(n_pairs)]

    def rows_of(t):
        return slice(t * HT, (t + 1) * HT)

    def block_ld(ref, t, i):
        c, pi = divmod(i, n_pairs)
        r0 = t * HT + c * L
        return ref[r0:r0 + L, pi * LANES:(pi + 1) * LANES]

    def prep_a(t):
        d, rows = st[t], rows_of(t)
        p = rw_ref[rows, :]
        before = prev_s[0:1, :] if t == 0 else rw_ref[t * HT - 1:t * HT, :]
        p_prev = jnp.where(row_in_half == 0, before, pltpu.roll(p, 1, 0))
        ps = p + (p_prev - p) * mu_ref[...]
        d["r"], d["k"], d["v"] = ps[:, 0:width], ps[:, width:2 * width], ps[:, 2 * width:3 * width]
        lr = ps[:, 3 * width:3 * width + DECAY_RANK + ICLR_RANK]
        gl = ps[:, 3 * width + DECAY_RANK + ICLR_RANK:]
        d["z"] = w0_ref[...] + _mm(jnp.tanh(lr), w2_ref[...])
        d["ai"] = a0_ref[...] + _mm(lr, a2_ref[...])
        gate_s[rows, :] = _mm(jax.nn.sigmoid(gl), g2_ref[...])
        d["kk"] = d["k"] * kk_ref[...]
        d["ss"] = _head_sum(d["kk"] * d["kk"], hsum_ref[...])

    def prep_b(t):
        d, rows = st[t], rows_of(t)
        d["logw"] = -jnp.exp(-_softplus(-d.pop("z")) - 0.5)
        iclr = jax.nn.sigmoid(d.pop("ai"))
        kk = d["kk"] * lax.rsqrt(jnp.maximum(d.pop("ss"), 1e-24))
        d["kk"] = kk
        d["b"] = kk * iclr
        d["k2"] = d.pop("k") * (1.0 + (iclr - 1.0) * ka_ref[...])
        bonus_s[rows, :] = _head_sum(d["r"] * d["k2"] * rk_ref[...], hsum_ref[...]) * d["v"]
        d["c"] = _sel_mm(tri_ref[...], d["logw"])

    def prep_c(t):
        d, rows = st[t], rows_of(t)
        c = d.pop("c")
        c_end = jnp.concatenate(
            [jnp.broadcast_to(c[i * L + L - 1:i * L + L, :], (L, width)) for i in range(n_chunks)], 0)
        inv_g = jnp.exp(-c)
        to_end = jnp.exp(c_end - c)
        b, k2 = d.pop("b"), d.pop("k2")
        ah_s[rows, :] = -d.pop("kk") * jnp.exp(c - d.pop("logw"))
        rh_s[rows, :] = d.pop("r") * jnp.exp(c)
        bt_s[rows, :] = b * inv_g
        kt_s[rows, :] = k2 * inv_g
        bb_s[rows, :] = b * to_end
        kb_s[rows, :] = k2 * to_end
        v_s[rows, :] = d.pop("v")
        gl_s[rows, :] = jnp.exp(c_end)

    def gram(t):
        d = st[t]
        ah = [block_ld(ah_s, t, i) for i in range(n_blocks)]
        rh = [block_ld(rh_s, t, i) for i in range(n_blocks)]
        bt = [block_ld(bt_s, t, i) for i in range(n_blocks)]
        kt = [block_ld(kt_s, t, i) for i in range(n_blocks)]
        d["ah0"] = [sel(m0, x) for x in ah]
        d["ah1"] = [sel(m1, x) for x in ah]
        d["rh"] = rh
        g0 = [_mm_nt(cat0(d["ah0"][i], sel(m0, rh[i])), cat0(bt[i], kt[i])) for i in range(n_blocks)]
        g1 = [_mm_nt(cat0(d["ah1"][i], sel(m1, rh[i])), cat0(kt[i], bt[i])) for i in range(n_blocks)]
        d["gb"] = [cat1(sel(tril2, x0[L:]), sel(tril2, x1[L:])) for x0, x1 in zip(g0, g1)]
        d["gt"] = [cat1(sel(m1 & stril2, x0[:L]), sel(m0 & stril2, x1[:L])) for x0, x1 in zip(g0, g1)]
        d["pw"] = [jnp.where(p_mask, cat0(x0[:L], x1[:L]), 0.0) for x0, x1 in zip(g0, g1)]
        d["tinv"] = [eye_f + x for x in d["pw"]]

    def neumann(t):
        d = st[t]
        pwb = [x.astype(BF16) for x in d["pw"]]
        d["pw"] = [jnp.dot(x, x, preferred_element_type=F32) for x in pwb]
        d["tinv"] = [tm + _mm(x, tm) for x, tm in zip(d["pw"], d["tinv"])]

    def solve(t):
        d = st[t]
        d.pop("pw")
        vv = [block_ld(v_s, t, i) for i in range(n_blocks)]
        d["v0"] = [sel(m0, x) for x in vv]
        d["v1"] = [sel(m1, x) for x in vv]
        d["vv"] = vv
        makv = [_mm(g, cat0(a, a, b_, b_)) for g, a, b_ in zip(d.pop("gt"), d["v0"], d["v1"])]
        w = [_mm(tm, cat1(cat0(a0_, a1_), cat0(sel(m0, mk), sel(m1, mk))))
             for tm, a0_, a1_, mk in zip(d.pop("tinv"), d.pop("ah0"), d.pop("ah1"), makv)]
        d["w1"] = [x[:L, :LANES] + x[L:, :LANES] for x in w]
        d["w2"] = [x[:L, LANES:] + x[L:, LANES:] for x in w]

    def fold(t):
        d = st[t]
        lhs, cm, y0, gcol = [], [], [], []
        for i in range(n_blocks):
            w1, w2, vv = d["w1"][i], d["w2"][i], d["vv"][i]
            ac = _mm_tn(cat0(block_ld(bb_s, t, i), block_ld(kb_s, t, i)),
                        cat1(cat0(w1, zeros), cat0(w2, vv)))
            ry0 = _mm(d["gb"][i], cat0(cat1(sel(m0, w1), sel(m0, w2)), cat1(zeros, d["v0"][i]),
                                       cat1(zeros, d["v1"][i]), cat1(sel(m1, w1), sel(m1, w2))))
            a_mat = jnp.where(same_head, ac[:, :LANES], 0.0)
            cm.append(jnp.where(same_head, ac[:, LANES:], 0.0))
            lhs.append(cat0(a_mat, d["rh"][i] + ry0[:, :LANES]).astype(BF16))
            y0.append(ry0[:, LANES:])
            g_end = block_ld(gl_s, t, i)[0:1, :]
            gcol.append(jnp.sum(jnp.where(eye, jnp.broadcast_to(g_end, (2 * L, LANES)), 0.0),
                                axis=1, keepdims=True))
        for key in ("w1", "w2", "vv", "v0", "v1", "gb", "rh"):
            d.pop(key)
        d.update(lhs=lhs, cm=cm, y0=y0, gcol=gcol)

    def scan(t, c):
        d = st[t]
        r0 = t * HT + c * L
        for pi in range(n_pairs):
            i = c * n_pairs + pi
            h = hs[pi]
            out = jnp.dot(d["lhs"][i], h.astype(BF16), preferred_element_type=F32)
            hs[pi] = d["gcol"][i] * h + out[:2 * L] + d["cm"][i]
            y_s[r0:r0 + L, pi * LANES:(pi + 1) * LANES] = out[2 * L:] + d["y0"][i]

    def finish(t):
        rows = rows_of(t)
        y = y_s[rows, :]
        inv_n = 1.0 / RWKV_HEAD_DIM
        mean = _head_sum(y, hsum_ref[...]) * inv_n
        yc = y - mean
        var = _head_sum(yc * yc, hsum_ref[...]) * inv_n
        yn = yc * lax.rsqrt(var + LNX_EPS) * lg_ref[...] + lb_ref[...]
        o_ref[rows, :] = ((yn + bonus_s[rows, :]) * gate_s[rows, :]).astype(o_ref.dtype)

    n_rounds = int(math.log2(L)) - 1
    prep_a(0); prep_b(0); prep_c(0)
    prep_a(1)
    gram(0)
    neumann(0)
    prep_b(1)
    neumann(0); neumann(0)
    prep_c(1)
    for _ in range(n_rounds - 3):
        neumann(0)
    solve(0); fold(0)
    gram(1)
    for c in range(n_chunks):
        if c < n_rounds:
            neumann(1)
        scan(0, c)
    for _ in range(n_rounds - n_chunks):
        neumann(1)
    finish(0)
    solve(1); fold(1)
    for c in range(n_chunks):
        scan(1, c)
    finish(1)
    prev_s[0:1, :] = rw_ref[TL - 1:TL, :]
    for pi in range(n_pairs):
        h_s[pi] = hs[pi]


def _rwkv(rw, mu, w0, w2, a0, a2, g2, k_k, k_a, r_k, lnx_g, lnx_b):
    B, S, n_rw = rw.shape
    width = w0.shape[-1]
    n_pairs = width // LANES
    L, TL = SCAN_CHUNK, SCAN_TILE
    assert S % TL == 0 and n_rw == 3 * width + DECAY_RANK + ICLR_RANK + GATE_RANK
    assert DECAY_RANK + ICLR_RANK == LANES
    ti = jnp.arange(TL // 2)
    same_chunk = (ti[:, None] // L) == (ti[None, :] // L)
    tri = (same_chunk & (ti[None, :] <= ti[:, None])).astype(BF16)
    ch = jnp.arange(LANES) // RWKV_HEAD_DIM
    hsum = (ch[:, None] == ch[None, :]).astype(BF16)
    w2p = jnp.concatenate([w2, jnp.zeros((ICLR_RANK, width), w2.dtype)], 0).astype(BF16)
    a2p = jnp.concatenate([jnp.zeros((DECAY_RANK, width), a2.dtype), a2], 0).astype(BF16)
    vec = lambda a: a.reshape(1, -1).astype(F32)
    tile = pl.BlockSpec((None, TL, n_rw), lambda b, j: (b, j, 0))
    feat = lambda: pltpu.VMEM((TL, width), F32)
    return pl.pallas_call(
        functools.partial(_rwkv_kernel, width=width, n_pairs=n_pairs),
        grid=(B, S // TL),
        in_specs=[tile, _const_spec((1, n_rw)), _const_spec((1, width)), _const_spec((LANES, width)),
                  _const_spec((1, width)), _const_spec((LANES, width)), _const_spec((GATE_RANK, width)),
                  _const_spec((1, width)), _const_spec((1, width)), _const_spec((1, width)),
                  _const_spec((1, width)), _const_spec((1, width)),
                  _const_spec((TL // 2, TL // 2)), _const_spec((LANES, LANES))],
        out_specs=pl.BlockSpec((None, TL, width), lambda b, j: (b, j, 0)),
        out_shape=jax.ShapeDtypeStruct((B, S, width), BF16),
        scratch_shapes=[pltpu.VMEM((n_pairs, LANES, LANES), F32), pltpu.VMEM((8, n_rw), F32)]
                       + [feat() for _ in range(11)],
        compiler_params=pltpu.CompilerParams(dimension_semantics=("parallel", "arbitrary"),
                                             vmem_limit_bytes=VMEM_LIMIT_BYTES),
        name="rwkv7_chunked",
    )(rw, vec(mu), vec(w0), w2p, vec(a0), a2p, g2.astype(BF16), vec(k_k), vec(k_a), vec(r_k),
      vec(lnx_g), vec(lnx_b), tri, hsum)


def _post_kernel(x_ref, lng_ref, lnb_ref, ya_ref, yb_ref, ga_ref, gb_ref, wua_ref, wub_ref, wo_ref,
                 l1g_ref, l1b_ref, wg_ref, wu_ref, wd_ref, l2g_ref, l2b_ref, o_ref,
                 z_even, z_odd, *, ffn_cuts):
    i = pl.program_id(0)
    dot = functools.partial(jnp.dot, preferred_element_type=F32)

    @pl.when(i == 0)
    def _():
        z_even[...] = jnp.zeros_like(z_even)

    def ffn_part(h1b, lo, hi):
        gate = dot(h1b, wg_ref[:, lo:hi])
        up = dot(h1b, wu_ref[:, lo:hi])
        act = (gate * jax.nn.sigmoid(gate) * up).astype(BF16)
        return dot(act, wd_ref[lo:hi, :])

    def step(z_prev, z_next):
        up_a = dot(ya_ref[...], wua_ref[...])
        up_b = dot(yb_ref[...], wub_ref[...])
        h1 = _layer_norm(z_prev[...], l1g_ref[...], l1b_ref[...])
        h1b = h1.astype(BF16)
        ffn = ffn_part(h1b, ffn_cuts[0], ffn_cuts[1])
        merged = ga_ref[...].astype(F32) * up_a + gb_ref[...].astype(F32) * up_b
        for lo, hi in zip(ffn_cuts[1:-1], ffn_cuts[2:]):
            ffn = ffn + ffn_part(h1b, lo, hi)
        mix = dot(merged.astype(BF16), wo_ref[...])
        o_ref[...] = _layer_norm(DEEPNORM_ALPHA * h1 + ffn, l2g_ref[...], l2b_ref[...])
        h = _layer_norm(x_ref[...], lng_ref[...], lnb_ref[...])
        z_next[...] = DEEPNORM_ALPHA * h + mix

    @pl.when(i % 2 == 0)
    def _():
        step(z_even, z_odd)

    @pl.when(i % 2 == 1)
    def _():
        step(z_odd, z_even)


def _post(x2, ln_g, ln_b, ya, yb, ga, gb, w_up_a, w_up_b, w_out, ln1_g, ln1_b,
          w_gate, w_up, w_down, ln2_g, ln2_b):
    T, D = x2.shape
    hidden = w_gate.shape[1]
    assert hidden % MXU_DEPTH == 0
    ffn_cuts = (0, (hidden // MXU_DEPTH // 2) * MXU_DEPTH, hidden)
    tm = ROW_TILE
    n_tiles = T // tm
    row = lambda n: pl.BlockSpec((tm, n), lambda i: (jnp.minimum(i, n_tiles - 1), 0))
    vec = lambda a: a.reshape(1, D)
    cs = lambda a: _const_spec(a.shape)
    wts = [w.astype(BF16) for w in (w_up_a, w_up_b, w_out, w_gate, w_up, w_down)]
    wua, wub, wo, wg, wu, wd = wts
    return pl.pallas_call(
        functools.partial(_post_kernel, ffn_cuts=ffn_cuts),
        grid=(n_tiles + 1,),
        in_specs=[row(D), _const_spec((1, D)), _const_spec((1, D)), row(ya.shape[1]), row(yb.shape[1]),
                  row(D), row(D), cs(wua), cs(wub), cs(wo), _const_spec((1, D)), _const_spec((1, D)),
                  cs(wg), cs(wu), cs(wd), _const_spec((1, D)), _const_spec((1, D))],
        out_specs=pl.BlockSpec((tm, D), lambda i: (jnp.maximum(i - 1, 0), 0)),
        out_shape=jax.ShapeDtypeStruct((T, D), F32),
        scratch_shapes=[pltpu.VMEM((tm, D), F32), pltpu.VMEM((tm, D), F32)],
        compiler_params=pltpu.CompilerParams(dimension_semantics=("arbitrary",),
                                             vmem_limit_bytes=VMEM_LIMIT_BYTES),
        name="merge_ffn",
    )(x2, vec(ln_g), vec(ln_b), ya, yb, ga, gb, wua, wub, wo, vec(ln1_g), vec(ln1_b),
      wg, wu, wd, vec(ln2_g), vec(ln2_b))


def kernel(x, ln_in_g, ln_in_b, rel_bias, w_in, diff_lam_q1, diff_lam_k1, diff_lam_q2, diff_lam_k2, diff_subln_g, rwkv_mu, rwkv_w0, rwkv_w2, rwkv_a0, rwkv_a2, rwkv_g2, rwkv_k_k, rwkv_k_a, rwkv_r_k, rwkv_lnx_g, rwkv_lnx_b, w_up_a, w_up_b, w_out, ln1_g, ln1_b, ffn_w_gate, ffn_w_up, ffn_w_down, ln2_g, ln2_b):
    B, S, D = x.shape
    T = B * S
    width = rwkv_w0.shape[-1]
    n_qk = DIFF_HEADS * HEAD_PAIR
    n_rw = 3 * width + DECAY_RANK + ICLR_RANK + GATE_RANK
    sizes = (n_qk, n_qk, n_qk, n_rw, D, D)
    assert w_in.shape[0] == 1 and w_in.shape[2] == sum(sizes), "single-layer model expected"
    x2 = x.reshape(T, D)
    q, k, v, rw, ga, gb = _inproj(x2, ln_in_g, ln_in_b, w_in[0].astype(BF16), sizes)
    lam_vecs = jnp.concatenate([diff_lam_q1, diff_lam_k1, diff_lam_q2, diff_lam_k2], 0).astype(F32)
    ya = _attention(q.reshape(B, S, n_qk), k.reshape(B, S, n_qk), v.reshape(B, S, n_qk),
                    rel_bias, lam_vecs, diff_subln_g[0])
    yb = _rwkv(rw.reshape(B, S, n_rw), rwkv_mu[0], rwkv_w0[0], rwkv_w2[0], rwkv_a0[0], rwkv_a2[0],
               rwkv_g2[0], rwkv_k_k[0], rwkv_k_a[0], rwkv_r_k[0], rwkv_lnx_g[0], rwkv_lnx_b[0])
    out = _post(x2, ln_in_g, ln_in_b, ya.reshape(T, n_qk), yb.reshape(T, width), ga, gb,
                w_up_a[0], w_up_b[0], w_out[0], ln1_g[0], ln1_b[0],
                ffn_w_gate[0], ffn_w_up[0], ffn_w_down[0], ln2_g[0], ln2_b[0])
    return out.reshape(B, S, D)
```

```python
import functools
import math

import jax
import jax.numpy as jnp
from jax import lax
from jax.experimental import pallas as pl
from jax.experimental.pallas import tpu as pltpu

F32 = jnp.float32
BF16 = jnp.bfloat16

DIFF_HEADS = 4
DIFF_HEAD_DIM = 64
HEAD_PAIR = 2 * DIFF_HEAD_DIM
RWKV_HEAD_DIM = 64
DECAY_RANK = 64
ICLR_RANK = 64
GATE_RANK = 128
NUM_BUCKETS = 32
MAX_DISTANCE = 128
LN_EPS = 1e-5
LNX_EPS = 64e-5
NEG_BIG = -1e30
DEPTH = 1
DEEPNORM_ALPHA = (2.0 * DEPTH) ** 0.25
LAMBDA_INIT = 0.8 - 0.6 * math.exp(-0.3 * 0)
LOG2_E = math.log2(math.e)

LANES = 128
MXU_DEPTH = 256
VMEM_LIMIT_BYTES = 56 * 1024 * 1024
ROW_TILE = 256
ATTN_TILE = 256
SCAN_CHUNK = 64
SCAN_TASK = 256
SCAN_TILE = 512


def _mm(a, b):
    return jnp.dot(a.astype(BF16), b.astype(BF16), preferred_element_type=F32)


def _mm_nt(a, b):
    return lax.dot_general(a.astype(BF16), b.astype(BF16), (((1,), (1,)), ((), ())),
                           preferred_element_type=F32)


def _mm_tn(a, b):
    return lax.dot_general(a.astype(BF16), b.astype(BF16), (((0,), (0,)), ((), ())),
                           preferred_element_type=F32)


def _split2(x):
    hi = x.astype(BF16)
    lo = (x - hi.astype(F32)).astype(BF16)
    return hi, lo


def _sel_mm(sel, x):
    hi, lo = _split2(x)
    d = functools.partial(jnp.dot, preferred_element_type=F32)
    return d(sel, hi) + d(sel, lo)


def _head_sum(x, sel):
    xb = x.astype(BF16)
    return jnp.concatenate(
        [jnp.dot(xb[:, c0:c0 + LANES], sel, preferred_element_type=F32)
         for c0 in range(0, x.shape[1], LANES)], axis=1)


def _layer_norm(x, g, b, eps=LN_EPS):
    mu = jnp.mean(x, -1, keepdims=True)
    xc = x - mu
    var = jnp.mean(xc * xc, -1, keepdims=True)
    return xc * lax.rsqrt(var + eps) * g + b


def _softplus(x):
    return jnp.maximum(x, 0.0) + jnp.log1p(jnp.exp(-jnp.abs(x)))


def _const_spec(shape):
    return pl.BlockSpec(shape, lambda *_: (0,) * len(shape), pipeline_mode=pl.Buffered(1))


def _inproj_kernel(x_ref, g_ref, b_ref, w_ref, q_ref, k_ref, v_ref, rw_ref, ga_ref, gb_ref,
                   h_even, h_odd, *, splits):
    i = pl.program_id(0)

    @pl.when(i == 0)
    def _():
        h_even[...] = jnp.zeros_like(h_even)

    def step(h_prev, h_next):
        h_next[...] = _layer_norm(x_ref[...], g_ref[...], b_ref[...]).astype(BF16)

        def proj(lo, hi):
            return jnp.dot(h_prev[...], w_ref[:, lo:hi], preferred_element_type=F32)

        s_q, s_k, s_v, s_rw, s_ga = splits
        ga_ref[...] = jax.nn.sigmoid(proj(s_rw, s_ga)).astype(BF16)
        gb_ref[...] = jax.nn.sigmoid(proj(s_ga, w_ref.shape[1])).astype(BF16)
        q_ref[...] = (proj(0, s_q) * (DIFF_HEAD_DIM ** -0.5 * LOG2_E)).astype(BF16)
        k_ref[...] = proj(s_q, s_k).astype(BF16)
        v_ref[...] = proj(s_k, s_v).astype(BF16)
        rw_ref[...] = proj(s_v, s_rw)

    @pl.when(i % 2 == 0)
    def _():
        step(h_even, h_odd)

    @pl.when(i % 2 == 1)
    def _():
        step(h_odd, h_even)


def _inproj(x2, ln_g, ln_b, w_in_bf, sizes):
    T, D = x2.shape
    n_q, n_k, n_v, n_rw, n_ga, n_gb = sizes
    acc, cuts = 0, []
    for s in sizes[:-1]:
        acc += s
        cuts.append(acc)
    splits = tuple(cuts)
    tm = ROW_TILE
    n_tiles = T // tm
    row_in = pl.BlockSpec((tm, D), lambda i: (jnp.minimum(i, n_tiles - 1), 0))
    row = lambda n: pl.BlockSpec((tm, n), lambda i: (jnp.maximum(i - 1, 0), 0))
    return pl.pallas_call(
        functools.partial(_inproj_kernel, splits=splits),
        grid=(n_tiles + 1,),
        in_specs=[row_in, _const_spec((1, D)), _const_spec((1, D)), _const_spec(w_in_bf.shape)],
        out_specs=[row(n_q), row(n_k), row(n_v), row(n_rw), row(n_ga), row(n_gb)],
        out_shape=[jax.ShapeDtypeStruct((T, n_q), BF16), jax.ShapeDtypeStruct((T, n_k), BF16),
                   jax.ShapeDtypeStruct((T, n_v), BF16), jax.ShapeDtypeStruct((T, n_rw), F32),
                   jax.ShapeDtypeStruct((T, n_ga), BF16), jax.ShapeDtypeStruct((T, n_gb), BF16)],
        scratch_shapes=[pltpu.VMEM((tm, D), BF16), pltpu.VMEM((tm, D), BF16)],
        compiler_params=pltpu.CompilerParams(dimension_semantics=("arbitrary",),
                                             vmem_limit_bytes=VMEM_LIMIT_BYTES),
        name="ln_inproj",
    )(x2, ln_g.reshape(1, D), ln_b.reshape(1, D), w_in_bf)


def _attn_kernel(q_ref, k_ref, v_ref, bdiag_ref, bnear_ref, bfar_ref, lam_ref, g_ref, o_ref,
                 *, nq, t):
    lane = lax.broadcasted_iota(jnp.int32, (t, HEAD_PAIR), 1)
    lo = lane < DIFF_HEAD_DIM
    lv = lam_ref[...]
    lam = (jnp.exp(jnp.sum(lv[0:1] * lv[1:2], keepdims=True))
           - jnp.exp(jnp.sum(lv[2:3] * lv[3:4], keepdims=True)) + LAMBDA_INIT)
    bfar = bfar_ref[...][:, 0:1]
    zero = jnp.zeros((), BF16)
    rowmax = lambda a: jnp.max(a, -1, keepdims=True)
    rowsum = lambda a: jnp.sum(a, -1, keepdims=True)

    def logits(qi):
        q = q_ref[qi * t:(qi + 1) * t, :]
        qs = jnp.concatenate([jnp.where(lo, q, zero), jnp.where(lo, zero, q)], axis=0)
        return lax.dot_general(qs, k_ref[0:(qi + 1) * t, :], (((1,), (1,)), ((), ())),
                               preferred_element_type=F32)

    def softmax(qi, s):
        s_diag = s[:, qi * t:] + bdiag_ref[...]
        m = rowmax(s_diag)
        if qi >= 1:
            s_near = s[:, (qi - 1) * t:qi * t] + bnear_ref[...]
            m = jnp.maximum(m, rowmax(s_near))
        if qi >= 2:
            s_far = s[:, :(qi - 1) * t]
            m = jnp.maximum(m, rowmax(s_far) + bfar)
        parts = []
        if qi >= 2:
            parts.append(jnp.exp2(s_far - (m - bfar)))
        if qi >= 1:
            parts.append(jnp.exp2(s_near - m))
        parts.append(jnp.exp2(s_diag - m))
        l = sum(rowsum(p) for p in parts)
        return jnp.concatenate([x.astype(BF16) for x in parts], axis=1), l

    def values(qi, pl_):
        p, l = pl_
        acc = jnp.dot(p, v_ref[0:(qi + 1) * t, :], preferred_element_type=F32)
        o = acc[:t] * (1.0 / l[:t]) - acc[t:] * (lam / l[t:])
        ms = jnp.mean(o * o, -1, keepdims=True)
        o = o * lax.rsqrt(ms + LN_EPS) * g_ref[...] * (1.0 - LAMBDA_INIT)
        o_ref[qi * t:(qi + 1) * t, :] = o.astype(o_ref.dtype)

    s_q, p_q = {}, {}
    for step in range(nq + 2):
        if step < nq:
            s_q[step] = logits(step)
        if 0 <= step - 1 < nq:
            p_q[step - 1] = softmax(step - 1, s_q.pop(step - 1))
        if 0 <= step - 2 < nq:
            values(step - 2, p_q.pop(step - 2))


def _t5_bucket(dist):
    max_exact = NUM_BUCKETS // 2
    d = jnp.maximum(dist, 1).astype(F32)
    large = max_exact + (jnp.log(d / max_exact) / math.log(MAX_DISTANCE / max_exact)
                         * (NUM_BUCKETS - max_exact)).astype(jnp.int32)
    large = jnp.minimum(large, NUM_BUCKETS - 1)
    return jnp.where(dist < max_exact, dist, large)


def _attention(q, k, v, rel_bias, lam_vecs, subln_g):
    B, S, _ = q.shape
    t = ATTN_TILE
    assert S % t == 0 and t >= MAX_DISTANCE
    nq = S // t
    dd = jnp.arange(t, dtype=jnp.int32)[:, None] - jnp.arange(t, dtype=jnp.int32)[None, :]
    buckets = jnp.arange(NUM_BUCKETS, dtype=jnp.int32)

    def bias_tile(dist):
        onehot = (_t5_bucket(dist)[..., None] == buckets).astype(F32)
        return jnp.einsum("ijb,bh->hij", onehot, rel_bias.astype(F32),
                          precision=lax.Precision.HIGHEST)

    bdiag = jnp.where(dd >= 0, bias_tile(jnp.maximum(dd, 0)) * LOG2_E, NEG_BIG)
    bnear = bias_tile(dd + t) * LOG2_E
    bdiag = jnp.concatenate([bdiag, bdiag], axis=1)
    bnear = jnp.concatenate([bnear, bnear], axis=1)
    bfar = jnp.broadcast_to((rel_bias[NUM_BUCKETS - 1] * LOG2_E)[:, None, None],
                            (DIFF_HEADS, 1, LANES))
    seq = pl.BlockSpec((None, S, HEAD_PAIR), lambda b, h: (b, 0, h))
    per_head = lambda r, c: pl.BlockSpec((None, r, c), lambda b, h: (h, 0, 0))
    return pl.pallas_call(
        functools.partial(_attn_kernel, nq=nq, t=t),
        grid=(B, DIFF_HEADS),
        in_specs=[seq, seq, seq, per_head(2 * t, t), per_head(2 * t, t), per_head(1, LANES),
                  pl.BlockSpec((4, DIFF_HEAD_DIM), lambda b, h: (0, 0)),
                  pl.BlockSpec((1, HEAD_PAIR), lambda b, h: (0, 0))],
        out_specs=seq,
        out_shape=jax.ShapeDtypeStruct((B, S, DIFF_HEADS * HEAD_PAIR), BF16),
        compiler_params=pltpu.CompilerParams(dimension_semantics=("parallel", "parallel"),
                                             vmem_limit_bytes=VMEM_LIMIT_BYTES),
        name="diff_attention",
    )(q, k, v, bdiag, bnear, bfar, lam_vecs, subln_g.reshape(1, HEAD_PAIR))


def _rwkv_kernel(rw_ref, mu_ref, w0_ref, w2_ref, a0_ref, a2_ref, g2_ref, kk_ref, ka_ref, rk_ref,
                 lg_ref, lb_ref, tri_ref, hsum_ref, o_ref,
                 h_s, prev_s, ah_s, rh_s, bt_s, kt_s, bb_s, kb_s, v_s, gl_s, y_s, bonus_s, gate_s,
                 *, width, n_pairs):
    L = SCAN_CHUNK
    TL = rw_ref.shape[0]
    j = pl.program_id(1)

    @pl.when(j == 0)
    def _():
        h_s[...] = jnp.zeros_like(h_s)
        prev_s[...] = jnp.zeros_like(prev_s)

    HT = SCAN_TASK
    n_tasks = TL // HT
    n_rounds = int(math.log2(L))
    n_chunks = HT // L
    n_blocks = n_chunks * n_pairs

    lane = lax.broadcasted_iota(jnp.int32, (L, LANES), 1)
    m0 = lane < RWKV_HEAD_DIM
    m1 = jnp.logical_not(m0)
    ri = lax.broadcasted_iota(jnp.int32, (2 * L, LANES), 0)
    ci = lax.broadcasted_iota(jnp.int32, (2 * L, LANES), 1)
    eye = ri == ci
    rl = lax.broadcasted_iota(jnp.int32, (L, LANES), 0)
    tril2 = (lane % L) <= rl
    stril2 = (lane % L) < rl
    eye2_f = jnp.where((lane % L) == rl, 1.0, 0.0).astype(F32)
    zeros = jnp.zeros((L, LANES), F32)

    sel = lambda m, x: jnp.where(m, x, zeros)
    cat0 = lambda *xs: jnp.concatenate(xs, 0)
    cat1 = lambda *xs: jnp.concatenate(xs, 1)
    block_diag = lambda x: cat0(sel(m0, x), sel(m1, x))
    row_in_half = lax.broadcasted_iota(jnp.int32, (HT, 1), 0)
    st = [dict() for _ in range(n_tasks)]
    hs = [h_s[pi] for pi in range(n_pairs)]

    def rows_of(t):
        return slice(t * HT, (t + 1) * HT)

    def block_ld(ref, t, i):
        c, pi = divmod(i, n_pairs)
        r0 = t * HT + c * L
        return ref[r0:r0 + L, pi * LANES:(pi + 1) * LANES]

    def prep_a(t):
        d, rows = st[t], rows_of(t)
        p = rw_ref[rows, :]
        before = prev_s[0:1, :] if t == 0 else rw_ref[t * HT - 1:t * HT, :]
        p_prev = jnp.where(row_in_half == 0, before, pltpu.roll(p, 1, 0))
        ps = p + (p_prev - p) * mu_ref[...]
        d["r"], d["k"], d["v"] = ps[:, 0:width], ps[:, width:2 * width], ps[:, 2 * width:3 * width]
        lr = ps[:, 3 * width:3 * width + DECAY_RANK + ICLR_RANK]
        gl = ps[:, 3 * width + DECAY_RANK + ICLR_RANK:]
        d["z"] = w0_ref[...] + _mm(jnp.tanh(lr), w2_ref[...])
        d["ai"] = a0_ref[...] + _mm(lr, a2_ref[...])
        gate_s[rows, :] = _mm(jax.nn.sigmoid(gl), g2_ref[...])
        d["kk"] = d["k"] * kk_ref[...]
        d["ss"] = _head_sum(d["kk"] * d["kk"], hsum_ref[...])

    def prep_b(t):
        d, rows = st[t], rows_of(t)
        d["logw"] = -jnp.exp(-_softplus(-d.pop("z")) - 0.5)
        iclr = jax.nn.sigmoid(d.pop("ai"))
        kk = d["kk"] * lax.rsqrt(jnp.maximum(d.pop("ss"), 1e-24))
        d["kk"] = kk
        d["b"] = kk * iclr
        d["k2"] = d.pop("k") * (1.0 + (iclr - 1.0) * ka_ref[...])
        bonus_s[rows, :] = _head_sum(d["r"] * d["k2"] * rk_ref[...], hsum_ref[...]) * d["v"]
        d["c"] = _sel_mm(tri_ref[...], d["logw"])

    def prep_c(t):
        d, rows = st[t], rows_of(t)
        c = d.pop("c")
        c_end = jnp.concatenate(
            [jnp.broadcast_to(c[i * L + L - 1:i * L + L, :], (L, width)) for i in range(n_chunks)], 0)
        inv_g = jnp.exp(-c)
        to_end = jnp.exp(c_end - c)
        b, k2 = d.pop("b"), d.pop("k2")
        ah_s[rows, :] = -d.pop("kk") * jnp.exp(c - d.pop("logw"))
        rh_s[rows, :] = d.pop("r") * jnp.exp(c)
        bt_s[rows, :] = b * inv_g
        kt_s[rows, :] = k2 * inv_g
        bb_s[rows, :] = b * to_end
        kb_s[rows, :] = k2 * to_end
        v_s[rows, :] = d.pop("v")
        gl_s[rows, :] = jnp.exp(c_end)

    def gram(t):
        d = st[t]
        ah = [block_ld(ah_s, t, i) for i in range(n_blocks)]
        rh = [block_ld(rh_s, t, i) for i in range(n_blocks)]
        bt = [block_ld(bt_s, t, i) for i in range(n_blocks)]
        kt = [block_ld(kt_s, t, i) for i in range(n_blocks)]
        d["ah0"] = [sel(m0, x) for x in ah]
        d["ah1"] = [sel(m1, x) for x in ah]
        d["rh"] = rh
        g0 = [_mm_nt(cat0(d["ah0"][i], sel(m0, rh[i])), cat0(bt[i], kt[i])) for i in range(n_blocks)]
        g1 = [_mm_nt(cat0(d["ah1"][i], sel(m1, rh[i])), cat0(kt[i], bt[i])) for i in range(n_blocks)]
        d["gb"] = [cat1(sel(tril2, x0[L:]), sel(tril2, x1[L:])) for x0, x1 in zip(g0, g1)]
        d["gt"] = [cat1(sel(m1 & stril2, x0[:L]), sel(m0 & stril2, x1[:L])) for x0, x1 in zip(g0, g1)]
        d["pw"] = [sel(stril2, jnp.where(m0, x0[:L], x1[:L])) for x0, x1 in zip(g0, g1)]
        d["tinv"] = [eye2_f + x for x in d["pw"]]
        d["round"] = 0

    def neumann(t):
        d = st[t]
        first, last = d["round"] == 0, d["round"] == n_rounds - 1
        d["round"] += 1
        if first:
            d["pw"] = [_mm(x, block_diag(x)) for x in d["pw"]]
            return
        out = [_mm(tm if last else cat0(tm, x), block_diag(x)) for x, tm in zip(d["pw"], d["tinv"])]
        d["tinv"] = [tm + o[:L] for tm, o in zip(d["tinv"], out)]
        if not last:
            d["pw"] = [o[L:] for o in out]

    def solve(t):
        d = st[t]
        d.pop("pw")
        vv = [block_ld(v_s, t, i) for i in range(n_blocks)]
        d["v0"] = [sel(m0, x) for x in vv]
        d["v1"] = [sel(m1, x) for x in vv]
        d["vv"] = vv
        makv = [_mm(g, cat0(a, a, b_, b_)) for g, a, b_ in zip(d.pop("gt"), d["v0"], d["v1"])]
        w = [_mm(tm, cat1(cat0(a0_, a1_), block_diag(mk)))
             for tm, a0_, a1_, mk in zip(d.pop("tinv"), d.pop("ah0"), d.pop("ah1"), makv)]
        d["w1"] = [x[:, :LANES] for x in w]
        d["w2"] = [x[:, LANES:] for x in w]

    def fold(t):
        d = st[t]
        lhs, cm, y0, gcol = [], [], [], []
        for i in range(n_blocks):
            w1, w2, vv = d["w1"][i], d["w2"][i], d["vv"][i]
            ac = _mm_tn(cat0(block_ld(bb_s, t, i), block_ld(kb_s, t, i)),
                        cat1(cat0(w1, zeros), cat0(w2, vv)))
            ry0 = _mm(d["gb"][i], cat0(cat1(sel(m0, w1), sel(m0, w2)), cat1(zeros, d["v0"][i]),
                                       cat1(zeros, d["v1"][i]), cat1(sel(m1, w1), sel(m1, w2))))
            a_mat = jnp.where(m0, ac[:L, :LANES], ac[L:, :LANES])
            cm.append(jnp.where(m0, ac[:L, LANES:], ac[L:, LANES:]))
            lhs.append(cat0(a_mat, d["rh"][i] + ry0[:, :LANES]).astype(BF16))
            y0.append(ry0[:, LANES:])
            g_end = block_ld(gl_s, t, i)[0:1, :]
            g_rows = jnp.sum(jnp.where(eye, jnp.broadcast_to(g_end, (2 * L, LANES)), 0.0),
                             axis=1, keepdims=True)
            gcol.append(jnp.where(m0, g_rows[:L], g_rows[L:]))
        for key in ("w1", "w2", "vv", "v0", "v1", "gb", "rh"):
            d.pop(key)
        d.update(lhs=lhs, cm=cm, y0=y0, gcol=gcol)

    def scan(t, c):
        d = st[t]
        r0 = t * HT + c * L
        for pi in range(n_pairs):
            i = c * n_pairs + pi
            h = hs[pi]
            out = _mm(d["lhs"][i], block_diag(h))
            hs[pi] = d["gcol"][i] * h + out[:L] + d["cm"][i]
            y_s[r0:r0 + L, pi * LANES:(pi + 1) * LANES] = out[L:] + d["y0"][i]

    def finish(t):
        rows = rows_of(t)
        y = y_s[rows, :]
        inv_n = 1.0 / RWKV_HEAD_DIM
        mean = _head_sum(y, hsum_ref[...]) * inv_n
        yc = y - mean
        var = _head_sum(yc * yc, hsum_ref[...]) * inv_n
        yn = yc * lax.rsqrt(var + LNX_EPS) * lg_ref[...] + lb_ref[...]
        o_ref[rows, :] = ((yn + bonus_s[rows, :]) * gate_s[rows, :]).astype(o_ref.dtype)

    assert n_rounds >= max(n_chunks, 3)
    prep_a(0); prep_b(0); prep_c(0)
    for t in range(n_tasks):
        nxt = t + 1 if t + 1 < n_tasks else None
        prv = t - 1 if t >= 1 else None
        feature_stages = iter((prep_b, prep_c))
        if nxt is not None:
            prep_a(nxt)
        gram(t)
        for rnd in range(n_rounds):
            neumann(t)
            if prv is not None and rnd < n_chunks:
                scan(prv, rnd)
            if nxt is not None and rnd in (0, 2):
                next(feature_stages)(nxt)
        if prv is not None:
            finish(prv)
        solve(t); fold(t)
    for c in range(n_chunks):
        scan(n_tasks - 1, c)
    finish(n_tasks - 1)
    prev_s[0:1, :] = rw_ref[TL - 1:TL, :]
    for pi in range(n_pairs):
        h_s[pi] = hs[pi]


def _rwkv(rw, mu, w0, w2, a0, a2, g2, k_k, k_a, r_k, lnx_g, lnx_b):
    B, S, n_rw = rw.shape
    width = w0.shape[-1]
    n_pairs = width // LANES
    L, TL = SCAN_CHUNK, min(SCAN_TILE, S)
    assert S % TL == 0 and TL % SCAN_TASK == 0
    assert n_rw == 3 * width + DECAY_RANK + ICLR_RANK + GATE_RANK
    assert DECAY_RANK + ICLR_RANK == LANES
    ti = jnp.arange(SCAN_TASK)
    same_chunk = (ti[:, None] // L) == (ti[None, :] // L)
    tri = (same_chunk & (ti[None, :] <= ti[:, None])).astype(BF16)
    ch = jnp.arange(LANES) // RWKV_HEAD_DIM
    hsum = (ch[:, None] == ch[None, :]).astype(BF16)
    w2p = jnp.concatenate([w2, jnp.zeros((ICLR_RANK, width), w2.dtype)], 0).astype(BF16)
    a2p = jnp.concatenate([jnp.zeros((DECAY_RANK, width), a2.dtype), a2], 0).astype(BF16)
    vec = lambda a: a.reshape(1, -1).astype(F32)
    tile = pl.BlockSpec((None, TL, n_rw), lambda b, j: (b, j, 0))
    feat = lambda: pltpu.VMEM((TL, width), F32)
    return pl.pallas_call(
        functools.partial(_rwkv_kernel, width=width, n_pairs=n_pairs),
        grid=(B, S // TL),
        in_specs=[tile, _const_spec((1, n_rw)), _const_spec((1, width)), _const_spec((LANES, width)),
                  _const_spec((1, width)), _const_spec((LANES, width)), _const_spec((GATE_RANK, width)),
                  _const_spec((1, width)), _const_spec((1, width)), _const_spec((1, width)),
                  _const_spec((1, width)), _const_spec((1, width)),
                  _const_spec((SCAN_TASK, SCAN_TASK)), _const_spec((LANES, LANES))],
        out_specs=pl.BlockSpec((None, TL, width), lambda b, j: (b, j, 0)),
        out_shape=jax.ShapeDtypeStruct((B, S, width), BF16),
        scratch_shapes=[pltpu.VMEM((n_pairs, L, LANES), F32), pltpu.VMEM((8, n_rw), F32)]
                       + [feat() for _ in range(11)],
        compiler_params=pltpu.CompilerParams(dimension_semantics=("parallel", "arbitrary"),
                                             vmem_limit_bytes=VMEM_LIMIT_BYTES),
        name="rwkv7_chunked",
    )(rw, vec(mu), vec(w0), w2p, vec(a0), a2p, g2.astype(BF16), vec(k_k), vec(k_a), vec(r_k),
      vec(lnx_g), vec(lnx_b), tri, hsum)


def _post_kernel(x_ref, lng_ref, lnb_ref, ya_ref, yb_ref, ga_ref, gb_ref, wua_ref, wub_ref, wo_ref,
                 l1g_ref, l1b_ref, wg_ref, wu_ref, wd_ref, l2g_ref, l2b_ref, o_ref,
                 z_even, z_odd, *, ffn_cuts):
    i = pl.program_id(0)
    dot = functools.partial(jnp.dot, preferred_element_type=F32)

    @pl.when(i == 0)
    def _():
        z_even[...] = jnp.zeros_like(z_even)

    def ffn_part(h1b, lo, hi):
        gate = dot(h1b, wg_ref[:, lo:hi])
        up = dot(h1b, wu_ref[:, lo:hi])
        act = (gate * jax.nn.sigmoid(gate) * up).astype(BF16)
        return dot(act, wd_ref[lo:hi, :])

    def step(z_prev, z_next):
        up_a = dot(ya_ref[...], wua_ref[...])
        up_b = dot(yb_ref[...], wub_ref[...])
        h1 = _layer_norm(z_prev[...], l1g_ref[...], l1b_ref[...])
        h1b = h1.astype(BF16)
        ffn = ffn_part(h1b, ffn_cuts[0], ffn_cuts[1])
        merged = ga_ref[...].astype(F32) * up_a + gb_ref[...].astype(F32) * up_b
        for lo, hi in zip(ffn_cuts[1:-1], ffn_cuts[2:]):
            ffn = ffn + ffn_part(h1b, lo, hi)
        mix = dot(merged.astype(BF16), wo_ref[...])
        o_ref[...] = _layer_norm(DEEPNORM_ALPHA * h1 + ffn, l2g_ref[...], l2b_ref[...])
        h = _layer_norm(x_ref[...], lng_ref[...], lnb_ref[...])
        z_next[...] = DEEPNORM_ALPHA * h + mix

    @pl.when(i % 2 == 0)
    def _():
        step(z_even, z_odd)

    @pl.when(i % 2 == 1)
    def _():
        step(z_odd, z_even)


def _post(x2, ln_g, ln_b, ya, yb, ga, gb, w_up_a, w_up_b, w_out, ln1_g, ln1_b,
          w_gate, w_up, w_down, ln2_g, ln2_b):
    T, D = x2.shape
    hidden = w_gate.shape[1]
    assert hidden % MXU_DEPTH == 0
    ffn_cuts = (0, (hidden // MXU_DEPTH // 2) * MXU_DEPTH, hidden)
    tm = ROW_TILE
    n_tiles = T // tm
    row = lambda n: pl.BlockSpec((tm, n), lambda i: (jnp.minimum(i, n_tiles - 1), 0))
    vec = lambda a: a.reshape(1, D)
    cs = lambda a: _const_spec(a.shape)
    wts = [w.astype(BF16) for w in (w_up_a, w_up_b, w_out, w_gate, w_up, w_down)]
    wua, wub, wo, wg, wu, wd = wts
    return pl.pallas_call(
        functools.partial(_post_kernel, ffn_cuts=ffn_cuts),
        grid=(n_tiles + 1,),
        in_specs=[row(D), _const_spec((1, D)), _const_spec((1, D)), row(ya.shape[1]), row(yb.shape[1]),
                  row(D), row(D), cs(wua), cs(wub), cs(wo), _const_spec((1, D)), _const_spec((1, D)),
                  cs(wg), cs(wu), cs(wd), _const_spec((1, D)), _const_spec((1, D))],
        out_specs=pl.BlockSpec((tm, D), lambda i: (jnp.maximum(i - 1, 0), 0)),
        out_shape=jax.ShapeDtypeStruct((T, D), F32),
        scratch_shapes=[pltpu.VMEM((tm, D), F32), pltpu.VMEM((tm, D), F32)],
        compiler_params=pltpu.CompilerParams(dimension_semantics=("arbitrary",),
                                             vmem_limit_bytes=VMEM_LIMIT_BYTES),
        name="merge_ffn",
    )(x2, vec(ln_g), vec(ln_b), ya, yb, ga, gb, wua, wub, wo, vec(ln1_g), vec(ln1_b),
      wg, wu, wd, vec(ln2_g), vec(ln2_b))


def kernel(x, ln_in_g, ln_in_b, rel_bias, w_in, diff_lam_q1, diff_lam_k1, diff_lam_q2, diff_lam_k2, diff_subln_g, rwkv_mu, rwkv_w0, rwkv_w2, rwkv_a0, rwkv_a2, rwkv_g2, rwkv_k_k, rwkv_k_a, rwkv_r_k, rwkv_lnx_g, rwkv_lnx_b, w_up_a, w_up_b, w_out, ln1_g, ln1_b, ffn_w_gate, ffn_w_up, ffn_w_down, ln2_g, ln2_b):
    B, S, D = x.shape
    T = B * S
    width = rwkv_w0.shape[-1]
    n_qk = DIFF_HEADS * HEAD_PAIR
    n_rw = 3 * width + DECAY_RANK + ICLR_RANK + GATE_RANK
    sizes = (n_qk, n_qk, n_qk, n_rw, D, D)
    assert w_in.shape[0] == 1 and w_in.shape[2] == sum(sizes), "single-layer model expected"
    x2 = x.reshape(T, D)
    q, k, v, rw, ga, gb = _inproj(x2, ln_in_g, ln_in_b, w_in[0].astype(BF16), sizes)
    lam_vecs = jnp.concatenate([diff_lam_q1, diff_lam_k1, diff_lam_q2, diff_lam_k2], 0).astype(F32)
    ya = _attention(q.reshape(B, S, n_qk), k.reshape(B, S, n_qk), v.reshape(B, S, n_qk),
                    rel_bias, lam_vecs, diff_subln_g[0])
    yb = _rwkv(rw.reshape(B, S, n_rw), rwkv_mu[0], rwkv_w0[0], rwkv_w2[0], rwkv_a0[0], rwkv_a2[0],
               rwkv_g2[0], rwkv_k_k[0], rwkv_k_a[0], rwkv_r_k[0], rwkv_lnx_g[0], rwkv_lnx_b[0])
    out = _post(x2, ln_in_g, ln_in_b, ya.reshape(T, n_qk), yb.reshape(T, width), ga, gb,
                w_up_a[0], w_up_b[0], w_out[0], ln1_g[0], ln1_b[0],
                ffn_w_gate[0], ffn_w_up[0], ffn_w_down[0], ln2_g[0], ln2_b[0])
    return out.reshape(B, S, D)
```

```python
import functools
import math

import jax
import jax.numpy as jnp
from jax import lax
from jax.experimental import pallas as pl
from jax.experimental.pallas import tpu as pltpu

F32 = jnp.float32
BF16 = jnp.bfloat16

DIFF_HEADS = 4
DIFF_HEAD_DIM = 64
HEAD_PAIR = 2 * DIFF_HEAD_DIM
RWKV_HEAD_DIM = 64
DECAY_RANK = 64
ICLR_RANK = 64
GATE_RANK = 128
NUM_BUCKETS = 32
MAX_DISTANCE = 128
LN_EPS = 1e-5
LNX_EPS = 64e-5
NEG_BIG = -1e30
DEPTH = 1
DEEPNORM_ALPHA = (2.0 * DEPTH) ** 0.25
LAMBDA_INIT = 0.8 - 0.6 * math.exp(-0.3 * 0)
LOG2_E = math.log2(math.e)

LANES = 128
MXU_DEPTH = 256
VMEM_LIMIT_BYTES = 56 * 1024 * 1024
ROW_TILE = 256
ATTN_TILE = 256
SCAN_CHUNK = 64
SCAN_TASK = 256
SCAN_TILE = 512


def _mm(a, b):
    return jnp.dot(a.astype(BF16), b.astype(BF16), preferred_element_type=F32)


def _mm_nt(a, b):
    return lax.dot_general(a.astype(BF16), b.astype(BF16), (((1,), (1,)), ((), ())),
                           preferred_element_type=F32)


def _mm_tn(a, b):
    return lax.dot_general(a.astype(BF16), b.astype(BF16), (((0,), (0,)), ((), ())),
                           preferred_element_type=F32)


def _split2(x):
    hi = x.astype(BF16)
    lo = (x - hi.astype(F32)).astype(BF16)
    return hi, lo


def _sel_mm(sel, x):
    hi, lo = _split2(x)
    d = functools.partial(jnp.dot, preferred_element_type=F32)
    return d(sel, hi) + d(sel, lo)


def _head_sum(x, sel):
    xb = x.astype(BF16)
    return jnp.concatenate(
        [jnp.dot(xb[:, c0:c0 + LANES], sel, preferred_element_type=F32)
         for c0 in range(0, x.shape[1], LANES)], axis=1)


def _layer_norm(x, g, b, eps=LN_EPS):
    mu = jnp.mean(x, -1, keepdims=True)
    xc = x - mu
    var = jnp.mean(xc * xc, -1, keepdims=True)
    return xc * lax.rsqrt(var + eps) * g + b


def _const_spec(shape):
    return pl.BlockSpec(shape, lambda *_: (0,) * len(shape), pipeline_mode=pl.Buffered(1))


def _inproj_kernel(x_ref, g_ref, b_ref, w_ref, q_ref, k_ref, v_ref, rw_ref, ga_ref, gb_ref,
                   h_even, h_odd, *, splits):
    i = pl.program_id(0)

    @pl.when(i == 0)
    def _():
        h_even[...] = jnp.zeros_like(h_even)

    def step(h_prev, h_next):
        h_next[...] = _layer_norm(x_ref[...], g_ref[...], b_ref[...]).astype(BF16)

        def proj(lo, hi):
            return jnp.dot(h_prev[...], w_ref[:, lo:hi], preferred_element_type=F32)

        s_q, s_k, s_v, s_rw, s_ga = splits
        ga_ref[...] = jax.nn.sigmoid(proj(s_rw, s_ga)).astype(BF16)
        gb_ref[...] = jax.nn.sigmoid(proj(s_ga, w_ref.shape[1])).astype(BF16)
        q_ref[...] = (proj(0, s_q) * (DIFF_HEAD_DIM ** -0.5 * LOG2_E)).astype(BF16)
        k_ref[...] = proj(s_q, s_k).astype(BF16)
        v_ref[...] = proj(s_k, s_v).astype(BF16)
        rw_ref[...] = proj(s_v, s_rw)

    @pl.when(i % 2 == 0)
    def _():
        step(h_even, h_odd)

    @pl.when(i % 2 == 1)
    def _():
        step(h_odd, h_even)


def _inproj(x2, ln_g, ln_b, w_in_bf, sizes):
    T, D = x2.shape
    n_q, n_k, n_v, n_rw, n_ga, n_gb = sizes
    acc, cuts = 0, []
    for s in sizes[:-1]:
        acc += s
        cuts.append(acc)
    splits = tuple(cuts)
    tm = ROW_TILE
    n_tiles = T // tm
    row_in = pl.BlockSpec((tm, D), lambda i: (jnp.minimum(i, n_tiles - 1), 0))
    row = lambda n: pl.BlockSpec((tm, n), lambda i: (jnp.maximum(i - 1, 0), 0))
    return pl.pallas_call(
        functools.partial(_inproj_kernel, splits=splits),
        grid=(n_tiles + 1,),
        in_specs=[row_in, _const_spec((1, D)), _const_spec((1, D)), _const_spec(w_in_bf.shape)],
        out_specs=[row(n_q), row(n_k), row(n_v), row(n_rw), row(n_ga), row(n_gb)],
        out_shape=[jax.ShapeDtypeStruct((T, n_q), BF16), jax.ShapeDtypeStruct((T, n_k), BF16),
                   jax.ShapeDtypeStruct((T, n_v), BF16), jax.ShapeDtypeStruct((T, n_rw), F32),
                   jax.ShapeDtypeStruct((T, n_ga), BF16), jax.ShapeDtypeStruct((T, n_gb), BF16)],
        scratch_shapes=[pltpu.VMEM((tm, D), BF16), pltpu.VMEM((tm, D), BF16)],
        compiler_params=pltpu.CompilerParams(dimension_semantics=("arbitrary",),
                                             vmem_limit_bytes=VMEM_LIMIT_BYTES),
        name="ln_inproj",
    )(x2, ln_g.reshape(1, D), ln_b.reshape(1, D), w_in_bf)


def _attn_kernel(q_ref, k_ref, v_ref, bdiag_ref, bnear_ref, bfar_ref, lam_ref, g_ref, o_ref,
                 *, nq, t):
    lane = lax.broadcasted_iota(jnp.int32, (t, HEAD_PAIR), 1)
    lo = lane < DIFF_HEAD_DIM
    lv = lam_ref[...]
    lam = (jnp.exp(jnp.sum(lv[0:1] * lv[1:2], keepdims=True))
           - jnp.exp(jnp.sum(lv[2:3] * lv[3:4], keepdims=True)) + LAMBDA_INIT)
    bfar = bfar_ref[...][:, 0:1]
    zero = jnp.zeros((), BF16)
    rowmax = lambda a: jnp.max(a, -1, keepdims=True)
    rowsum = lambda a: jnp.sum(a, -1, keepdims=True)

    def logits(qi):
        q = q_ref[qi * t:(qi + 1) * t, :]
        qs = jnp.concatenate([jnp.where(lo, q, zero), jnp.where(lo, zero, q)], axis=0)
        return lax.dot_general(qs, k_ref[0:(qi + 1) * t, :], (((1,), (1,)), ((), ())),
                               preferred_element_type=F32)

    def softmax(qi, s):
        s_diag = s[:, qi * t:] + bdiag_ref[...]
        m = rowmax(s_diag)
        if qi >= 1:
            s_near = s[:, (qi - 1) * t:qi * t] + bnear_ref[...]
            m = jnp.maximum(m, rowmax(s_near))
        if qi >= 2:
            s_far = s[:, :(qi - 1) * t]
            m = jnp.maximum(m, rowmax(s_far) + bfar)
        parts = []
        if qi >= 2:
            parts.append(jnp.exp2(s_far - (m - bfar)))
        if qi >= 1:
            parts.append(jnp.exp2(s_near - m))
        parts.append(jnp.exp2(s_diag - m))
        l = sum(rowsum(p) for p in parts)
        return jnp.concatenate([x.astype(BF16) for x in parts], axis=1), l

    def values(qi, pl_):
        p, l = pl_
        acc = jnp.dot(p, v_ref[0:(qi + 1) * t, :], preferred_element_type=F32)
        o = acc[:t] * (1.0 / l[:t]) - acc[t:] * (lam / l[t:])
        ms = jnp.mean(o * o, -1, keepdims=True)
        o = o * lax.rsqrt(ms + LN_EPS) * g_ref[...] * (1.0 - LAMBDA_INIT)
        o_ref[qi * t:(qi + 1) * t, :] = o.astype(o_ref.dtype)

    s_q, p_q = {}, {}
    for step in range(nq + 2):
        if step < nq:
            s_q[step] = logits(step)
        if 0 <= step - 1 < nq:
            p_q[step - 1] = softmax(step - 1, s_q.pop(step - 1))
        if 0 <= step - 2 < nq:
            values(step - 2, p_q.pop(step - 2))


def _t5_bucket(dist):
    max_exact = NUM_BUCKETS // 2
    d = jnp.maximum(dist, 1).astype(F32)
    large = max_exact + (jnp.log(d / max_exact) / math.log(MAX_DISTANCE / max_exact)
                         * (NUM_BUCKETS - max_exact)).astype(jnp.int32)
    large = jnp.minimum(large, NUM_BUCKETS - 1)
    return jnp.where(dist < max_exact, dist, large)


def _attention(q, k, v, rel_bias, lam_vecs, subln_g):
    B, S, _ = q.shape
    t = ATTN_TILE
    assert S % t == 0 and t >= MAX_DISTANCE
    nq = S // t
    dd = jnp.arange(t, dtype=jnp.int32)[:, None] - jnp.arange(t, dtype=jnp.int32)[None, :]
    buckets = jnp.arange(NUM_BUCKETS, dtype=jnp.int32)

    def bias_tile(dist):
        onehot = (_t5_bucket(dist)[..., None] == buckets).astype(F32)
        return jnp.einsum("ijb,bh->hij", onehot, rel_bias.astype(F32),
                          precision=lax.Precision.HIGHEST)

    bdiag = jnp.where(dd >= 0, bias_tile(jnp.maximum(dd, 0)) * LOG2_E, NEG_BIG)
    bnear = bias_tile(dd + t) * LOG2_E
    bdiag = jnp.concatenate([bdiag, bdiag], axis=1)
    bnear = jnp.concatenate([bnear, bnear], axis=1)
    bfar = jnp.broadcast_to((rel_bias[NUM_BUCKETS - 1] * LOG2_E)[:, None, None],
                            (DIFF_HEADS, 1, LANES))
    seq = pl.BlockSpec((None, S, HEAD_PAIR), lambda b, h: (b, 0, h))
    per_head = lambda r, c: pl.BlockSpec((None, r, c), lambda b, h: (h, 0, 0))
    return pl.pallas_call(
        functools.partial(_attn_kernel, nq=nq, t=t),
        grid=(B, DIFF_HEADS),
        in_specs=[seq, seq, seq, per_head(2 * t, t), per_head(2 * t, t), per_head(1, LANES),
                  pl.BlockSpec((4, DIFF_HEAD_DIM), lambda b, h: (0, 0)),
                  pl.BlockSpec((1, HEAD_PAIR), lambda b, h: (0, 0))],
        out_specs=seq,
        out_shape=jax.ShapeDtypeStruct((B, S, DIFF_HEADS * HEAD_PAIR), BF16),
        compiler_params=pltpu.CompilerParams(dimension_semantics=("parallel", "parallel"),
                                             vmem_limit_bytes=VMEM_LIMIT_BYTES),
        name="diff_attention",
    )(q, k, v, bdiag, bnear, bfar, lam_vecs, subln_g.reshape(1, HEAD_PAIR))


def _rwkv_kernel(rw_ref, mu_ref, w0_ref, w2_ref, a0_ref, a2_ref, g2_ref, kk_ref, ka_ref, rk_ref,
                 lg_ref, lb_ref, tri_ref, hsum_ref, o_ref,
                 h_s, prev_s, ah_s, rh_s, bt_s, kt_s, bb_s, kb_s, v_s, gl_s, y_s, bonus_s, gate_s,
                 *, width, n_pairs):
    L = SCAN_CHUNK
    TL = rw_ref.shape[0]
    j = pl.program_id(1)

    @pl.when(j == 0)
    def _():
        h_s[...] = jnp.zeros_like(h_s)
        prev_s[...] = jnp.zeros_like(prev_s)

    HT = SCAN_TASK
    n_tasks = TL // HT
    n_rounds = int(math.log2(L))
    n_chunks = HT // L
    n_blocks = n_chunks * n_pairs

    lane = lax.broadcasted_iota(jnp.int32, (L, LANES), 1)
    m0 = lane < RWKV_HEAD_DIM
    m1 = jnp.logical_not(m0)
    ri = lax.broadcasted_iota(jnp.int32, (2 * L, LANES), 0)
    ci = lax.broadcasted_iota(jnp.int32, (2 * L, LANES), 1)
    eye = ri == ci
    rl = lax.broadcasted_iota(jnp.int32, (L, LANES), 0)
    tril2 = (lane % L) <= rl
    stril2 = (lane % L) < rl
    eye2_f = jnp.where((lane % L) == rl, 1.0, 0.0).astype(F32)
    zeros = jnp.zeros((L, LANES), F32)

    sel = lambda m, x: jnp.where(m, x, zeros)
    cat0 = lambda *xs: jnp.concatenate(xs, 0)
    cat1 = lambda *xs: jnp.concatenate(xs, 1)
    block_diag = lambda x: cat0(sel(m0, x), sel(m1, x))
    swap_halves = lambda x: pltpu.roll(x, RWKV_HEAD_DIM, 1)
    row_in_half = lax.broadcasted_iota(jnp.int32, (HT, 1), 0)
    st = [dict() for _ in range(n_tasks)]
    hs = [h_s[pi] for pi in range(n_pairs)]

    def rows_of(t):
        return slice(t * HT, (t + 1) * HT)

    def block_ld(ref, t, i):
        c, pi = divmod(i, n_pairs)
        r0 = t * HT + c * L
        return ref[r0:r0 + L, pi * LANES:(pi + 1) * LANES]

    def prep_a(t):
        d, rows = st[t], rows_of(t)
        p = rw_ref[rows, :]
        before = prev_s[0:1, :] if t == 0 else rw_ref[t * HT - 1:t * HT, :]
        p_prev = jnp.where(row_in_half == 0, before, pltpu.roll(p, 1, 0))
        ps = p + (p_prev - p) * mu_ref[...]
        d["r"], d["k"], d["v"] = ps[:, 0:width], ps[:, width:2 * width], ps[:, 2 * width:3 * width]
        lr = ps[:, 3 * width:3 * width + DECAY_RANK + ICLR_RANK]
        gl = ps[:, 3 * width + DECAY_RANK + ICLR_RANK:]
        d["z"] = w0_ref[...] + _mm(jnp.tanh(lr), w2_ref[...])
        d["ai"] = a0_ref[...] + _mm(lr, a2_ref[...])
        gate_s[rows, :] = _mm(jax.nn.sigmoid(gl), g2_ref[...])
        d["kk"] = d["k"] * kk_ref[...]
        d["ss"] = _head_sum(d["kk"] * d["kk"], hsum_ref[...])

    def prep_b(t):
        d, rows = st[t], rows_of(t)
        d["logw"] = (-math.exp(-0.5) * LOG2_E) * jax.nn.sigmoid(d.pop("z"))
        iclr = jax.nn.sigmoid(d.pop("ai"))
        kk = d["kk"] * lax.rsqrt(jnp.maximum(d.pop("ss"), 1e-24))
        d["kk"] = kk
        d["b"] = kk * iclr
        d["k2"] = d.pop("k") * (1.0 + (iclr - 1.0) * ka_ref[...])
        bonus_s[rows, :] = _head_sum(d["r"] * d["k2"] * rk_ref[...], hsum_ref[...]) * d["v"]
        d["c"] = _sel_mm(tri_ref[...], d["logw"])

    def prep_c(t):
        d, rows = st[t], rows_of(t)
        c = d.pop("c")
        c_last = [c[i * L + L - 1:i * L + L, :] for i in range(n_chunks)]
        c_end = jnp.concatenate([jnp.broadcast_to(x, (L, width)) for x in c_last], 0)
        inv_g = jnp.exp2(-c)
        to_end = jnp.exp2(c_end - c)
        b, k2 = d.pop("b"), d.pop("k2")
        ah_s[rows, :] = -d.pop("kk") * jnp.exp2(c - d.pop("logw"))
        rh_s[rows, :] = d.pop("r") * jnp.exp2(c)
        bt_s[rows, :] = b * inv_g
        kt_s[rows, :] = k2 * inv_g
        bb_s[rows, :] = b * to_end
        kb_s[rows, :] = k2 * to_end
        v_s[rows, :] = d.pop("v")
        for i, x in enumerate(c_last):
            gl_s[t * n_chunks + i:t * n_chunks + i + 1, :] = jnp.exp2(x)

    def gram(t):
        d = st[t]
        ah = [block_ld(ah_s, t, i) for i in range(n_blocks)]
        rh = [block_ld(rh_s, t, i) for i in range(n_blocks)]
        bt = [block_ld(bt_s, t, i) for i in range(n_blocks)]
        kt = [block_ld(kt_s, t, i) for i in range(n_blocks)]
        d["ah0"] = [sel(m0, x) for x in ah]
        d["ah1"] = [sel(m1, x) for x in ah]
        d["rh"] = rh
        g = [_mm_nt(cat0(d["ah0"][i], sel(m0, rh[i]), d["ah1"][i], sel(m1, rh[i])), cat0(bt[i], kt[i]))
             for i in range(n_blocks)]
        pair = lambda x0, x1: jnp.where(m0, x0, swap_halves(x1))
        pair_r = lambda x0, x1: jnp.where(m0, swap_halves(x0), x1)
        d["ak"] = [sel(stril2, pair_r(x[:L], x[2 * L:3 * L])) for x in g]
        d["rb"] = [sel(tril2, pair(x[L:2 * L], x[3 * L:])) for x in g]
        d["rk"] = [sel(tril2, pair_r(x[L:2 * L], x[3 * L:])) for x in g]
        d["pw"] = [sel(stril2, pair(x[:L], x[2 * L:3 * L])) for x in g]
        d["tinv"] = [eye2_f + x for x in d["pw"]]
        d["round"] = 0

    def neumann(t):
        d = st[t]
        first, last = d["round"] == 0, d["round"] == n_rounds - 1
        d["round"] += 1
        if first:
            d["pw"] = [_mm(x, block_diag(x)) for x in d["pw"]]
            return
        out = [_mm(tm if last else cat0(tm, x), block_diag(x)) for x, tm in zip(d["pw"], d["tinv"])]
        d["tinv"] = [tm + o[:L] for tm, o in zip(d["tinv"], out)]
        if not last:
            d["pw"] = [o[L:] for o in out]

    def solve(t):
        d = st[t]
        d.pop("pw")
        vv = [block_ld(v_s, t, i) for i in range(n_blocks)]
        d["vv"] = vv
        kv = [_mm(cat0(ak, rk), block_diag(x)) for ak, rk, x in zip(d.pop("ak"), d.pop("rk"), vv)]
        d["rkv"] = [x[L:] for x in kv]
        w = [_mm(tm, cat1(cat0(a0_, a1_), block_diag(x[:L])))
             for tm, a0_, a1_, x in zip(d.pop("tinv"), d.pop("ah0"), d.pop("ah1"), kv)]
        d["w1"] = [x[:, :LANES] for x in w]
        d["w2"] = [x[:, LANES:] for x in w]

    def fold(t):
        d = st[t]
        lhs, cm, y0, gcol = [], [], [], []
        for i in range(n_blocks):
            w1, w2, vv = d["w1"][i], d["w2"][i], d["vv"][i]
            ac = _mm_tn(cat0(block_ld(bb_s, t, i), block_ld(kb_s, t, i)),
                        cat1(cat0(w1, zeros), cat0(w2, vv)))
            ry0 = _mm(d["rb"][i], cat1(block_diag(w1), block_diag(w2)))
            a_mat = jnp.where(m0, ac[:L, :LANES], ac[L:, :LANES])
            cm.append(jnp.where(m0, ac[:L, LANES:], ac[L:, LANES:]))
            lhs.append(cat0(a_mat, d["rh"][i] + ry0[:, :LANES]).astype(BF16))
            y0.append(ry0[:, LANES:] + d["rkv"][i])
            ch, pi = divmod(i, n_pairs)
            g_end = gl_s[t * n_chunks + ch:t * n_chunks + ch + 1, pi * LANES:(pi + 1) * LANES]
            g_rows = jnp.sum(jnp.where(eye, jnp.broadcast_to(g_end, (2 * L, LANES)), 0.0),
                             axis=1, keepdims=True)
            gcol.append(jnp.where(m0, g_rows[:L], g_rows[L:]))
        for key in ("w1", "w2", "vv", "rb", "rkv", "rh"):
            d.pop(key)
        d.update(lhs=lhs, cm=cm, y0=y0, gcol=gcol)

    def scan(t, c):
        d = st[t]
        r0 = t * HT + c * L
        for pi in range(n_pairs):
            i = c * n_pairs + pi
            h = hs[pi]
            out = _mm(d["lhs"][i], block_diag(h))
            hs[pi] = d["gcol"][i] * h + out[:L] + d["cm"][i]
            y_s[r0:r0 + L, pi * LANES:(pi + 1) * LANES] = out[L:] + d["y0"][i]

    def finish(t):
        rows = rows_of(t)
        y = y_s[rows, :]
        inv_n = 1.0 / RWKV_HEAD_DIM
        mean = _head_sum(y, hsum_ref[...]) * inv_n
        yc = y - mean
        var = _head_sum(yc * yc, hsum_ref[...]) * inv_n
        yn = yc * lax.rsqrt(var + LNX_EPS) * lg_ref[...] + lb_ref[...]
        o_ref[rows, :] = ((yn + bonus_s[rows, :]) * gate_s[rows, :]).astype(o_ref.dtype)

    assert n_rounds >= max(n_chunks, 3)
    prep_a(0); prep_b(0); prep_c(0)
    for t in range(n_tasks):
        nxt = t + 1 if t + 1 < n_tasks else None
        prv = t - 1 if t >= 1 else None
        feature_stages = iter((prep_b, prep_c))
        if nxt is not None:
            prep_a(nxt)
        gram(t)
        for rnd in range(n_rounds):
            neumann(t)
            if prv is not None and rnd < n_chunks:
                scan(prv, rnd)
            if nxt is not None and rnd in (0, 2):
                next(feature_stages)(nxt)
        if prv is not None:
            finish(prv)
        solve(t); fold(t)
    for c in range(n_chunks):
        scan(n_tasks - 1, c)
    finish(n_tasks - 1)
    prev_s[0:1, :] = rw_ref[TL - 1:TL, :]
    for pi in range(n_pairs):
        h_s[pi] = hs[pi]


def _rwkv(rw, mu, w0, w2, a0, a2, g2, k_k, k_a, r_k, lnx_g, lnx_b):
    B, S, n_rw = rw.shape
    width = w0.shape[-1]
    n_pairs = width // LANES
    L, TL = SCAN_CHUNK, min(SCAN_TILE, S)
    assert S % TL == 0 and TL % SCAN_TASK == 0
    assert n_rw == 3 * width + DECAY_RANK + ICLR_RANK + GATE_RANK
    assert DECAY_RANK + ICLR_RANK == LANES
    ti = jnp.arange(SCAN_TASK)
    same_chunk = (ti[:, None] // L) == (ti[None, :] // L)
    tri = (same_chunk & (ti[None, :] <= ti[:, None])).astype(BF16)
    ch = jnp.arange(LANES) // RWKV_HEAD_DIM
    hsum = (ch[:, None] == ch[None, :]).astype(BF16)
    w2p = jnp.concatenate([w2, jnp.zeros((ICLR_RANK, width), w2.dtype)], 0).astype(BF16)
    a2p = jnp.concatenate([jnp.zeros((DECAY_RANK, width), a2.dtype), a2], 0).astype(BF16)
    vec = lambda a: a.reshape(1, -1).astype(F32)
    tile = pl.BlockSpec((None, TL, n_rw), lambda b, j: (b, j, 0))
    feat = lambda: pltpu.VMEM((TL, width), F32)
    return pl.pallas_call(
        functools.partial(_rwkv_kernel, width=width, n_pairs=n_pairs),
        grid=(B, S // TL),
        in_specs=[tile, _const_spec((1, n_rw)), _const_spec((1, width)), _const_spec((LANES, width)),
                  _const_spec((1, width)), _const_spec((LANES, width)), _const_spec((GATE_RANK, width)),
                  _const_spec((1, width)), _const_spec((1, width)), _const_spec((1, width)),
                  _const_spec((1, width)), _const_spec((1, width)),
                  _const_spec((SCAN_TASK, SCAN_TASK)), _const_spec((LANES, LANES))],
        out_specs=pl.BlockSpec((None, TL, width), lambda b, j: (b, j, 0)),
        out_shape=jax.ShapeDtypeStruct((B, S, width), BF16),
        scratch_shapes=[pltpu.VMEM((n_pairs, L, LANES), F32), pltpu.VMEM((8, n_rw), F32)]
                       + [feat() for _ in range(7)] + [pltpu.VMEM((TL // L, width), F32)]
                       + [feat() for _ in range(3)],
        compiler_params=pltpu.CompilerParams(dimension_semantics=("parallel", "arbitrary"),
                                             vmem_limit_bytes=VMEM_LIMIT_BYTES),
        name="rwkv7_chunked",
    )(rw, vec(mu), vec(w0), w2p, vec(a0), a2p, g2.astype(BF16), vec(k_k), vec(k_a), vec(r_k),
      vec(lnx_g), vec(lnx_b), tri, hsum)


def _post_kernel(x_ref, lng_ref, lnb_ref, ya_ref, yb_ref, ga_ref, gb_ref, wua_ref, wub_ref, wo_ref,
                 l1g_ref, l1b_ref, wg_ref, wu_ref, wd_ref, l2g_ref, l2b_ref, o_ref,
                 z_even, z_odd, *, ffn_cuts):
    i = pl.program_id(0)
    dot = functools.partial(jnp.dot, preferred_element_type=F32)

    @pl.when(i == 0)
    def _():
        z_even[...] = jnp.zeros_like(z_even)

    def ffn_part(h1b, lo, hi):
        gate = dot(h1b, wg_ref[:, lo:hi])
        up = dot(h1b, wu_ref[:, lo:hi])
        act = (gate * jax.nn.sigmoid(gate) * up).astype(BF16)
        return dot(act, wd_ref[lo:hi, :])

    def step(z_prev, z_next):
        up_a = dot(ya_ref[...], wua_ref[...])
        up_b = dot(yb_ref[...], wub_ref[...])
        h1 = _layer_norm(z_prev[...], l1g_ref[...], l1b_ref[...])
        h1b = h1.astype(BF16)
        ffn = ffn_part(h1b, ffn_cuts[0], ffn_cuts[1])
        merged = ga_ref[...].astype(F32) * up_a + gb_ref[...].astype(F32) * up_b
        for lo, hi in zip(ffn_cuts[1:-1], ffn_cuts[2:]):
            ffn = ffn + ffn_part(h1b, lo, hi)
        mix = dot(merged.astype(BF16), wo_ref[...])
        o_ref[...] = _layer_norm(DEEPNORM_ALPHA * h1 + ffn, l2g_ref[...], l2b_ref[...])
        h = _layer_norm(x_ref[...], lng_ref[...], lnb_ref[...])
        z_next[...] = DEEPNORM_ALPHA * h + mix

    @pl.when(i % 2 == 0)
    def _():
        step(z_even, z_odd)

    @pl.when(i % 2 == 1)
    def _():
        step(z_odd, z_even)


def _post(x2, ln_g, ln_b, ya, yb, ga, gb, w_up_a, w_up_b, w_out, ln1_g, ln1_b,
          w_gate, w_up, w_down, ln2_g, ln2_b):
    T, D = x2.shape
    hidden = w_gate.shape[1]
    assert hidden % MXU_DEPTH == 0
    ffn_cuts = (0, (hidden // MXU_DEPTH // 2) * MXU_DEPTH, hidden)
    tm = ROW_TILE
    n_tiles = T // tm
    row = lambda n: pl.BlockSpec((tm, n), lambda i: (jnp.minimum(i, n_tiles - 1), 0))
    vec = lambda a: a.reshape(1, D)
    cs = lambda a: _const_spec(a.shape)
    wts = [w.astype(BF16) for w in (w_up_a, w_up_b, w_out, w_gate, w_up, w_down)]
    wua, wub, wo, wg, wu, wd = wts
    return pl.pallas_call(
        functools.partial(_post_kernel, ffn_cuts=ffn_cuts),
        grid=(n_tiles + 1,),
        in_specs=[row(D), _const_spec((1, D)), _const_spec((1, D)), row(ya.shape[1]), row(yb.shape[1]),
                  row(D), row(D), cs(wua), cs(wub), cs(wo), _const_spec((1, D)), _const_spec((1, D)),
                  cs(wg), cs(wu), cs(wd), _const_spec((1, D)), _const_spec((1, D))],
        out_specs=pl.BlockSpec((tm, D), lambda i: (jnp.maximum(i - 1, 0), 0)),
        out_shape=jax.ShapeDtypeStruct((T, D), F32),
        scratch_shapes=[pltpu.VMEM((tm, D), F32), pltpu.VMEM((tm, D), F32)],
        compiler_params=pltpu.CompilerParams(dimension_semantics=("arbitrary",),
                                             vmem_limit_bytes=VMEM_LIMIT_BYTES),
        name="merge_ffn",
    )(x2, vec(ln_g), vec(ln_b), ya, yb, ga, gb, wua, wub, wo, vec(ln1_g), vec(ln1_b),
      wg, wu, wd, vec(ln2_g), vec(ln2_b))


def kernel(x, ln_in_g, ln_in_b, rel_bias, w_in, diff_lam_q1, diff_lam_k1, diff_lam_q2, diff_lam_k2, diff_subln_g, rwkv_mu, rwkv_w0, rwkv_w2, rwkv_a0, rwkv_a2, rwkv_g2, rwkv_k_k, rwkv_k_a, rwkv_r_k, rwkv_lnx_g, rwkv_lnx_b, w_up_a, w_up_b, w_out, ln1_g, ln1_b, ffn_w_gate, ffn_w_up, ffn_w_down, ln2_g, ln2_b):
    B, S, D = x.shape
    T = B * S
    width = rwkv_w0.shape[-1]
    n_qk = DIFF_HEADS * HEAD_PAIR
    n_rw = 3 * width + DECAY_RANK + ICLR_RANK + GATE_RANK
    sizes = (n_qk, n_qk, n_qk, n_rw, D, D)
    assert w_in.shape[0] == 1 and w_in.shape[2] == sum(sizes), "single-layer model expected"
    x2 = x.reshape(T, D)
    q, k, v, rw, ga, gb = _inproj(x2, ln_in_g, ln_in_b, w_in[0].astype(BF16), sizes)
    lam_vecs = jnp.concatenate([diff_lam_q1, diff_lam_k1, diff_lam_q2, diff_lam_k2], 0).astype(F32)
    ya = _attention(q.reshape(B, S, n_qk), k.reshape(B, S, n_qk), v.reshape(B, S, n_qk),
                    rel_bias, lam_vecs, diff_subln_g[0])
    yb = _rwkv(rw.reshape(B, S, n_rw), rwkv_mu[0], rwkv_w0[0], rwkv_w2[0], rwkv_a0[0], rwkv_a2[0],
               rwkv_g2[0], rwkv_k_k[0], rwkv_k_a[0], rwkv_r_k[0], rwkv_lnx_g[0], rwkv_lnx_b[0])
    out = _post(x2, ln_in_g, ln_in_b, ya.reshape(T, n_qk), yb.reshape(T, width), ga, gb,
                w_up_a[0], w_up_b[0], w_out[0], ln1_g[0], ln1_b[0],
                ffn_w_gate[0], ffn_w_up[0], ffn_w_down[0], ln2_g[0], ln2_b[0])
    return out.reshape(B, S, D)
```

```python
import functools
import math

import jax
import jax.numpy as jnp
from jax import lax
from jax.experimental import pallas as pl
from jax.experimental.pallas import tpu as pltpu

F32 = jnp.float32
BF16 = jnp.bfloat16

DIFF_HEADS = 4
DIFF_HEAD_DIM = 64
HEAD_PAIR = 2 * DIFF_HEAD_DIM
RWKV_HEAD_DIM = 64
DECAY_RANK = 64
ICLR_RANK = 64
GATE_RANK = 128
NUM_BUCKETS = 32
MAX_DISTANCE = 128
LN_EPS = 1e-5
LNX_EPS = 64e-5
NEG_BIG = -1e30
DEPTH = 1
DEEPNORM_ALPHA = (2.0 * DEPTH) ** 0.25
LAMBDA_INIT = 0.8 - 0.6 * math.exp(-0.3 * 0)
LOG2_E = math.log2(math.e)

LANES = 128
MXU_DEPTH = 256
VMEM_LIMIT_BYTES = 56 * 1024 * 1024
ROW_TILE = 256
ATTN_TILE = 256
SCAN_CHUNK = 64
SCAN_TASK = 256
SCAN_TILE = 512


def _mm(a, b):
    return jnp.dot(a.astype(BF16), b.astype(BF16), preferred_element_type=F32)


def _mm_nt(a, b):
    return lax.dot_general(a.astype(BF16), b.astype(BF16), (((1,), (1,)), ((), ())),
                           preferred_element_type=F32)


def _mm_tn(a, b):
    return lax.dot_general(a.astype(BF16), b.astype(BF16), (((0,), (0,)), ((), ())),
                           preferred_element_type=F32)


def _split2(x):
    hi = x.astype(BF16)
    lo = (x - hi.astype(F32)).astype(BF16)
    return hi, lo


def _sel_mm(sel, x):
    hi, lo = _split2(x)
    d = functools.partial(jnp.dot, preferred_element_type=F32)
    return d(sel, hi) + d(sel, lo)


def _head_sum(x, sel):
    xb = x.astype(BF16)
    return jnp.concatenate(
        [jnp.dot(xb[:, c0:c0 + LANES], sel, preferred_element_type=F32)
         for c0 in range(0, x.shape[1], LANES)], axis=1)


def _layer_norm(x, g, b, eps=LN_EPS):
    mu = jnp.mean(x, -1, keepdims=True)
    xc = x - mu
    var = jnp.mean(xc * xc, -1, keepdims=True)
    return xc * lax.rsqrt(var + eps) * g + b


def _const_spec(shape):
    return pl.BlockSpec(shape, lambda *_: (0,) * len(shape), pipeline_mode=pl.Buffered(1))


def _inproj_kernel(x_ref, g_ref, b_ref, w_ref, q_ref, k_ref, v_ref, rw_ref, ga_ref, gb_ref,
                   h_even, h_odd, *, splits):
    i = pl.program_id(0)

    @pl.when(i == 0)
    def _():
        h_even[...] = jnp.zeros_like(h_even)

    def step(h_prev, h_next):
        h_next[...] = _layer_norm(x_ref[...], g_ref[...], b_ref[...]).astype(BF16)

        def proj(lo, hi):
            return jnp.dot(h_prev[...], w_ref[:, lo:hi], preferred_element_type=F32)

        s_q, s_k, s_v, s_rw, s_ga = splits
        ga_ref[...] = jax.nn.sigmoid(proj(s_rw, s_ga)).astype(BF16)
        gb_ref[...] = jax.nn.sigmoid(proj(s_ga, w_ref.shape[1])).astype(BF16)
        q_ref[...] = (proj(0, s_q) * (DIFF_HEAD_DIM ** -0.5 * LOG2_E)).astype(BF16)
        k_ref[...] = proj(s_q, s_k).astype(BF16)
        v_ref[...] = proj(s_k, s_v).astype(BF16)
        rw_ref[...] = proj(s_v, s_rw)

    @pl.when(i % 2 == 0)
    def _():
        step(h_even, h_odd)

    @pl.when(i % 2 == 1)
    def _():
        step(h_odd, h_even)


def _inproj(x2, ln_g, ln_b, w_in_bf, sizes):
    T, D = x2.shape
    n_q, n_k, n_v, n_rw, n_ga, n_gb = sizes
    acc, cuts = 0, []
    for s in sizes[:-1]:
        acc += s
        cuts.append(acc)
    splits = tuple(cuts)
    tm = ROW_TILE
    n_tiles = T // tm
    row_in = pl.BlockSpec((tm, D), lambda i: (jnp.minimum(i, n_tiles - 1), 0))
    row = lambda n: pl.BlockSpec((tm, n), lambda i: (jnp.maximum(i - 1, 0), 0))
    return pl.pallas_call(
        functools.partial(_inproj_kernel, splits=splits),
        grid=(n_tiles + 1,),
        in_specs=[row_in, _const_spec((1, D)), _const_spec((1, D)), _const_spec(w_in_bf.shape)],
        out_specs=[row(n_q), row(n_k), row(n_v), row(n_rw), row(n_ga), row(n_gb)],
        out_shape=[jax.ShapeDtypeStruct((T, n_q), BF16), jax.ShapeDtypeStruct((T, n_k), BF16),
                   jax.ShapeDtypeStruct((T, n_v), BF16), jax.ShapeDtypeStruct((T, n_rw), F32),
                   jax.ShapeDtypeStruct((T, n_ga), BF16), jax.ShapeDtypeStruct((T, n_gb), BF16)],
        scratch_shapes=[pltpu.VMEM((tm, D), BF16), pltpu.VMEM((tm, D), BF16)],
        compiler_params=pltpu.CompilerParams(dimension_semantics=("arbitrary",),
                                             vmem_limit_bytes=VMEM_LIMIT_BYTES),
        name="ln_inproj",
    )(x2, ln_g.reshape(1, D), ln_b.reshape(1, D), w_in_bf)


def _attn_stages(q_ref, k_ref, v_ref, bdiag_ref, bnear_ref, bfar_ref, lam_ref, g_ref, o_ref,
                 v1_s, *, nq, t):
    lane = lax.broadcasted_iota(jnp.int32, (t, HEAD_PAIR), 1)
    lo = lane < DIFF_HEAD_DIM
    lv = lam_ref[...]
    lam = (jnp.exp(jnp.sum(lv[0:1] * lv[1:2], keepdims=True))
           - jnp.exp(jnp.sum(lv[2:3] * lv[3:4], keepdims=True)) + LAMBDA_INIT)
    bfar = bfar_ref[...][:, 0:1]
    zero = jnp.zeros((), BF16)
    rowmax = lambda a: jnp.max(a, -1, keepdims=True)
    v1_s[:, :HEAD_PAIR] = v_ref[...]
    v1_s[:, HEAD_PAIR:] = jnp.ones((v_ref.shape[0], HEAD_PAIR), BF16)

    def logits(qi):
        q = q_ref[qi * t:(qi + 1) * t, :]
        qs = jnp.concatenate([jnp.where(lo, q, zero), jnp.where(lo, zero, q)], axis=0)
        return lax.dot_general(qs, k_ref[0:(qi + 1) * t, :], (((1,), (1,)), ((), ())),
                               preferred_element_type=F32)

    def softmax(qi, s):
        s_diag = s[:, qi * t:] + bdiag_ref[...]
        m = rowmax(s_diag)
        if qi >= 1:
            s_near = s[:, (qi - 1) * t:qi * t] + bnear_ref[...]
            m = jnp.maximum(m, rowmax(s_near))
        if qi >= 2:
            s_far = s[:, :(qi - 1) * t]
            m = jnp.maximum(m, rowmax(s_far) + bfar)
        parts = []
        if qi >= 2:
            parts.append(jnp.exp2(s_far - (m - bfar)))
        if qi >= 1:
            parts.append(jnp.exp2(s_near - m))
        parts.append(jnp.exp2(s_diag - m))
        return jnp.concatenate([x.astype(BF16) for x in parts], axis=1)

    def values(qi, p):
        acc = jnp.dot(p, v1_s[0:(qi + 1) * t, :], preferred_element_type=F32)
        l = acc[:, HEAD_PAIR:HEAD_PAIR + 1]
        acc = acc[:, :HEAD_PAIR]
        o = acc[:t] * (1.0 / l[:t]) - acc[t:] * (lam / l[t:])
        ms = jnp.mean(o * o, -1, keepdims=True)
        o = o * lax.rsqrt(ms + LN_EPS) * g_ref[...] * (1.0 - LAMBDA_INIT)
        o_ref[qi * t:(qi + 1) * t, :] = o.astype(o_ref.dtype)

    s_q, p_q = {}, {}

    def do_logits(qi):
        s_q[qi] = logits(qi)

    def do_softmax(qi):
        p_q[qi] = softmax(qi, s_q.pop(qi))

    def do_values(qi):
        values(qi, p_q.pop(qi))

    stages = []
    for step in range(nq + 2):
        if step < nq:
            stages.append(functools.partial(do_logits, step))
        if 0 <= step - 1 < nq:
            stages.append(functools.partial(do_softmax, step - 1))
        if 0 <= step - 2 < nq:
            stages.append(functools.partial(do_values, step - 2))
    return stages


def _t5_bucket(dist):
    max_exact = NUM_BUCKETS // 2
    d = jnp.maximum(dist, 1).astype(F32)
    large = max_exact + (jnp.log(d / max_exact) / math.log(MAX_DISTANCE / max_exact)
                         * (NUM_BUCKETS - max_exact)).astype(jnp.int32)
    large = jnp.minimum(large, NUM_BUCKETS - 1)
    return jnp.where(dist < max_exact, dist, large)


def _attention(q, k, v, rel_bias, lam_vecs, subln_g):
    B, S, _ = q.shape
    t = ATTN_TILE
    assert S % t == 0 and t >= MAX_DISTANCE
    nq = S // t
    dd = jnp.arange(t, dtype=jnp.int32)[:, None] - jnp.arange(t, dtype=jnp.int32)[None, :]
    buckets = jnp.arange(NUM_BUCKETS, dtype=jnp.int32)

    def bias_tile(dist):
        onehot = (_t5_bucket(dist)[..., None] == buckets).astype(F32)
        return jnp.einsum("ijb,bh->hij", onehot, rel_bias.astype(F32),
                          precision=lax.Precision.HIGHEST)

    bdiag = jnp.where(dd >= 0, bias_tile(jnp.maximum(dd, 0)) * LOG2_E, NEG_BIG)
    bnear = bias_tile(dd + t) * LOG2_E
    bdiag = jnp.concatenate([bdiag, bdiag], axis=1)
    bnear = jnp.concatenate([bnear, bnear], axis=1)
    bfar = jnp.broadcast_to((rel_bias[NUM_BUCKETS - 1] * LOG2_E)[:, None, None],
                            (DIFF_HEADS, 1, LANES))
    seq = pl.BlockSpec((None, S, HEAD_PAIR), lambda b, h: (b, 0, h))
    per_head = lambda r, c: pl.BlockSpec((None, r, c), lambda b, h: (h, 0, 0))
    return dict(
        kwargs=dict(nq=nq, t=t),
        operands=(q, k, v, bdiag, bnear, bfar, lam_vecs, subln_g.reshape(1, HEAD_PAIR)),
        in_specs=[seq, seq, seq, per_head(2 * t, t), per_head(2 * t, t), per_head(1, LANES),
                  pl.BlockSpec((4, DIFF_HEAD_DIM), lambda b, h: (0, 0)),
                  pl.BlockSpec((1, HEAD_PAIR), lambda b, h: (0, 0))],
        out_spec=seq,
        out_shape=jax.ShapeDtypeStruct((B, S, DIFF_HEADS * HEAD_PAIR), BF16),
        scratch_shapes=[pltpu.VMEM((S, 2 * HEAD_PAIR), BF16)],
        grid=(B, DIFF_HEADS))


def _rwkv_stages(rw_ref, mu_ref, w0_ref, w2_ref, a0_ref, a2_ref, g2_ref, kk_ref, ka_ref, rk_ref,
                 lg_ref, lb_ref, tri_ref, hsum_ref, o_ref,
                 h_s, prev_s, ah_s, rh_s, bt_s, kt_s, bb_s, kb_s, v_s, gl_s, y_s, bonus_s, gate_s,
                 *, width, n_pairs):
    L = SCAN_CHUNK
    TL = rw_ref.shape[0]
    j = pl.program_id(1)

    @pl.when(j == 0)
    def _():
        h_s[...] = jnp.zeros_like(h_s)
        prev_s[...] = jnp.zeros_like(prev_s)

    HT = SCAN_TASK
    n_tasks = TL // HT
    n_rounds = int(math.log2(L))
    n_chunks = HT // L
    n_blocks = n_chunks * n_pairs

    lane = lax.broadcasted_iota(jnp.int32, (L, LANES), 1)
    m0 = lane < RWKV_HEAD_DIM
    m1 = jnp.logical_not(m0)
    ri = lax.broadcasted_iota(jnp.int32, (2 * L, LANES), 0)
    ci = lax.broadcasted_iota(jnp.int32, (2 * L, LANES), 1)
    eye = ri == ci
    rl = lax.broadcasted_iota(jnp.int32, (L, LANES), 0)
    tril2 = (lane % L) <= rl
    stril2 = (lane % L) < rl
    eye2_f = jnp.where((lane % L) == rl, 1.0, 0.0).astype(F32)
    zeros = jnp.zeros((L, LANES), F32)

    sel = lambda m, x: jnp.where(m, x, zeros)
    cat0 = lambda *xs: jnp.concatenate(xs, 0)
    cat1 = lambda *xs: jnp.concatenate(xs, 1)
    block_diag = lambda x: cat0(sel(m0, x), sel(m1, x))
    swap_halves = lambda x: pltpu.roll(x, RWKV_HEAD_DIM, 1)
    row_in_half = lax.broadcasted_iota(jnp.int32, (HT, 1), 0)
    st = [dict() for _ in range(n_tasks)]
    hs = [h_s[pi] for pi in range(n_pairs)]

    def rows_of(t):
        return slice(t * HT, (t + 1) * HT)

    def block_ld(ref, t, i):
        c, pi = divmod(i, n_pairs)
        r0 = t * HT + c * L
        return ref[r0:r0 + L, pi * LANES:(pi + 1) * LANES]

    def prep_a(t):
        d, rows = st[t], rows_of(t)
        p = rw_ref[rows, :]
        before = prev_s[0:1, :] if t == 0 else rw_ref[t * HT - 1:t * HT, :]
        p_prev = jnp.where(row_in_half == 0, before, pltpu.roll(p, 1, 0))
        ps = p + (p_prev - p) * mu_ref[...]
        d["r"], d["k"], d["v"] = ps[:, 0:width], ps[:, width:2 * width], ps[:, 2 * width:3 * width]
        lr = ps[:, 3 * width:3 * width + DECAY_RANK + ICLR_RANK]
        gl = ps[:, 3 * width + DECAY_RANK + ICLR_RANK:]
        d["z"] = w0_ref[...] + _mm(jnp.tanh(lr), w2_ref[...])
        d["ai"] = a0_ref[...] + _mm(lr, a2_ref[...])
        gate_s[rows, :] = _mm(jax.nn.sigmoid(gl), g2_ref[...])
        d["kk"] = d["k"] * kk_ref[...]
        d["ss"] = _head_sum(d["kk"] * d["kk"], hsum_ref[...])

    def prep_b(t):
        d, rows = st[t], rows_of(t)
        d["logw"] = (-math.exp(-0.5) * LOG2_E) * jax.nn.sigmoid(d.pop("z"))
        iclr = jax.nn.sigmoid(d.pop("ai"))
        kk = d["kk"] * lax.rsqrt(jnp.maximum(d.pop("ss"), 1e-24))
        d["kk"] = kk
        d["b"] = kk * iclr
        d["k2"] = d.pop("k") * (1.0 + (iclr - 1.0) * ka_ref[...])
        bonus_s[rows, :] = _head_sum(d["r"] * d["k2"] * rk_ref[...], hsum_ref[...]) * d["v"]
        d["c"] = _sel_mm(tri_ref[...], d["logw"])

    def prep_c(t):
        d, rows = st[t], rows_of(t)
        c = d.pop("c")
        c_last = [c[i * L + L - 1:i * L + L, :] for i in range(n_chunks)]
        c_end = jnp.concatenate([jnp.broadcast_to(x, (L, width)) for x in c_last], 0)
        inv_g = jnp.exp2(-c)
        to_end = jnp.exp2(c_end - c)
        b, k2 = d.pop("b"), d.pop("k2")
        ah_s[rows, :] = -d.pop("kk") * jnp.exp2(c - d.pop("logw"))
        rh_s[rows, :] = d.pop("r") * jnp.exp2(c)
        bt_s[rows, :] = b * inv_g
        kt_s[rows, :] = k2 * inv_g
        bb_s[rows, :] = b * to_end
        kb_s[rows, :] = k2 * to_end
        v_s[rows, :] = d.pop("v")
        for i, x in enumerate(c_last):
            gl_s[t * n_chunks + i:t * n_chunks + i + 1, :] = jnp.exp2(x)

    def gram(t):
        d = st[t]
        ah = [block_ld(ah_s, t, i) for i in range(n_blocks)]
        rh = [block_ld(rh_s, t, i) for i in range(n_blocks)]
        bt = [block_ld(bt_s, t, i) for i in range(n_blocks)]
        kt = [block_ld(kt_s, t, i) for i in range(n_blocks)]
        d["ah0"] = [sel(m0, x) for x in ah]
        d["ah1"] = [sel(m1, x) for x in ah]
        d["rh"] = rh
        g = [_mm_nt(cat0(d["ah0"][i], sel(m0, rh[i]), d["ah1"][i], sel(m1, rh[i])), cat0(bt[i], kt[i]))
             for i in range(n_blocks)]
        pair = lambda x0, x1: jnp.where(m0, x0, swap_halves(x1))
        pair_r = lambda x0, x1: jnp.where(m0, swap_halves(x0), x1)
        d["ak"] = [sel(stril2, pair_r(x[:L], x[2 * L:3 * L])) for x in g]
        d["rb"] = [sel(tril2, pair(x[L:2 * L], x[3 * L:])) for x in g]
        d["rk"] = [sel(tril2, pair_r(x[L:2 * L], x[3 * L:])) for x in g]
        d["pw"] = [sel(stril2, pair(x[:L], x[2 * L:3 * L])) for x in g]
        d["tinv"] = [eye2_f + x for x in d["pw"]]
        d["round"] = 0

    def neumann(t):
        d = st[t]
        first, last = d["round"] == 0, d["round"] == n_rounds - 1
        d["round"] += 1
        if first:
            d["pw"] = [_mm(x, block_diag(x)) for x in d["pw"]]
            return
        out = [_mm(tm if last else cat0(tm, x), block_diag(x)) for x, tm in zip(d["pw"], d["tinv"])]
        d["tinv"] = [tm + o[:L] for tm, o in zip(d["tinv"], out)]
        if not last:
            d["pw"] = [o[L:] for o in out]

    def solve(t):
        d = st[t]
        d.pop("pw")
        vv = [block_ld(v_s, t, i) for i in range(n_blocks)]
        d["vv"] = vv
        kv = [_mm(cat0(ak, rk), block_diag(x)) for ak, rk, x in zip(d.pop("ak"), d.pop("rk"), vv)]
        d["rkv"] = [x[L:] for x in kv]
        w = [_mm(tm, cat1(cat0(a0_, a1_), block_diag(x[:L])))
             for tm, a0_, a1_, x in zip(d.pop("tinv"), d.pop("ah0"), d.pop("ah1"), kv)]
        d["w1"] = [x[:, :LANES] for x in w]
        d["w2"] = [x[:, LANES:] for x in w]

    def fold(t):
        d = st[t]
        lhs, cm, y0, gcol = [], [], [], []
        for i in range(n_blocks):
            w1, w2, vv = d["w1"][i], d["w2"][i], d["vv"][i]
            ac = _mm_tn(cat0(block_ld(bb_s, t, i), block_ld(kb_s, t, i)),
                        cat1(cat0(w1, zeros), cat0(w2, vv)))
            ry0 = _mm(d["rb"][i], cat1(block_diag(w1), block_diag(w2)))
            a_mat = jnp.where(m0, ac[:L, :LANES], ac[L:, :LANES])
            cm.append(jnp.where(m0, ac[:L, LANES:], ac[L:, LANES:]))
            lhs.append(cat0(a_mat, d["rh"][i] + ry0[:, :LANES]).astype(BF16))
            y0.append(ry0[:, LANES:] + d["rkv"][i])
            ch, pi = divmod(i, n_pairs)
            g_end = gl_s[t * n_chunks + ch:t * n_chunks + ch + 1, pi * LANES:(pi + 1) * LANES]
            g_rows = jnp.sum(jnp.where(eye, jnp.broadcast_to(g_end, (2 * L, LANES)), 0.0),
                             axis=1, keepdims=True)
            gcol.append(jnp.where(m0, g_rows[:L], g_rows[L:]))
        for key in ("w1", "w2", "vv", "rb", "rkv", "rh"):
            d.pop(key)
        d.update(lhs=lhs, cm=cm, y0=y0, gcol=gcol)

    def scan(t, c):
        d = st[t]
        r0 = t * HT + c * L
        for pi in range(n_pairs):
            i = c * n_pairs + pi
            h = hs[pi]
            out = _mm(d["lhs"][i], block_diag(h))
            hs[pi] = d["gcol"][i] * h + out[:L] + d["cm"][i]
            y_s[r0:r0 + L, pi * LANES:(pi + 1) * LANES] = out[L:] + d["y0"][i]

    def finish(t):
        rows = rows_of(t)
        y = y_s[rows, :]
        inv_n = 1.0 / RWKV_HEAD_DIM
        mean = _head_sum(y, hsum_ref[...]) * inv_n
        yc = y - mean
        var = _head_sum(yc * yc, hsum_ref[...]) * inv_n
        yn = yc * lax.rsqrt(var + LNX_EPS) * lg_ref[...] + lb_ref[...]
        o_ref[rows, :] = ((yn + bonus_s[rows, :]) * gate_s[rows, :]).astype(o_ref.dtype)

    assert n_rounds >= max(n_chunks, 3)

    def carry_out():
        prev_s[0:1, :] = rw_ref[TL - 1:TL, :]
        for pi in range(n_pairs):
            h_s[pi] = hs[pi]

    stages = []
    add = lambda f, *a: stages.append(functools.partial(f, *a))
    add(prep_a, 0); add(prep_b, 0); add(prep_c, 0)
    for t in range(n_tasks):
        nxt = t + 1 if t + 1 < n_tasks else None
        prv = t - 1 if t >= 1 else None
        feature_stages = iter((prep_b, prep_c))
        if nxt is not None:
            add(prep_a, nxt)
        add(gram, t)
        for rnd in range(n_rounds):
            add(neumann, t)
            if prv is not None and rnd < n_chunks:
                add(scan, prv, rnd)
            if nxt is not None and rnd in (0, 2):
                add(next(feature_stages), nxt)
        if prv is not None:
            add(finish, prv)
        add(solve, t); add(fold, t)
    for c in range(n_chunks):
        add(scan, n_tasks - 1, c)
    add(finish, n_tasks - 1)
    add(carry_out)
    return stages


def _rwkv(rw, mu, w0, w2, a0, a2, g2, k_k, k_a, r_k, lnx_g, lnx_b):
    B, S, n_rw = rw.shape
    width = w0.shape[-1]
    n_pairs = width // LANES
    L, TL = SCAN_CHUNK, min(SCAN_TILE, S)
    assert S % TL == 0 and TL % SCAN_TASK == 0
    assert n_rw == 3 * width + DECAY_RANK + ICLR_RANK + GATE_RANK
    assert DECAY_RANK + ICLR_RANK == LANES
    ti = jnp.arange(SCAN_TASK)
    same_chunk = (ti[:, None] // L) == (ti[None, :] // L)
    tri = (same_chunk & (ti[None, :] <= ti[:, None])).astype(BF16)
    ch = jnp.arange(LANES) // RWKV_HEAD_DIM
    hsum = (ch[:, None] == ch[None, :]).astype(BF16)
    w2p = jnp.concatenate([w2, jnp.zeros((ICLR_RANK, width), w2.dtype)], 0).astype(BF16)
    a2p = jnp.concatenate([jnp.zeros((DECAY_RANK, width), a2.dtype), a2], 0).astype(BF16)
    vec = lambda a: a.reshape(1, -1).astype(F32)
    tile = pl.BlockSpec((None, TL, n_rw), lambda b, j: (b, j, 0))
    feat = lambda: pltpu.VMEM((TL, width), F32)
    return dict(
        kwargs=dict(width=width, n_pairs=n_pairs),
        operands=(rw, vec(mu), vec(w0), w2p, vec(a0), a2p, g2.astype(BF16), vec(k_k), vec(k_a),
                  vec(r_k), vec(lnx_g), vec(lnx_b), tri, hsum),
        in_specs=[tile, _const_spec((1, n_rw)), _const_spec((1, width)), _const_spec((LANES, width)),
                  _const_spec((1, width)), _const_spec((LANES, width)), _const_spec((GATE_RANK, width)),
                  _const_spec((1, width)), _const_spec((1, width)), _const_spec((1, width)),
                  _const_spec((1, width)), _const_spec((1, width)),
                  _const_spec((SCAN_TASK, SCAN_TASK)), _const_spec((LANES, LANES))],
        out_spec=pl.BlockSpec((None, TL, width), lambda b, j: (b, j, 0)),
        out_shape=jax.ShapeDtypeStruct((B, S, width), BF16),
        scratch_shapes=[pltpu.VMEM((n_pairs, L, LANES), F32), pltpu.VMEM((8, n_rw), F32)]
                       + [feat() for _ in range(7)] + [pltpu.VMEM((TL // L, width), F32)]
                       + [feat() for _ in range(3)],
        grid=(B, S // TL))


def _run_stages(stages_fn, *refs, **kwargs):
    for stage in stages_fn(*refs, **kwargs):
        stage()


def _mixer_call(call, stages_fn, semantics, name):
    return pl.pallas_call(
        functools.partial(_run_stages, stages_fn, **call["kwargs"]),
        grid=call["grid"],
        in_specs=call["in_specs"],
        out_specs=call["out_spec"],
        out_shape=call["out_shape"],
        scratch_shapes=call.get("scratch_shapes", ()),
        compiler_params=pltpu.CompilerParams(dimension_semantics=semantics,
                                             vmem_limit_bytes=VMEM_LIMIT_BYTES),
        name=name,
    )(*call["operands"])


def _post_kernel(x_ref, lng_ref, lnb_ref, ya_ref, yb_ref, ga_ref, gb_ref, wua_ref, wub_ref, wo_ref,
                 l1g_ref, l1b_ref, wg_ref, wu_ref, wd_ref, l2g_ref, l2b_ref, o_ref,
                 z_even, z_odd, *, ffn_cuts):
    i = pl.program_id(0)
    dot = functools.partial(jnp.dot, preferred_element_type=F32)

    @pl.when(i == 0)
    def _():
        z_even[...] = jnp.zeros_like(z_even)

    def ffn_part(h1b, lo, hi):
        gate = dot(h1b, wg_ref[:, lo:hi])
        up = dot(h1b, wu_ref[:, lo:hi])
        act = (gate * jax.nn.sigmoid(gate) * up).astype(BF16)
        return dot(act, wd_ref[lo:hi, :])

    def step(z_prev, z_next):
        up_a = dot(ya_ref[...], wua_ref[...])
        up_b = dot(yb_ref[...], wub_ref[...])
        h1 = _layer_norm(z_prev[...], l1g_ref[...], l1b_ref[...])
        h1b = h1.astype(BF16)
        ffn = ffn_part(h1b, ffn_cuts[0], ffn_cuts[1])
        merged = ga_ref[...].astype(F32) * up_a + gb_ref[...].astype(F32) * up_b
        for lo, hi in zip(ffn_cuts[1:-1], ffn_cuts[2:]):
            ffn = ffn + ffn_part(h1b, lo, hi)
        mix = dot(merged.astype(BF16), wo_ref[...])
        o_ref[...] = _layer_norm(DEEPNORM_ALPHA * h1 + ffn, l2g_ref[...], l2b_ref[...])
        h = _layer_norm(x_ref[...], lng_ref[...], lnb_ref[...])
        z_next[...] = DEEPNORM_ALPHA * h + mix

    @pl.when(i % 2 == 0)
    def _():
        step(z_even, z_odd)

    @pl.when(i % 2 == 1)
    def _():
        step(z_odd, z_even)


def _post(x2, ln_g, ln_b, ya, yb, ga, gb, w_up_a, w_up_b, w_out, ln1_g, ln1_b,
          w_gate, w_up, w_down, ln2_g, ln2_b):
    T, D = x2.shape
    hidden = w_gate.shape[1]
    assert hidden % MXU_DEPTH == 0
    ffn_cuts = (0, (hidden // MXU_DEPTH // 2) * MXU_DEPTH, hidden)
    tm = ROW_TILE
    n_tiles = T // tm
    row = lambda n: pl.BlockSpec((tm, n), lambda i: (jnp.minimum(i, n_tiles - 1), 0))
    vec = lambda a: a.reshape(1, D)
    cs = lambda a: _const_spec(a.shape)
    wts = [w.astype(BF16) for w in (w_up_a, w_up_b, w_out, w_gate, w_up, w_down)]
    wua, wub, wo, wg, wu, wd = wts
    return pl.pallas_call(
        functools.partial(_post_kernel, ffn_cuts=ffn_cuts),
        grid=(n_tiles + 1,),
        in_specs=[row(D), _const_spec((1, D)), _const_spec((1, D)), row(ya.shape[1]), row(yb.shape[1]),
                  row(D), row(D), cs(wua), cs(wub), cs(wo), _const_spec((1, D)), _const_spec((1, D)),
                  cs(wg), cs(wu), cs(wd), _const_spec((1, D)), _const_spec((1, D))],
        out_specs=pl.BlockSpec((tm, D), lambda i: (jnp.maximum(i - 1, 0), 0)),
        out_shape=jax.ShapeDtypeStruct((T, D), F32),
        scratch_shapes=[pltpu.VMEM((tm, D), F32), pltpu.VMEM((tm, D), F32)],
        compiler_params=pltpu.CompilerParams(dimension_semantics=("arbitrary",),
                                             vmem_limit_bytes=VMEM_LIMIT_BYTES),
        name="merge_ffn",
    )(x2, vec(ln_g), vec(ln_b), ya, yb, ga, gb, wua, wub, wo, vec(ln1_g), vec(ln1_b),
      wg, wu, wd, vec(ln2_g), vec(ln2_b))


def kernel(x, ln_in_g, ln_in_b, rel_bias, w_in, diff_lam_q1, diff_lam_k1, diff_lam_q2, diff_lam_k2, diff_subln_g, rwkv_mu, rwkv_w0, rwkv_w2, rwkv_a0, rwkv_a2, rwkv_g2, rwkv_k_k, rwkv_k_a, rwkv_r_k, rwkv_lnx_g, rwkv_lnx_b, w_up_a, w_up_b, w_out, ln1_g, ln1_b, ffn_w_gate, ffn_w_up, ffn_w_down, ln2_g, ln2_b):
    B, S, D = x.shape
    T = B * S
    width = rwkv_w0.shape[-1]
    n_qk = DIFF_HEADS * HEAD_PAIR
    n_rw = 3 * width + DECAY_RANK + ICLR_RANK + GATE_RANK
    sizes = (n_qk, n_qk, n_qk, n_rw, D, D)
    assert w_in.shape[0] == 1 and w_in.shape[2] == sum(sizes), "single-layer model expected"
    x2 = x.reshape(T, D)
    q, k, v, rw, ga, gb = _inproj(x2, ln_in_g, ln_in_b, w_in[0].astype(BF16), sizes)
    lam_vecs = jnp.concatenate([diff_lam_q1, diff_lam_k1, diff_lam_q2, diff_lam_k2], 0).astype(F32)
    ya = _mixer_call(
        _attention(q.reshape(B, S, n_qk), k.reshape(B, S, n_qk), v.reshape(B, S, n_qk),
                   rel_bias, lam_vecs, diff_subln_g[0]),
        _attn_stages, ("parallel", "parallel"), "diff_attention")
    yb = _mixer_call(
        _rwkv(rw.reshape(B, S, n_rw), rwkv_mu[0], rwkv_w0[0], rwkv_w2[0], rwkv_a0[0], rwkv_a2[0],
              rwkv_g2[0], rwkv_k_k[0], rwkv_k_a[0], rwkv_r_k[0], rwkv_lnx_g[0], rwkv_lnx_b[0]),
        _rwkv_stages, ("parallel", "arbitrary"), "rwkv7_chunked")
    out = _post(x2, ln_in_g, ln_in_b, ya.reshape(T, n_qk), yb.reshape(T, width), ga, gb,
                w_up_a[0], w_up_b[0], w_out[0], ln1_g[0], ln1_b[0],
                ffn_w_gate[0], ffn_w_up[0], ffn_w_down[0], ln2_g[0], ln2_b[0])
    return out.reshape(B, S, D)
```

```python
import functools
import math

import jax
import jax.numpy as jnp
from jax import lax
from jax.experimental import pallas as pl
from jax.experimental.pallas import tpu as pltpu

F32 = jnp.float32
BF16 = jnp.bfloat16

DIFF_HEADS = 4
DIFF_HEAD_DIM = 64
HEAD_PAIR = 2 * DIFF_HEAD_DIM
RWKV_HEAD_DIM = 64
DECAY_RANK = 64
ICLR_RANK = 64
GATE_RANK = 128
NUM_BUCKETS = 32
MAX_DISTANCE = 128
LN_EPS = 1e-5
LNX_EPS = 64e-5
NEG_BIG = -1e30
DEPTH = 1
DEEPNORM_ALPHA = (2.0 * DEPTH) ** 0.25
LAMBDA_INIT = 0.8 - 0.6 * math.exp(-0.3 * 0)
LOG2_E = math.log2(math.e)

LANES = 128
MXU_DEPTH = 256
VMEM_LIMIT_BYTES = 56 * 1024 * 1024
ROW_TILE = 256
ATTN_TILE = 256
SCAN_CHUNK = 64
SCAN_TASK = 256
SCAN_TILE = 512


def _mm(a, b):
    return jnp.dot(a.astype(BF16), b.astype(BF16), preferred_element_type=F32)


def _mm_nt(a, b):
    return lax.dot_general(a.astype(BF16), b.astype(BF16), (((1,), (1,)), ((), ())),
                           preferred_element_type=F32)


def _mm_tn(a, b):
    return lax.dot_general(a.astype(BF16), b.astype(BF16), (((0,), (0,)), ((), ())),
                           preferred_element_type=F32)


def _split2(x):
    hi = x.astype(BF16)
    lo = (x - hi.astype(F32)).astype(BF16)
    return hi, lo


def _sel_mm(sel, x):
    hi, lo = _split2(x)
    d = functools.partial(jnp.dot, preferred_element_type=F32)
    return d(sel, hi) + d(sel, lo)


def _head_sum(x, sel):
    xb = x.astype(BF16)
    return jnp.concatenate(
        [jnp.dot(xb[:, c0:c0 + LANES], sel, preferred_element_type=F32)
         for c0 in range(0, x.shape[1], LANES)], axis=1)


def _layer_norm(x, g, b, eps=LN_EPS):
    mu = jnp.mean(x, -1, keepdims=True)
    xc = x - mu
    var = jnp.mean(xc * xc, -1, keepdims=True)
    return xc * lax.rsqrt(var + eps) * g + b


def _const_spec(shape):
    return pl.BlockSpec(shape, lambda *_: (0,) * len(shape), pipeline_mode=pl.Buffered(1))


def _rwkv_feature_stages(p, before, mu, w0, w2, a0, a2, g2, k_k, k_a, r_k, tri, hsum, *, width):
    L = SCAN_CHUNK
    n_chunks = p.shape[0] // L
    d, out = {}, {}

    def shift():
        row = lax.broadcasted_iota(jnp.int32, (p.shape[0], 1), 0)
        p_prev = jnp.where(row == 0, before, pltpu.roll(p, 1, 0))
        ps = p + (p_prev - p) * mu
        d["r"], d["k"], out["v"] = ps[:, 0:width], ps[:, width:2 * width], ps[:, 2 * width:3 * width]
        lr = ps[:, 3 * width:3 * width + DECAY_RANK + ICLR_RANK]
        gl = ps[:, 3 * width + DECAY_RANK + ICLR_RANK:]
        d["lr"], d["tanh_lr"], d["sig_gl"] = lr.astype(BF16), jnp.tanh(lr).astype(BF16), jax.nn.sigmoid(gl)
        d["kk"] = d["k"] * k_k

    def project():
        d["z"] = w0 + _mm(d.pop("tanh_lr"), w2)
        d["ai"] = a0 + _mm(d.pop("lr"), a2)
        out["gate"] = _mm(d.pop("sig_gl"), g2)
        d["ss"] = _head_sum(d["kk"] * d["kk"], hsum)

    def rates():
        d["logw"] = (-math.exp(-0.5) * LOG2_E) * jax.nn.sigmoid(d.pop("z"))
        iclr = jax.nn.sigmoid(d.pop("ai"))
        d["kk"] = d["kk"] * lax.rsqrt(jnp.maximum(d.pop("ss"), 1e-24))
        d["b"] = d["kk"] * iclr
        d["k2"] = d.pop("k") * (1.0 + (iclr - 1.0) * k_a)
        d["rk"] = d["r"] * d["k2"] * r_k

    def sums():
        out["bonus"] = _head_sum(d.pop("rk"), hsum) * out["v"]
        d["c"] = _sel_mm(tri, d["logw"])

    def scale():
        c = d.pop("c")
        c_last = [c[i * L + L - 1:i * L + L, :] for i in range(n_chunks)]
        c_end = jnp.concatenate([jnp.broadcast_to(x, (L, width)) for x in c_last], 0)
        inv_g = jnp.exp2(-c)
        to_end = jnp.exp2(c_end - c)
        b, k2 = d.pop("b"), d.pop("k2")
        out.update(ah=-d.pop("kk") * jnp.exp2(c - d.pop("logw")), rh=d.pop("r") * jnp.exp2(c),
                   bt=b * inv_g, kt=k2 * inv_g, bb=b * to_end, kb=k2 * to_end,
                   gl=[jnp.exp2(x) for x in c_last])

    return (shift, project, rates, sums, scale), out


RWKV_FEATURES = ("ah", "rh", "bt", "kt", "bb", "kb", "v", "bonus", "gate")


def _inproj_kernel(x_ref, g_ref, b_ref, w_ref, mu_ref, w0_ref, w2_ref, a0_ref, a2_ref, g2_ref,
                   kk_ref, ka_ref, rk_ref, tri_ref, hsum_ref,
                   q_ref, k_ref, v_ref, ga_ref, gb_ref, *rest, splits, tiles_per_seq):
    feat_refs = dict(zip(RWKV_FEATURES, rest[:len(RWKV_FEATURES)]))
    gl_ref, h_even, h_odd, last_s = rest[len(RWKV_FEATURES):]
    i = pl.program_id(0)
    width = w0_ref.shape[1]

    @pl.when(i == 0)
    def _():
        h_even[...] = jnp.zeros_like(h_even)
        last_s[...] = jnp.zeros_like(last_s)

    def step(h_prev, h_next):
        h_next[...] = _layer_norm(x_ref[...], g_ref[...], b_ref[...]).astype(BF16)

        def proj(lo, hi):
            return jnp.dot(h_prev[...], w_ref[:, lo:hi], preferred_element_type=F32)

        s_q, s_k, s_v, s_rw, s_ga = splits
        rw = proj(s_v, s_rw)
        seq_start = ((i - 1) % tiles_per_seq) == 0
        before = jnp.where(seq_start, 0.0, last_s[0:1, :])
        last_s[0:1, :] = rw[rw.shape[0] - 1:, :]
        (shift, project, rates, sums, scale), f = _rwkv_feature_stages(
            rw, before, mu_ref[...], w0_ref[...], w2_ref[...], a0_ref[...], a2_ref[...],
            g2_ref[...], kk_ref[...], ka_ref[...], rk_ref[...], tri_ref[...], hsum_ref[...],
            width=width)
        shift()
        ga_ref[...] = jax.nn.sigmoid(proj(s_rw, s_ga)).astype(BF16)
        project()
        gb_ref[...] = jax.nn.sigmoid(proj(s_ga, w_ref.shape[1])).astype(BF16)
        rates()
        sums()
        q_ref[...] = (proj(0, s_q) * (DIFF_HEAD_DIM ** -0.5 * LOG2_E)).astype(BF16)
        scale()
        for name, ref in feat_refs.items():
            ref[...] = f[name].astype(ref.dtype)
        gl_ref[...] = jnp.zeros_like(gl_ref)
        for c, x in enumerate(f["gl"]):
            gl_ref[c:c + 1, :] = x
        k_ref[...] = proj(s_q, s_k).astype(BF16)
        v_ref[...] = proj(s_k, s_v).astype(BF16)

    @pl.when(i % 2 == 0)
    def _():
        step(h_even, h_odd)

    @pl.when(i % 2 == 1)
    def _():
        step(h_odd, h_even)


def _inproj(x2, seq_len, ln_g, ln_b, w_in_bf, sizes, mu, w0, w2, a0, a2, g2, k_k, k_a, r_k):
    T, D = x2.shape
    n_q, n_k, n_v, n_rw, n_ga, n_gb = sizes
    width = w0.shape[-1]
    acc, cuts = 0, []
    for s in sizes[:-1]:
        acc += s
        cuts.append(acc)
    splits = tuple(cuts)
    tm, L = ROW_TILE, SCAN_CHUNK
    n_tiles = T // tm
    assert seq_len % tm == 0 and tm % L == 0 and tm // L <= 8
    assert n_rw == 3 * width + DECAY_RANK + ICLR_RANK + GATE_RANK
    assert DECAY_RANK + ICLR_RANK == LANES
    ti = jnp.arange(tm)
    same_chunk = (ti[:, None] // L) == (ti[None, :] // L)
    tri = (same_chunk & (ti[None, :] <= ti[:, None])).astype(BF16)
    ch = jnp.arange(LANES) // RWKV_HEAD_DIM
    hsum = (ch[:, None] == ch[None, :]).astype(BF16)
    w2p = jnp.concatenate([w2, jnp.zeros((ICLR_RANK, width), w2.dtype)], 0).astype(BF16)
    a2p = jnp.concatenate([jnp.zeros((DECAY_RANK, width), a2.dtype), a2], 0).astype(BF16)
    vec = lambda a: a.reshape(1, -1).astype(F32)
    row_in = pl.BlockSpec((tm, D), lambda i: (jnp.minimum(i, n_tiles - 1), 0))
    row = lambda n: pl.BlockSpec((tm, n), lambda i: (jnp.maximum(i - 1, 0), 0))
    n_feat = len(RWKV_FEATURES)
    return pl.pallas_call(
        functools.partial(_inproj_kernel, splits=splits, tiles_per_seq=seq_len // tm),
        grid=(n_tiles + 1,),
        in_specs=[row_in, _const_spec((1, D)), _const_spec((1, D)), _const_spec(w_in_bf.shape),
                  _const_spec((1, n_rw)), _const_spec((1, width)), _const_spec((LANES, width)),
                  _const_spec((1, width)), _const_spec((LANES, width)), _const_spec((GATE_RANK, width)),
                  _const_spec((1, width)), _const_spec((1, width)), _const_spec((1, width)),
                  _const_spec((tm, tm)), _const_spec((LANES, LANES))],
        out_specs=[row(n_q), row(n_k), row(n_v), row(n_ga), row(n_gb)] + [row(width)] * n_feat
                  + [pl.BlockSpec((None, 8, width), lambda i: (jnp.maximum(i - 1, 0), 0, 0))],
        out_shape=[jax.ShapeDtypeStruct((T, n_q), BF16), jax.ShapeDtypeStruct((T, n_k), BF16),
                   jax.ShapeDtypeStruct((T, n_v), BF16), jax.ShapeDtypeStruct((T, n_ga), BF16),
                   jax.ShapeDtypeStruct((T, n_gb), BF16)]
                  + [jax.ShapeDtypeStruct((T, width), BF16)] * n_feat
                  + [jax.ShapeDtypeStruct((n_tiles, 8, width), F32)],
        scratch_shapes=[pltpu.VMEM((tm, D), BF16), pltpu.VMEM((tm, D), BF16),
                        pltpu.VMEM((8, n_rw), F32)],
        compiler_params=pltpu.CompilerParams(dimension_semantics=("arbitrary",),
                                             vmem_limit_bytes=VMEM_LIMIT_BYTES),
        name="ln_inproj",
    )(x2, ln_g.reshape(1, D), ln_b.reshape(1, D), w_in_bf, vec(mu), vec(w0), w2p, vec(a0), a2p,
      g2.astype(BF16), vec(k_k), vec(k_a), vec(r_k), tri, hsum)


def _attn_stages(q_ref, k_ref, v_ref, bdiag_ref, bnear_ref, bfar_ref, lam_ref, g_ref, o_ref,
                 v1_s, *, nq, t):
    lane = lax.broadcasted_iota(jnp.int32, (t, HEAD_PAIR), 1)
    lo = lane < DIFF_HEAD_DIM
    lv = lam_ref[...]
    lam = (jnp.exp(jnp.sum(lv[0:1] * lv[1:2], keepdims=True))
           - jnp.exp(jnp.sum(lv[2:3] * lv[3:4], keepdims=True)) + LAMBDA_INIT)
    bfar = bfar_ref[...][:, 0:1]
    zero = jnp.zeros((), BF16)
    rowmax = lambda a: jnp.max(a, -1, keepdims=True)
    v1_s[:, :HEAD_PAIR] = v_ref[...]
    v1_s[:, HEAD_PAIR:] = jnp.ones((v_ref.shape[0], HEAD_PAIR), BF16)

    def logits(qi):
        q = q_ref[qi * t:(qi + 1) * t, :]
        qs = jnp.concatenate([jnp.where(lo, q, zero), jnp.where(lo, zero, q)], axis=0)
        return lax.dot_general(qs, k_ref[0:(qi + 1) * t, :], (((1,), (1,)), ((), ())),
                               preferred_element_type=F32)

    def softmax(qi, s):
        s_diag = s[:, qi * t:] + bdiag_ref[...]
        m = rowmax(s_diag)
        if qi >= 1:
            s_near = s[:, (qi - 1) * t:qi * t] + bnear_ref[...]
            m = jnp.maximum(m, rowmax(s_near))
        if qi >= 2:
            s_far = s[:, :(qi - 1) * t]
            m = jnp.maximum(m, rowmax(s_far) + bfar)
        parts = []
        if qi >= 2:
            parts.append(jnp.exp2(s_far - (m - bfar)))
        if qi >= 1:
            parts.append(jnp.exp2(s_near - m))
        parts.append(jnp.exp2(s_diag - m))
        return jnp.concatenate([x.astype(BF16) for x in parts], axis=1)

    def values(qi, p):
        acc = jnp.dot(p, v1_s[0:(qi + 1) * t, :], preferred_element_type=F32)
        l = acc[:, HEAD_PAIR:HEAD_PAIR + 1]
        acc = acc[:, :HEAD_PAIR]
        o = acc[:t] * (1.0 / l[:t]) - acc[t:] * (lam / l[t:])
        ms = jnp.mean(o * o, -1, keepdims=True)
        o = o * lax.rsqrt(ms + LN_EPS) * g_ref[...] * (1.0 - LAMBDA_INIT)
        o_ref[qi * t:(qi + 1) * t, :] = o.astype(o_ref.dtype)

    s_q, p_q = {}, {}

    def do_logits(qi):
        s_q[qi] = logits(qi)

    def do_softmax(qi):
        p_q[qi] = softmax(qi, s_q.pop(qi))

    def do_values(qi):
        values(qi, p_q.pop(qi))

    stages = []
    for step in range(nq + 2):
        if step < nq:
            stages.append(functools.partial(do_logits, step))
        if 0 <= step - 1 < nq:
            stages.append(functools.partial(do_softmax, step - 1))
        if 0 <= step - 2 < nq:
            stages.append(functools.partial(do_values, step - 2))
    return stages


def _t5_bucket(dist):
    max_exact = NUM_BUCKETS // 2
    d = jnp.maximum(dist, 1).astype(F32)
    large = max_exact + (jnp.log(d / max_exact) / math.log(MAX_DISTANCE / max_exact)
                         * (NUM_BUCKETS - max_exact)).astype(jnp.int32)
    large = jnp.minimum(large, NUM_BUCKETS - 1)
    return jnp.where(dist < max_exact, dist, large)


def _attention(q, k, v, rel_bias, lam_vecs, subln_g):
    B, S, _ = q.shape
    t = ATTN_TILE
    assert S % t == 0 and t >= MAX_DISTANCE
    nq = S // t
    by_dist = rel_bias.astype(F32)[_t5_bucket(jnp.arange(2 * t, dtype=jnp.int32))].T * LOG2_E
    dd = jnp.arange(t, dtype=jnp.int32)[:, None] - jnp.arange(t, dtype=jnp.int32)[None, :]

    def toeplitz(f_neg, f_pos):
        period = jnp.concatenate([f_neg, jnp.zeros_like(f_neg[:, :1]), f_pos[:, ::-1]], axis=1)
        rows = jnp.tile(period, (1, t))[:, :t * (2 * t - 1)].reshape(-1, t, 2 * t - 1)
        return rows[:, :, :t]

    bdiag = jnp.where(dd >= 0, toeplitz(jnp.broadcast_to(by_dist[:, :1], (DIFF_HEADS, t)),
                                        by_dist[:, 1:t]), NEG_BIG)
    bnear = toeplitz(by_dist[:, t:0:-1], by_dist[:, t + 1:])
    bdiag = jnp.concatenate([bdiag, bdiag], axis=1)
    bnear = jnp.concatenate([bnear, bnear], axis=1)
    bfar = jnp.broadcast_to((rel_bias[NUM_BUCKETS - 1] * LOG2_E)[:, None, None],
                            (DIFF_HEADS, 1, LANES))
    seq = pl.BlockSpec((None, S, HEAD_PAIR), lambda b, h: (b, 0, h))
    per_head = lambda r, c: pl.BlockSpec((None, r, c), lambda b, h: (h, 0, 0))
    return dict(
        kwargs=dict(nq=nq, t=t),
        operands=(q, k, v, bdiag, bnear, bfar, lam_vecs, subln_g.reshape(1, HEAD_PAIR)),
        in_specs=[seq, seq, seq, per_head(2 * t, t), per_head(2 * t, t), per_head(1, LANES),
                  pl.BlockSpec((4, DIFF_HEAD_DIM), lambda b, h: (0, 0)),
                  pl.BlockSpec((1, HEAD_PAIR), lambda b, h: (0, 0))],
        out_spec=seq,
        out_shape=jax.ShapeDtypeStruct((B, S, DIFF_HEADS * HEAD_PAIR), BF16),
        scratch_shapes=[pltpu.VMEM((S, 2 * HEAD_PAIR), BF16)],
        grid=(B, DIFF_HEADS))


def _rwkv_stages(ah_s, rh_s, bt_s, kt_s, bb_s, kb_s, v_s, bonus_s, gate_s, gl_s,
                 lg_ref, lb_ref, hsum_ref, o_ref, h_s, y_s, *, n_pairs):
    L = SCAN_CHUNK
    TL = ah_s.shape[0]

    @pl.when(pl.program_id(1) == 0)
    def _():
        h_s[...] = jnp.zeros_like(h_s)

    HT = SCAN_TASK
    n_tasks = TL // HT
    n_rounds = int(math.log2(L))
    n_chunks = HT // L
    n_blocks = n_chunks * n_pairs

    lane = lax.broadcasted_iota(jnp.int32, (L, LANES), 1)
    m0 = lane < RWKV_HEAD_DIM
    m1 = jnp.logical_not(m0)
    ri = lax.broadcasted_iota(jnp.int32, (2 * L, LANES), 0)
    ci = lax.broadcasted_iota(jnp.int32, (2 * L, LANES), 1)
    eye = ri == ci
    rl = lax.broadcasted_iota(jnp.int32, (L, LANES), 0)
    tril2 = (lane % L) <= rl
    stril2 = (lane % L) < rl
    eye2_f = jnp.where((lane % L) == rl, 1.0, 0.0).astype(F32)
    zeros = jnp.zeros((L, LANES), F32)

    sel = lambda m, x: jnp.where(m, x, jnp.zeros_like(x))
    cat0 = lambda *xs: jnp.concatenate(xs, 0)
    cat1 = lambda *xs: jnp.concatenate(xs, 1)
    block_diag = lambda x: cat0(sel(m0, x), sel(m1, x))
    swap_halves = lambda x: pltpu.roll(x, RWKV_HEAD_DIM, 1)
    st = [dict() for _ in range(n_tasks)]
    hs = [h_s[pi] for pi in range(n_pairs)]

    def rows_of(t):
        return slice(t * HT, (t + 1) * HT)

    def block_ld(ref, t, i):
        c, pi = divmod(i, n_pairs)
        r0 = t * HT + c * L
        return ref[r0:r0 + L, pi * LANES:(pi + 1) * LANES]

    def gram(t):
        d = st[t]
        ah = [block_ld(ah_s, t, i) for i in range(n_blocks)]
        rh = [block_ld(rh_s, t, i) for i in range(n_blocks)]
        bt = [block_ld(bt_s, t, i) for i in range(n_blocks)]
        kt = [block_ld(kt_s, t, i) for i in range(n_blocks)]
        d["ah0"] = [sel(m0, x) for x in ah]
        d["ah1"] = [sel(m1, x) for x in ah]
        d["rh"] = rh
        g = [_mm_nt(cat0(d["ah0"][i], sel(m0, rh[i]), d["ah1"][i], sel(m1, rh[i])), cat0(bt[i], kt[i]))
             for i in range(n_blocks)]
        pair = lambda x0, x1: jnp.where(m0, x0, swap_halves(x1))
        pair_r = lambda x0, x1: jnp.where(m0, swap_halves(x0), x1)
        d["ak"] = [sel(stril2, pair_r(x[:L], x[2 * L:3 * L])) for x in g]
        d["rb"] = [sel(tril2, pair(x[L:2 * L], x[3 * L:])) for x in g]
        d["rk"] = [sel(tril2, pair_r(x[L:2 * L], x[3 * L:])) for x in g]
        d["pw"] = [sel(stril2, pair(x[:L], x[2 * L:3 * L])) for x in g]
        d["tinv"] = [eye2_f + x for x in d["pw"]]
        d["round"] = 0

    def neumann(t):
        d = st[t]
        first, last = d["round"] == 0, d["round"] == n_rounds - 1
        d["round"] += 1
        if first:
            d["pw"] = [_mm(x, block_diag(x)) for x in d["pw"]]
            return
        out = [_mm(tm if last else cat0(tm, x), block_diag(x)) for x, tm in zip(d["pw"], d["tinv"])]
        d["tinv"] = [tm + o[:L] for tm, o in zip(d["tinv"], out)]
        if not last:
            d["pw"] = [o[L:] for o in out]

    def solve(t):
        d = st[t]
        d.pop("pw")
        vv = [block_ld(v_s, t, i) for i in range(n_blocks)]
        d["vv"] = vv
        kv = [_mm(cat0(ak, rk), block_diag(x)) for ak, rk, x in zip(d.pop("ak"), d.pop("rk"), vv)]
        d["rkv"] = [x[L:] for x in kv]
        w = [_mm(tm, cat1(cat0(a0_, a1_), block_diag(x[:L])))
             for tm, a0_, a1_, x in zip(d.pop("tinv"), d.pop("ah0"), d.pop("ah1"), kv)]
        d["w1"] = [x[:, :LANES] for x in w]
        d["w2"] = [x[:, LANES:] for x in w]

    def fold(t):
        d = st[t]
        lhs, cm, y0, gcol = [], [], [], []
        for i in range(n_blocks):
            w1, w2, vv = d["w1"][i], d["w2"][i], d["vv"][i]
            ac = _mm_tn(cat0(block_ld(bb_s, t, i), block_ld(kb_s, t, i)),
                        cat1(cat0(w1, zeros), cat0(w2, vv)))
            ry0 = _mm(d["rb"][i], cat1(block_diag(w1), block_diag(w2)))
            a_mat = jnp.where(m0, ac[:L, :LANES], ac[L:, :LANES])
            cm.append(jnp.where(m0, ac[:L, LANES:], ac[L:, LANES:]))
            lhs.append(cat0(a_mat, d["rh"][i] + ry0[:, :LANES]).astype(BF16))
            y0.append(ry0[:, LANES:] + d["rkv"][i])
            ch, pi = divmod(i, n_pairs)
            g_end = gl_s[t, ch:ch + 1, pi * LANES:(pi + 1) * LANES]
            g_rows = jnp.sum(jnp.where(eye, jnp.broadcast_to(g_end, (2 * L, LANES)), 0.0),
                             axis=1, keepdims=True)
            gcol.append(jnp.where(m0, g_rows[:L], g_rows[L:]))
        for key in ("w1", "w2", "vv", "rb", "rkv", "rh"):
            d.pop(key)
        d.update(lhs=lhs, cm=cm, y0=y0, gcol=gcol)

    def scan(t, c):
        d = st[t]
        r0 = t * HT + c * L
        for pi in range(n_pairs):
            i = c * n_pairs + pi
            h = hs[pi]
            out = _mm(d["lhs"][i], block_diag(h))
            hs[pi] = d["gcol"][i] * h + out[:L] + d["cm"][i]
            y_s[r0:r0 + L, pi * LANES:(pi + 1) * LANES] = out[L:] + d["y0"][i]

    def finish(t):
        rows = rows_of(t)
        y = y_s[rows, :]
        inv_n = 1.0 / RWKV_HEAD_DIM
        mean = _head_sum(y, hsum_ref[...]) * inv_n
        yc = y - mean
        var = _head_sum(yc * yc, hsum_ref[...]) * inv_n
        yn = yc * lax.rsqrt(var + LNX_EPS) * lg_ref[...] + lb_ref[...]
        o_ref[rows, :] = ((yn + bonus_s[rows, :].astype(F32))
                          * gate_s[rows, :].astype(F32)).astype(o_ref.dtype)

    assert n_rounds >= n_chunks

    def carry_out():
        for pi in range(n_pairs):
            h_s[pi] = hs[pi]

    stages = []
    add = lambda f, *a: stages.append(functools.partial(f, *a))
    for t in range(n_tasks):
        prv = t - 1 if t >= 1 else None
        add(gram, t)
        for rnd in range(n_rounds):
            add(neumann, t)
            if prv is not None and rnd < n_chunks:
                add(scan, prv, rnd)
        if prv is not None:
            add(finish, prv)
        add(solve, t); add(fold, t)
    for c in range(n_chunks):
        add(scan, n_tasks - 1, c)
    add(finish, n_tasks - 1)
    add(carry_out)
    return stages


def _rwkv(features, decay_rows, seq_len, lnx_g, lnx_b):
    T, width = features[0].shape
    B, S = T // seq_len, seq_len
    n_pairs = width // LANES
    L, TL = SCAN_CHUNK, min(SCAN_TILE, S)
    assert S % TL == 0 and TL % SCAN_TASK == 0 and SCAN_TASK == ROW_TILE
    ch = jnp.arange(LANES) // RWKV_HEAD_DIM
    hsum = (ch[:, None] == ch[None, :]).astype(BF16)
    vec = lambda a: a.reshape(1, -1).astype(F32)
    tile = pl.BlockSpec((None, TL, width), lambda b, j: (b, j, 0))
    tasks = TL // SCAN_TASK
    return dict(
        kwargs=dict(n_pairs=n_pairs),
        operands=tuple(f.reshape(B, S, width) for f in features)
                 + (decay_rows.reshape(B, S // SCAN_TASK, 8, width), vec(lnx_g), vec(lnx_b), hsum),
        in_specs=[tile] * len(features)
                 + [pl.BlockSpec((None, tasks, 8, width), lambda b, j: (b, j, 0, 0)),
                    _const_spec((1, width)), _const_spec((1, width)), _const_spec((LANES, LANES))],
        out_spec=tile,
        out_shape=jax.ShapeDtypeStruct((B, S, width), BF16),
        scratch_shapes=[pltpu.VMEM((n_pairs, L, LANES), F32), pltpu.VMEM((TL, width), F32)],
        grid=(B, S // TL))


def _run_stages(stages_fn, *refs, **kwargs):
    for stage in stages_fn(*refs, **kwargs):
        stage()


def _mixer_call(call, stages_fn, semantics, name):
    return pl.pallas_call(
        functools.partial(_run_stages, stages_fn, **call["kwargs"]),
        grid=call["grid"],
        in_specs=call["in_specs"],
        out_specs=call["out_spec"],
        out_shape=call["out_shape"],
        scratch_shapes=call.get("scratch_shapes", ()),
        compiler_params=pltpu.CompilerParams(dimension_semantics=semantics,
                                             vmem_limit_bytes=VMEM_LIMIT_BYTES),
        name=name,
    )(*call["operands"])


def _post_kernel(x_ref, lng_ref, lnb_ref, ya_ref, yb_ref, ga_ref, gb_ref, wua_ref, wub_ref, wo_ref,
                 l1g_ref, l1b_ref, wg_ref, wu_ref, wd_ref, l2g_ref, l2b_ref, o_ref,
                 z_even, z_odd, *, ffn_cuts):
    i = pl.program_id(0)
    dot = functools.partial(jnp.dot, preferred_element_type=F32)

    @pl.when(i == 0)
    def _():
        z_even[...] = jnp.zeros_like(z_even)

    def ffn_part(h1b, lo, hi):
        gate = dot(h1b, wg_ref[:, lo:hi])
        up = dot(h1b, wu_ref[:, lo:hi])
        act = (gate * jax.nn.sigmoid(gate) * up).astype(BF16)
        return dot(act, wd_ref[lo:hi, :])

    def step(z_prev, z_next):
        up_a = dot(ya_ref[...], wua_ref[...])
        up_b = dot(yb_ref[...], wub_ref[...])
        h1 = _layer_norm(z_prev[...], l1g_ref[...], l1b_ref[...])
        h1b = h1.astype(BF16)
        ffn = ffn_part(h1b, ffn_cuts[0], ffn_cuts[1])
        merged = ga_ref[...].astype(F32) * up_a + gb_ref[...].astype(F32) * up_b
        for lo, hi in zip(ffn_cuts[1:-1], ffn_cuts[2:]):
            ffn = ffn + ffn_part(h1b, lo, hi)
        mix = dot(merged.astype(BF16), wo_ref[...])
        o_ref[...] = _layer_norm(DEEPNORM_ALPHA * h1 + ffn, l2g_ref[...], l2b_ref[...])
        h = _layer_norm(x_ref[...], lng_ref[...], lnb_ref[...])
        z_next[...] = DEEPNORM_ALPHA * h + mix

    @pl.when(i % 2 == 0)
    def _():
        step(z_even, z_odd)

    @pl.when(i % 2 == 1)
    def _():
        step(z_odd, z_even)


def _post(x2, ln_g, ln_b, ya, yb, ga, gb, w_up_a, w_up_b, w_out, ln1_g, ln1_b,
          w_gate, w_up, w_down, ln2_g, ln2_b):
    T, D = x2.shape
    hidden = w_gate.shape[1]
    assert hidden % MXU_DEPTH == 0
    ffn_cuts = (0, (hidden // MXU_DEPTH // 2) * MXU_DEPTH, hidden)
    tm = ROW_TILE
    n_tiles = T // tm
    row = lambda n: pl.BlockSpec((tm, n), lambda i: (jnp.minimum(i, n_tiles - 1), 0))
    vec = lambda a: a.reshape(1, D)
    cs = lambda a: _const_spec(a.shape)
    wts = [w.astype(BF16) for w in (w_up_a, w_up_b, w_out, w_gate, w_up, w_down)]
    wua, wub, wo, wg, wu, wd = wts
    return pl.pallas_call(
        functools.partial(_post_kernel, ffn_cuts=ffn_cuts),
        grid=(n_tiles + 1,),
        in_specs=[row(D), _const_spec((1, D)), _const_spec((1, D)), row(ya.shape[1]), row(yb.shape[1]),
                  row(D), row(D), cs(wua), cs(wub), cs(wo), _const_spec((1, D)), _const_spec((1, D)),
                  cs(wg), cs(wu), cs(wd), _const_spec((1, D)), _const_spec((1, D))],
        out_specs=pl.BlockSpec((tm, D), lambda i: (jnp.maximum(i - 1, 0), 0)),
        out_shape=jax.ShapeDtypeStruct((T, D), F32),
        scratch_shapes=[pltpu.VMEM((tm, D), F32), pltpu.VMEM((tm, D), F32)],
        compiler_params=pltpu.CompilerParams(dimension_semantics=("arbitrary",),
                                             vmem_limit_bytes=VMEM_LIMIT_BYTES),
        name="merge_ffn",
    )(x2, vec(ln_g), vec(ln_b), ya, yb, ga, gb, wua, wub, wo, vec(ln1_g), vec(ln1_b),
      wg, wu, wd, vec(ln2_g), vec(ln2_b))


def kernel(x, ln_in_g, ln_in_b, rel_bias, w_in, diff_lam_q1, diff_lam_k1, diff_lam_q2, diff_lam_k2, diff_subln_g, rwkv_mu, rwkv_w0, rwkv_w2, rwkv_a0, rwkv_a2, rwkv_g2, rwkv_k_k, rwkv_k_a, rwkv_r_k, rwkv_lnx_g, rwkv_lnx_b, w_up_a, w_up_b, w_out, ln1_g, ln1_b, ffn_w_gate, ffn_w_up, ffn_w_down, ln2_g, ln2_b):
    B, S, D = x.shape
    T = B * S
    width = rwkv_w0.shape[-1]
    n_qk = DIFF_HEADS * HEAD_PAIR
    n_rw = 3 * width + DECAY_RANK + ICLR_RANK + GATE_RANK
    sizes = (n_qk, n_qk, n_qk, n_rw, D, D)
    assert w_in.shape[0] == 1 and w_in.shape[2] == sum(sizes), "single-layer model expected"
    x2 = x.reshape(T, D)
    q, k, v, ga, gb, *rwkv_features, decay_rows = _inproj(
        x2, S, ln_in_g, ln_in_b, w_in[0].astype(BF16), sizes, rwkv_mu[0], rwkv_w0[0], rwkv_w2[0],
        rwkv_a0[0], rwkv_a2[0], rwkv_g2[0], rwkv_k_k[0], rwkv_k_a[0], rwkv_r_k[0])
    lam_vecs = jnp.concatenate([diff_lam_q1, diff_lam_k1, diff_lam_q2, diff_lam_k2], 0).astype(F32)
    ya = _mixer_call(
        _attention(q.reshape(B, S, n_qk), k.reshape(B, S, n_qk), v.reshape(B, S, n_qk),
                   rel_bias, lam_vecs, diff_subln_g[0]),
        _attn_stages, ("parallel", "parallel"), "diff_attention")
    yb = _mixer_call(_rwkv(rwkv_features, decay_rows, S, rwkv_lnx_g[0], rwkv_lnx_b[0]),
                     _rwkv_stages, ("parallel", "arbitrary"), "rwkv7_chunked")
    out = _post(x2, ln_in_g, ln_in_b, ya.reshape(T, n_qk), yb.reshape(T, width), ga, gb,
                w_up_a[0], w_up_b[0], w_out[0], ln1_g[0], ln1_b[0],
                ffn_w_gate[0], ffn_w_up[0], ffn_w_down[0], ln2_g[0], ln2_b[0])
    return out.reshape(B, S, D)
```

```python
import functools
import math

import jax
import jax.numpy as jnp
from jax import lax
from jax.experimental import pallas as pl
from jax.experimental.pallas import tpu as pltpu

F32 = jnp.float32
BF16 = jnp.bfloat16

DIFF_HEADS = 4
DIFF_HEAD_DIM = 64
HEAD_PAIR = 2 * DIFF_HEAD_DIM
RWKV_HEAD_DIM = 64
DECAY_RANK = 64
ICLR_RANK = 64
GATE_RANK = 128
NUM_BUCKETS = 32
MAX_DISTANCE = 128
LN_EPS = 1e-5
LNX_EPS = 64e-5
NEG_BIG = -1e30
DEPTH = 1
DEEPNORM_ALPHA = (2.0 * DEPTH) ** 0.25
LAMBDA_INIT = 0.8 - 0.6 * math.exp(-0.3 * 0)
LOG2_E = math.log2(math.e)

LANES = 128
MXU_DEPTH = 256
VMEM_LIMIT_BYTES = 56 * 1024 * 1024
ROW_TILE = 256
ATTN_TILE = 256
SCAN_CHUNK = 64
SCAN_TASK = 256
SCAN_TILE = 1024


def _mm(a, b):
    return jnp.dot(a.astype(BF16), b.astype(BF16), preferred_element_type=F32)


def _mm_nt(a, b):
    return lax.dot_general(a.astype(BF16), b.astype(BF16), (((1,), (1,)), ((), ())),
                           preferred_element_type=F32)


def _mm_tn(a, b):
    return lax.dot_general(a.astype(BF16), b.astype(BF16), (((0,), (0,)), ((), ())),
                           preferred_element_type=F32)


def _split2(x):
    hi = x.astype(BF16)
    lo = (x - hi.astype(F32)).astype(BF16)
    return hi, lo


def _sel_mm(sel, x):
    hi, lo = _split2(x)
    d = functools.partial(jnp.dot, preferred_element_type=F32)
    return d(sel, hi) + d(sel, lo)


def _head_sum(x, sel):
    xb = x.astype(BF16)
    return jnp.concatenate(
        [jnp.dot(xb[:, c0:c0 + LANES], sel, preferred_element_type=F32)
         for c0 in range(0, x.shape[1], LANES)], axis=1)


def _layer_norm(x, g, b, eps=LN_EPS):
    mu = jnp.mean(x, -1, keepdims=True)
    xc = x - mu
    var = jnp.mean(xc * xc, -1, keepdims=True)
    return xc * lax.rsqrt(var + eps) * g + b


def _const_spec(shape):
    return pl.BlockSpec(shape, lambda *_: (0,) * len(shape), pipeline_mode=pl.Buffered(1))


def _rwkv_feature_stages(p, before, mu, w0, w2, a0, a2, g2, k_k, k_a, r_k, tri, hsum, *, width):
    L = SCAN_CHUNK
    n_chunks = p.shape[0] // L
    d, out = {}, {}

    def shift():
        row = lax.broadcasted_iota(jnp.int32, (p.shape[0], 1), 0)
        p_prev = jnp.where(row == 0, before, pltpu.roll(p, 1, 0))
        ps = p + (p_prev - p) * mu
        d["r"], d["k"], out["v"] = ps[:, 0:width], ps[:, width:2 * width], ps[:, 2 * width:3 * width]
        lr = ps[:, 3 * width:3 * width + DECAY_RANK + ICLR_RANK]
        gl = ps[:, 3 * width + DECAY_RANK + ICLR_RANK:]
        d["lr"], d["tanh_lr"], d["sig_gl"] = lr.astype(BF16), jnp.tanh(lr).astype(BF16), jax.nn.sigmoid(gl)
        d["kk"] = d["k"] * k_k

    def project():
        d["z"] = w0 + _mm(d.pop("tanh_lr"), w2)
        d["ai"] = a0 + _mm(d.pop("lr"), a2)
        out["gate"] = _mm(d.pop("sig_gl"), g2)
        d["ss"] = _head_sum(d["kk"] * d["kk"], hsum)

    def rates():
        d["logw"] = (-math.exp(-0.5) * LOG2_E) * jax.nn.sigmoid(d.pop("z"))
        iclr = jax.nn.sigmoid(d.pop("ai"))
        d["kk"] = d["kk"] * lax.rsqrt(jnp.maximum(d.pop("ss"), 1e-24))
        d["b"] = d["kk"] * iclr
        d["k2"] = d.pop("k") * (1.0 + (iclr - 1.0) * k_a)
        d["rk"] = d["r"] * d["k2"] * r_k

    def sums():
        out["bonus"] = _head_sum(d.pop("rk"), hsum) * out["v"]
        d["c"] = _sel_mm(tri, d["logw"])

    def scale():
        c = d.pop("c")
        c_last = [c[i * L + L - 1:i * L + L, :] for i in range(n_chunks)]
        c_end = jnp.concatenate([jnp.broadcast_to(x, (L, width)) for x in c_last], 0)
        inv_g = jnp.exp2(-c)
        to_end = jnp.exp2(c_end - c)
        b, k2 = d.pop("b"), d.pop("k2")
        out.update(ah=-d.pop("kk") * jnp.exp2(c - d.pop("logw")), rh=d.pop("r") * jnp.exp2(c),
                   bt=b * inv_g, kt=k2 * inv_g, bb=b * to_end, kb=k2 * to_end,
                   gl=[jnp.exp2(x) for x in c_last])

    return (shift, project, rates, sums, scale), out


RWKV_FEATURES = ("ah", "rh", "bt", "kt", "bb", "kb", "v", "bonus", "gate")


def _inproj_kernel(x_ref, g_ref, b_ref, w_ref, mu_ref, w0_ref, w2_ref, a0_ref, a2_ref, g2_ref,
                   kk_ref, ka_ref, rk_ref, tri_ref, hsum_ref,
                   q_ref, k_ref, v_ref, ga_ref, gb_ref, *rest, splits, tiles_per_seq):
    feat_refs = dict(zip(RWKV_FEATURES, rest[:len(RWKV_FEATURES)]))
    gl_ref, h_even, h_odd, last_s = rest[len(RWKV_FEATURES):]
    i = pl.program_id(0)
    width = w0_ref.shape[1]

    @pl.when(i == 0)
    def _():
        h_even[...] = jnp.zeros_like(h_even)
        last_s[...] = jnp.zeros_like(last_s)

    def step(h_prev, h_next):
        h_next[...] = _layer_norm(x_ref[...], g_ref[...], b_ref[...]).astype(BF16)

        def proj(lo, hi):
            return jnp.dot(h_prev[...], w_ref[:, lo:hi], preferred_element_type=F32)

        s_q, s_k, s_v, s_rw, s_ga = splits
        rw = proj(s_v, s_rw)
        seq_start = ((i - 1) % tiles_per_seq) == 0
        before = jnp.where(seq_start, 0.0, last_s[0:1, :])
        last_s[0:1, :] = rw[rw.shape[0] - 1:, :]
        (shift, project, rates, sums, scale), f = _rwkv_feature_stages(
            rw, before, mu_ref[...], w0_ref[...], w2_ref[...], a0_ref[...], a2_ref[...],
            g2_ref[...], kk_ref[...], ka_ref[...], rk_ref[...], tri_ref[...], hsum_ref[...],
            width=width)
        shift()
        ga_ref[...] = jax.nn.sigmoid(proj(s_rw, s_ga)).astype(BF16)
        project()
        gb_ref[...] = jax.nn.sigmoid(proj(s_ga, w_ref.shape[1])).astype(BF16)
        rates()
        sums()
        q_ref[...] = (proj(0, s_q) * (DIFF_HEAD_DIM ** -0.5 * LOG2_E)).astype(BF16)
        scale()
        for name, ref in feat_refs.items():
            ref[...] = f[name].astype(ref.dtype)
        gl_ref[...] = jnp.zeros_like(gl_ref)
        for c, x in enumerate(f["gl"]):
            gl_ref[c:c + 1, :] = x
        k_ref[...] = proj(s_q, s_k).astype(BF16)
        v_ref[...] = proj(s_k, s_v).astype(BF16)

    @pl.when(i % 2 == 0)
    def _():
        step(h_even, h_odd)

    @pl.when(i % 2 == 1)
    def _():
        step(h_odd, h_even)


def _inproj(x2, seq_len, ln_g, ln_b, w_in_bf, sizes, mu, w0, w2, a0, a2, g2, k_k, k_a, r_k):
    T, D = x2.shape
    n_q, n_k, n_v, n_rw, n_ga, n_gb = sizes
    width = w0.shape[-1]
    acc, cuts = 0, []
    for s in sizes[:-1]:
        acc += s
        cuts.append(acc)
    splits = tuple(cuts)
    tm, L = ROW_TILE, SCAN_CHUNK
    n_tiles = T // tm
    assert seq_len % tm == 0 and tm % L == 0 and tm // L <= 8
    assert n_rw == 3 * width + DECAY_RANK + ICLR_RANK + GATE_RANK
    assert DECAY_RANK + ICLR_RANK == LANES
    ti = jnp.arange(tm)
    same_chunk = (ti[:, None] // L) == (ti[None, :] // L)
    tri = (same_chunk & (ti[None, :] <= ti[:, None])).astype(BF16)
    ch = jnp.arange(LANES) // RWKV_HEAD_DIM
    hsum = (ch[:, None] == ch[None, :]).astype(BF16)
    w2p = jnp.concatenate([w2, jnp.zeros((ICLR_RANK, width), w2.dtype)], 0).astype(BF16)
    a2p = jnp.concatenate([jnp.zeros((DECAY_RANK, width), a2.dtype), a2], 0).astype(BF16)
    vec = lambda a: a.reshape(1, -1).astype(F32)
    row_in = pl.BlockSpec((tm, D), lambda i: (jnp.minimum(i, n_tiles - 1), 0))
    row = lambda n: pl.BlockSpec((tm, n), lambda i: (jnp.maximum(i - 1, 0), 0))
    n_feat = len(RWKV_FEATURES)
    return pl.pallas_call(
        functools.partial(_inproj_kernel, splits=splits, tiles_per_seq=seq_len // tm),
        grid=(n_tiles + 1,),
        in_specs=[row_in, _const_spec((1, D)), _const_spec((1, D)), _const_spec(w_in_bf.shape),
                  _const_spec((1, n_rw)), _const_spec((1, width)), _const_spec((LANES, width)),
                  _const_spec((1, width)), _const_spec((LANES, width)), _const_spec((GATE_RANK, width)),
                  _const_spec((1, width)), _const_spec((1, width)), _const_spec((1, width)),
                  _const_spec((tm, tm)), _const_spec((LANES, LANES))],
        out_specs=[row(n_q), row(n_k), row(n_v), row(n_ga), row(n_gb)] + [row(width)] * n_feat
                  + [pl.BlockSpec((None, 8, width), lambda i: (jnp.maximum(i - 1, 0), 0, 0))],
        out_shape=[jax.ShapeDtypeStruct((T, n_q), BF16), jax.ShapeDtypeStruct((T, n_k), BF16),
                   jax.ShapeDtypeStruct((T, n_v), BF16), jax.ShapeDtypeStruct((T, n_ga), BF16),
                   jax.ShapeDtypeStruct((T, n_gb), BF16)]
                  + [jax.ShapeDtypeStruct((T, width), BF16)] * n_feat
                  + [jax.ShapeDtypeStruct((n_tiles, 8, width), F32)],
        scratch_shapes=[pltpu.VMEM((tm, D), BF16), pltpu.VMEM((tm, D), BF16),
                        pltpu.VMEM((8, n_rw), F32)],
        compiler_params=pltpu.CompilerParams(dimension_semantics=("arbitrary",),
                                             vmem_limit_bytes=VMEM_LIMIT_BYTES),
        name="ln_inproj",
    )(x2, ln_g.reshape(1, D), ln_b.reshape(1, D), w_in_bf, vec(mu), vec(w0), w2p, vec(a0), a2p,
      g2.astype(BF16), vec(k_k), vec(k_a), vec(r_k), tri, hsum)


def _attn_stages(q_ref, k_ref, v_ref, bdiag_ref, bnear_ref, bfar_ref, lam_ref, g_ref, o_ref,
                 v1_s, *, nq, t):
    lane = lax.broadcasted_iota(jnp.int32, (t, HEAD_PAIR), 1)
    lo = lane < DIFF_HEAD_DIM
    lv = lam_ref[...]
    lam = (jnp.exp(jnp.sum(lv[0:1] * lv[1:2], keepdims=True))
           - jnp.exp(jnp.sum(lv[2:3] * lv[3:4], keepdims=True)) + LAMBDA_INIT)
    bfar = bfar_ref[...][:, 0:1]
    zero = jnp.zeros((), BF16)
    rowmax = lambda a: jnp.max(a, -1, keepdims=True)
    v1_s[:, :HEAD_PAIR] = v_ref[...]
    v1_s[:, HEAD_PAIR:] = jnp.ones((v_ref.shape[0], HEAD_PAIR), BF16)

    def logits(qi):
        q = q_ref[qi * t:(qi + 1) * t, :]
        qs = jnp.concatenate([jnp.where(lo, q, zero), jnp.where(lo, zero, q)], axis=0)
        return lax.dot_general(qs, k_ref[0:(qi + 1) * t, :], (((1,), (1,)), ((), ())),
                               preferred_element_type=F32)

    def softmax(qi, s):
        s_diag = s[:, qi * t:] + bdiag_ref[...]
        m = rowmax(s_diag)
        if qi >= 1:
            s_near = s[:, (qi - 1) * t:qi * t] + bnear_ref[...]
            m = jnp.maximum(m, rowmax(s_near))
        if qi >= 2:
            s_far = s[:, :(qi - 1) * t]
            m = jnp.maximum(m, rowmax(s_far) + bfar)
        parts = []
        if qi >= 2:
            parts.append(jnp.exp2(s_far - (m - bfar)))
        if qi >= 1:
            parts.append(jnp.exp2(s_near - m))
        parts.append(jnp.exp2(s_diag - m))
        return jnp.concatenate([x.astype(BF16) for x in parts], axis=1)

    def values(qi, p):
        acc = jnp.dot(p, v1_s[0:(qi + 1) * t, :], preferred_element_type=F32)
        l = acc[:, HEAD_PAIR:HEAD_PAIR + 1]
        acc = acc[:, :HEAD_PAIR]
        o = acc[:t] * (1.0 / l[:t]) - acc[t:] * (lam / l[t:])
        ms = jnp.mean(o * o, -1, keepdims=True)
        o = o * lax.rsqrt(ms + LN_EPS) * g_ref[...] * (1.0 - LAMBDA_INIT)
        o_ref[qi * t:(qi + 1) * t, :] = o.astype(o_ref.dtype)

    s_q, p_q = {}, {}

    def do_logits(qi):
        s_q[qi] = logits(qi)

    def do_softmax(qi):
        p_q[qi] = softmax(qi, s_q.pop(qi))

    def do_values(qi):
        values(qi, p_q.pop(qi))

    stages = []
    for step in range(nq + 2):
        if step < nq:
            stages.append(functools.partial(do_logits, step))
        if 0 <= step - 1 < nq:
            stages.append(functools.partial(do_softmax, step - 1))
        if 0 <= step - 2 < nq:
            stages.append(functools.partial(do_values, step - 2))
    return stages


def _t5_bucket(dist):
    max_exact = NUM_BUCKETS // 2
    d = jnp.maximum(dist, 1).astype(F32)
    large = max_exact + (jnp.log(d / max_exact) / math.log(MAX_DISTANCE / max_exact)
                         * (NUM_BUCKETS - max_exact)).astype(jnp.int32)
    large = jnp.minimum(large, NUM_BUCKETS - 1)
    return jnp.where(dist < max_exact, dist, large)


def _attention(q, k, v, rel_bias, lam_vecs, subln_g):
    B, S, _ = q.shape
    t = ATTN_TILE
    assert S % t == 0 and t >= MAX_DISTANCE
    nq = S // t
    by_dist = rel_bias.astype(F32)[_t5_bucket(jnp.arange(2 * t, dtype=jnp.int32))].T * LOG2_E
    dd = jnp.arange(t, dtype=jnp.int32)[:, None] - jnp.arange(t, dtype=jnp.int32)[None, :]

    def toeplitz(f_neg, f_pos):
        period = jnp.concatenate([f_neg, jnp.zeros_like(f_neg[:, :1]), f_pos[:, ::-1]], axis=1)
        rows = jnp.tile(period, (1, t))[:, :t * (2 * t - 1)].reshape(-1, t, 2 * t - 1)
        return rows[:, :, :t]

    bdiag = jnp.where(dd >= 0, toeplitz(jnp.broadcast_to(by_dist[:, :1], (DIFF_HEADS, t)),
                                        by_dist[:, 1:t]), NEG_BIG)
    bnear = toeplitz(by_dist[:, t:0:-1], by_dist[:, t + 1:])
    bdiag = jnp.concatenate([bdiag, bdiag], axis=1)
    bnear = jnp.concatenate([bnear, bnear], axis=1)
    bfar = jnp.broadcast_to((rel_bias[NUM_BUCKETS - 1] * LOG2_E)[:, None, None],
                            (DIFF_HEADS, 1, LANES))
    seq = pl.BlockSpec((None, S, HEAD_PAIR), lambda b, h: (b, 0, h))
    per_head = lambda r, c: pl.BlockSpec((None, r, c), lambda b, h: (h, 0, 0))
    return dict(
        kwargs=dict(nq=nq, t=t),
        operands=(q, k, v, bdiag, bnear, bfar, lam_vecs, subln_g.reshape(1, HEAD_PAIR)),
        in_specs=[seq, seq, seq, per_head(2 * t, t), per_head(2 * t, t), per_head(1, LANES),
                  pl.BlockSpec((4, DIFF_HEAD_DIM), lambda b, h: (0, 0)),
                  pl.BlockSpec((1, HEAD_PAIR), lambda b, h: (0, 0))],
        out_spec=seq,
        out_shape=jax.ShapeDtypeStruct((B, S, DIFF_HEADS * HEAD_PAIR), BF16),
        scratch_shapes=[pltpu.VMEM((S, 2 * HEAD_PAIR), BF16)],
        grid=(B, DIFF_HEADS))


def _rwkv_stages(ah_s, rh_s, bt_s, kt_s, bb_s, kb_s, v_s, bonus_s, gate_s, gl_s,
                 lg_ref, lb_ref, hsum_ref, o_ref, h_s, y_s, *, n_pairs):
    L = SCAN_CHUNK
    TL = ah_s.shape[0]

    @pl.when(pl.program_id(1) == 0)
    def _():
        h_s[...] = jnp.zeros_like(h_s)

    HT = SCAN_TASK
    n_tasks = TL // HT
    n_rounds = int(math.log2(L))
    n_chunks = HT // L
    n_blocks = n_chunks * n_pairs

    lane = lax.broadcasted_iota(jnp.int32, (L, LANES), 1)
    m0 = lane < RWKV_HEAD_DIM
    m1 = jnp.logical_not(m0)
    ri = lax.broadcasted_iota(jnp.int32, (2 * L, LANES), 0)
    ci = lax.broadcasted_iota(jnp.int32, (2 * L, LANES), 1)
    eye = ri == ci
    rl = lax.broadcasted_iota(jnp.int32, (L, LANES), 0)
    tril2 = (lane % L) <= rl
    stril2 = (lane % L) < rl
    eye2_f = jnp.where((lane % L) == rl, 1.0, 0.0).astype(F32)
    zeros = jnp.zeros((L, LANES), F32)

    sel = lambda m, x: jnp.where(m, x, jnp.zeros_like(x))
    cat0 = lambda *xs: jnp.concatenate(xs, 0)
    cat1 = lambda *xs: jnp.concatenate(xs, 1)
    block_diag = lambda x: cat0(sel(m0, x), sel(m1, x))
    swap_halves = lambda x: pltpu.roll(x, RWKV_HEAD_DIM, 1)
    st = [dict() for _ in range(n_tasks)]
    hs = [h_s[pi] for pi in range(n_pairs)]

    def rows_of(t):
        return slice(t * HT, (t + 1) * HT)

    def block_ld(ref, t, i):
        c, pi = divmod(i, n_pairs)
        r0 = t * HT + c * L
        return ref[r0:r0 + L, pi * LANES:(pi + 1) * LANES]

    def gram(t):
        d = st[t]
        ah = [block_ld(ah_s, t, i) for i in range(n_blocks)]
        rh = [block_ld(rh_s, t, i) for i in range(n_blocks)]
        bt = [block_ld(bt_s, t, i) for i in range(n_blocks)]
        kt = [block_ld(kt_s, t, i) for i in range(n_blocks)]
        d["ah0"] = [sel(m0, x) for x in ah]
        d["ah1"] = [sel(m1, x) for x in ah]
        d["rh"] = rh
        g = [_mm_nt(cat0(d["ah0"][i], sel(m0, rh[i]), d["ah1"][i], sel(m1, rh[i])), cat0(bt[i], kt[i]))
             for i in range(n_blocks)]
        pair = lambda x0, x1: jnp.where(m0, x0, swap_halves(x1))
        pair_r = lambda x0, x1: jnp.where(m0, swap_halves(x0), x1)
        d["ak"] = [sel(stril2, pair_r(x[:L], x[2 * L:3 * L])) for x in g]
        d["rb"] = [sel(tril2, pair(x[L:2 * L], x[3 * L:])) for x in g]
        d["rk"] = [sel(tril2, pair_r(x[L:2 * L], x[3 * L:])) for x in g]
        d["pw"] = [sel(stril2, pair(x[:L], x[2 * L:3 * L])) for x in g]
        d["tinv"] = [eye2_f + x for x in d["pw"]]
        d["round"] = 0

    def neumann(t):
        d = st[t]
        first, last = d["round"] == 0, d["round"] == n_rounds - 1
        d["round"] += 1
        if first:
            d["pw"] = [_mm(x, block_diag(x)) for x in d["pw"]]
            return
        out = [_mm(tm if last else cat0(tm, x), block_diag(x)) for x, tm in zip(d["pw"], d["tinv"])]
        d["tinv"] = [tm + o[:L] for tm, o in zip(d["tinv"], out)]
        if not last:
            d["pw"] = [o[L:] for o in out]

    def solve(t):
        d = st[t]
        d.pop("pw")
        vv = [block_ld(v_s, t, i) for i in range(n_blocks)]
        d["vv"] = vv
        kv = [_mm(cat0(ak, rk), block_diag(x)) for ak, rk, x in zip(d.pop("ak"), d.pop("rk"), vv)]
        d["rkv"] = [x[L:] for x in kv]
        w = [_mm(tm, cat1(cat0(a0_, a1_), block_diag(x[:L])))
             for tm, a0_, a1_, x in zip(d.pop("tinv"), d.pop("ah0"), d.pop("ah1"), kv)]
        d["w1"] = [x[:, :LANES] for x in w]
        d["w2"] = [x[:, LANES:] for x in w]

    def fold(t):
        d = st[t]
        lhs, cm, y0, gcol = [], [], [], []
        for i in range(n_blocks):
            w1, w2, vv = d["w1"][i], d["w2"][i], d["vv"][i]
            ac = _mm_tn(cat0(block_ld(bb_s, t, i), block_ld(kb_s, t, i)),
                        cat1(cat0(w1, zeros), cat0(w2, vv)))
            ry0 = _mm(d["rb"][i], cat1(block_diag(w1), block_diag(w2)))
            a_mat = jnp.where(m0, ac[:L, :LANES], ac[L:, :LANES])
            cm.append(jnp.where(m0, ac[:L, LANES:], ac[L:, LANES:]))
            lhs.append(cat0(a_mat, d["rh"][i] + ry0[:, :LANES]).astype(BF16))
            y0.append(ry0[:, LANES:] + d["rkv"][i])
            ch, pi = divmod(i, n_pairs)
            g_end = gl_s[t, ch:ch + 1, pi * LANES:(pi + 1) * LANES]
            g_rows = jnp.sum(jnp.where(eye, jnp.broadcast_to(g_end, (2 * L, LANES)), 0.0),
                             axis=1, keepdims=True)
            gcol.append(jnp.where(m0, g_rows[:L], g_rows[L:]))
        for key in ("w1", "w2", "vv", "rb", "rkv", "rh"):
            d.pop(key)
        d.update(lhs=lhs, cm=cm, y0=y0, gcol=gcol)

    def scan(t, c):
        d = st[t]
        r0 = t * HT + c * L
        for pi in range(n_pairs):
            i = c * n_pairs + pi
            h = hs[pi]
            out = _mm(d["lhs"][i], block_diag(h))
            hs[pi] = d["gcol"][i] * h + out[:L] + d["cm"][i]
            y_s[r0:r0 + L, pi * LANES:(pi + 1) * LANES] = out[L:] + d["y0"][i]

    def finish(t):
        rows = rows_of(t)
        y = y_s[rows, :]
        inv_n = 1.0 / RWKV_HEAD_DIM
        mean = _head_sum(y, hsum_ref[...]) * inv_n
        yc = y - mean
        var = _head_sum(yc * yc, hsum_ref[...]) * inv_n
        yn = yc * lax.rsqrt(var + LNX_EPS) * lg_ref[...] + lb_ref[...]
        o_ref[rows, :] = ((yn + bonus_s[rows, :].astype(F32))
                          * gate_s[rows, :].astype(F32)).astype(o_ref.dtype)

    assert n_rounds >= n_chunks

    def carry_out():
        for pi in range(n_pairs):
            h_s[pi] = hs[pi]

    stages = []
    add = lambda f, *a: stages.append(functools.partial(f, *a))
    for t in range(n_tasks):
        prv = t - 1 if t >= 1 else None
        add(gram, t)
        for rnd in range(n_rounds):
            add(neumann, t)
            if prv is not None and rnd < n_chunks:
                add(scan, prv, rnd)
        if prv is not None:
            add(finish, prv)
        add(solve, t); add(fold, t)
    for c in range(n_chunks):
        add(scan, n_tasks - 1, c)
    add(finish, n_tasks - 1)
    add(carry_out)
    return stages


def _rwkv(features, decay_rows, seq_len, lnx_g, lnx_b):
    T, width = features[0].shape
    B, S = T // seq_len, seq_len
    n_pairs = width // LANES
    L, TL = SCAN_CHUNK, min(SCAN_TILE, S)
    assert S % TL == 0 and TL % SCAN_TASK == 0 and SCAN_TASK == ROW_TILE
    ch = jnp.arange(LANES) // RWKV_HEAD_DIM
    hsum = (ch[:, None] == ch[None, :]).astype(BF16)
    vec = lambda a: a.reshape(1, -1).astype(F32)
    tile = pl.BlockSpec((None, TL, width), lambda b, j: (b, j, 0))
    tasks = TL // SCAN_TASK
    return dict(
        kwargs=dict(n_pairs=n_pairs),
        operands=tuple(f.reshape(B, S, width) for f in features)
                 + (decay_rows.reshape(B, S // SCAN_TASK, 8, width), vec(lnx_g), vec(lnx_b), hsum),
        in_specs=[tile] * len(features)
                 + [pl.BlockSpec((None, tasks, 8, width), lambda b, j: (b, j, 0, 0)),
                    _const_spec((1, width)), _const_spec((1, width)), _const_spec((LANES, LANES))],
        out_spec=tile,
        out_shape=jax.ShapeDtypeStruct((B, S, width), BF16),
        scratch_shapes=[pltpu.VMEM((n_pairs, L, LANES), F32), pltpu.VMEM((TL, width), F32)],
        grid=(B, S // TL))


def _run_stages(stages_fn, *refs, **kwargs):
    for stage in stages_fn(*refs, **kwargs):
        stage()


def _mixer_call(call, stages_fn, semantics, name):
    return pl.pallas_call(
        functools.partial(_run_stages, stages_fn, **call["kwargs"]),
        grid=call["grid"],
        in_specs=call["in_specs"],
        out_specs=call["out_spec"],
        out_shape=call["out_shape"],
        scratch_shapes=call.get("scratch_shapes", ()),
        compiler_params=pltpu.CompilerParams(dimension_semantics=semantics,
                                             vmem_limit_bytes=VMEM_LIMIT_BYTES),
        name=name,
    )(*call["operands"])


def _post_kernel(x_ref, lng_ref, lnb_ref, ya_ref, yb_ref, ga_ref, gb_ref, wua_ref, wub_ref, wo_ref,
                 l1g_ref, l1b_ref, wg_ref, wu_ref, wd_ref, l2g_ref, l2b_ref, o_ref,
                 z_even, z_odd, *, ffn_cuts):
    i = pl.program_id(0)
    dot = functools.partial(jnp.dot, preferred_element_type=F32)

    @pl.when(i == 0)
    def _():
        z_even[...] = jnp.zeros_like(z_even)

    def ffn_part(h1b, lo, hi):
        gate = dot(h1b, wg_ref[:, lo:hi])
        up = dot(h1b, wu_ref[:, lo:hi])
        act = (gate * jax.nn.sigmoid(gate) * up).astype(BF16)
        return dot(act, wd_ref[lo:hi, :])

    def step(z_prev, z_next):
        up_a = dot(ya_ref[...], wua_ref[...])
        up_b = dot(yb_ref[...], wub_ref[...])
        h1 = _layer_norm(z_prev[...], l1g_ref[...], l1b_ref[...])
        h1b = h1.astype(BF16)
        ffn = ffn_part(h1b, ffn_cuts[0], ffn_cuts[1])
        merged = ga_ref[...].astype(F32) * up_a + gb_ref[...].astype(F32) * up_b
        for lo, hi in zip(ffn_cuts[1:-1], ffn_cuts[2:]):
            ffn = ffn + ffn_part(h1b, lo, hi)
        mix = dot(merged.astype(BF16), wo_ref[...])
        o_ref[...] = _layer_norm(DEEPNORM_ALPHA * h1 + ffn, l2g_ref[...], l2b_ref[...])
        h = _layer_norm(x_ref[...], lng_ref[...], lnb_ref[...])
        z_next[...] = DEEPNORM_ALPHA * h + mix

    @pl.when(i % 2 == 0)
    def _():
        step(z_even, z_odd)

    @pl.when(i % 2 == 1)
    def _():
        step(z_odd, z_even)


def _post(x2, ln_g, ln_b, ya, yb, ga, gb, w_up_a, w_up_b, w_out, ln1_g, ln1_b,
          w_gate, w_up, w_down, ln2_g, ln2_b):
    T, D = x2.shape
    hidden = w_gate.shape[1]
    assert hidden % MXU_DEPTH == 0
    ffn_cuts = (0, (hidden // MXU_DEPTH // 2) * MXU_DEPTH, hidden)
    tm = ROW_TILE
    n_tiles = T // tm
    row = lambda n: pl.BlockSpec((tm, n), lambda i: (jnp.minimum(i, n_tiles - 1), 0))
    vec = lambda a: a.reshape(1, D)
    cs = lambda a: _const_spec(a.shape)
    wts = [w.astype(BF16) for w in (w_up_a, w_up_b, w_out, w_gate, w_up, w_down)]
    wua, wub, wo, wg, wu, wd = wts
    return pl.pallas_call(
        functools.partial(_post_kernel, ffn_cuts=ffn_cuts),
        grid=(n_tiles + 1,),
        in_specs=[row(D), _const_spec((1, D)), _const_spec((1, D)), row(ya.shape[1]), row(yb.shape[1]),
                  row(D), row(D), cs(wua), cs(wub), cs(wo), _const_spec((1, D)), _const_spec((1, D)),
                  cs(wg), cs(wu), cs(wd), _const_spec((1, D)), _const_spec((1, D))],
        out_specs=pl.BlockSpec((tm, D), lambda i: (jnp.maximum(i - 1, 0), 0)),
        out_shape=jax.ShapeDtypeStruct((T, D), F32),
        scratch_shapes=[pltpu.VMEM((tm, D), F32), pltpu.VMEM((tm, D), F32)],
        compiler_params=pltpu.CompilerParams(dimension_semantics=("arbitrary",),
                                             vmem_limit_bytes=VMEM_LIMIT_BYTES),
        name="merge_ffn",
    )(x2, vec(ln_g), vec(ln_b), ya, yb, ga, gb, wua, wub, wo, vec(ln1_g), vec(ln1_b),
      wg, wu, wd, vec(ln2_g), vec(ln2_b))


def kernel(x, ln_in_g, ln_in_b, rel_bias, w_in, diff_lam_q1, diff_lam_k1, diff_lam_q2, diff_lam_k2, diff_subln_g, rwkv_mu, rwkv_w0, rwkv_w2, rwkv_a0, rwkv_a2, rwkv_g2, rwkv_k_k, rwkv_k_a, rwkv_r_k, rwkv_lnx_g, rwkv_lnx_b, w_up_a, w_up_b, w_out, ln1_g, ln1_b, ffn_w_gate, ffn_w_up, ffn_w_down, ln2_g, ln2_b):
    B, S, D = x.shape
    T = B * S
    width = rwkv_w0.shape[-1]
    n_qk = DIFF_HEADS * HEAD_PAIR
    n_rw = 3 * width + DECAY_RANK + ICLR_RANK + GATE_RANK
    sizes = (n_qk, n_qk, n_qk, n_rw, D, D)
    assert w_in.shape[0] == 1 and w_in.shape[2] == sum(sizes), "single-layer model expected"
    x2 = x.reshape(T, D)
    q, k, v, ga, gb, *rwkv_features, decay_rows = _inproj(
        x2, S, ln_in_g, ln_in_b, w_in[0].astype(BF16), sizes, rwkv_mu[0], rwkv_w0[0], rwkv_w2[0],
        rwkv_a0[0], rwkv_a2[0], rwkv_g2[0], rwkv_k_k[0], rwkv_k_a[0], rwkv_r_k[0])
    lam_vecs = jnp.concatenate([diff_lam_q1, diff_lam_k1, diff_lam_q2, diff_lam_k2], 0).astype(F32)
    ya = _mixer_call(
        _attention(q.reshape(B, S, n_qk), k.reshape(B, S, n_qk), v.reshape(B, S, n_qk),
                   rel_bias, lam_vecs, diff_subln_g[0]),
        _attn_stages, ("parallel", "parallel"), "diff_attention")
    yb = _mixer_call(_rwkv(rwkv_features, decay_rows, S, rwkv_lnx_g[0], rwkv_lnx_b[0]),
                     _rwkv_stages, ("parallel", "arbitrary"), "rwkv7_chunked")
    out = _post(x2, ln_in_g, ln_in_b, ya.reshape(T, n_qk), yb.reshape(T, width), ga, gb,
                w_up_a[0], w_up_b[0], w_out[0], ln1_g[0], ln1_b[0],
                ffn_w_gate[0], ffn_w_up[0], ffn_w_down[0], ln2_g[0], ln2_b[0])
    return out.reshape(B, S, D)
```

```python
import functools
import math

import jax
import jax.numpy as jnp
from jax import lax
from jax.experimental import pallas as pl
from jax.experimental.pallas import tpu as pltpu

F32 = jnp.float32
BF16 = jnp.bfloat16

DIFF_HEADS = 4
DIFF_HEAD_DIM = 64
HEAD_PAIR = 2 * DIFF_HEAD_DIM
RWKV_HEAD_DIM = 64
DECAY_RANK = 64
ICLR_RANK = 64
GATE_RANK = 128
NUM_BUCKETS = 32
MAX_DISTANCE = 128
LN_EPS = 1e-5
LNX_EPS = 64e-5
NEG_BIG = -1e30
DEPTH = 1
DEEPNORM_ALPHA = (2.0 * DEPTH) ** 0.25
LAMBDA_INIT = 0.8 - 0.6 * math.exp(-0.3 * 0)
LOG2_E = math.log2(math.e)

LANES = 128
SUBLANES = 8
MXU_DEPTH = 256
VMEM_LIMIT_BYTES = 56 * 1024 * 1024
ROW_TILE = 256
ATTN_TILE = 256
SCAN_CHUNK = 64
SCAN_TASK = 256
SCAN_TILE = 512


def _mm(a, b):
    return jnp.dot(a.astype(BF16), b.astype(BF16), preferred_element_type=F32)


def _mm_nt(a, b):
    return lax.dot_general(a.astype(BF16), b.astype(BF16), (((1,), (1,)), ((), ())),
                           preferred_element_type=F32)


def _mm_tn(a, b):
    return lax.dot_general(a.astype(BF16), b.astype(BF16), (((0,), (0,)), ((), ())),
                           preferred_element_type=F32)


def _split2(x):
    hi = x.astype(BF16)
    lo = (x - hi.astype(F32)).astype(BF16)
    return hi, lo


def _sel_mm(sel, x):
    hi, lo = _split2(x)
    d = functools.partial(jnp.dot, preferred_element_type=F32)
    return d(sel, hi) + d(sel, lo)


def _head_sum(x, sel):
    xb = x.astype(BF16)
    return jnp.concatenate(
        [jnp.dot(xb[:, c0:c0 + LANES], sel, preferred_element_type=F32)
         for c0 in range(0, x.shape[1], LANES)], axis=1)


def _layer_norm(x, g, b, eps=LN_EPS):
    mu = jnp.mean(x, -1, keepdims=True)
    xc = x - mu
    var = jnp.mean(xc * xc, -1, keepdims=True)
    return xc * lax.rsqrt(var + eps) * g + b


def _const_spec(shape):
    return pl.BlockSpec(shape, lambda *_: (0,) * len(shape), pipeline_mode=pl.Buffered(1))


def _rwkv_feature_stages(p, before, mu, w0, w2, a0, a2, g2, k_k, k_a, r_k, tri, hsum, *, width):
    L = SCAN_CHUNK
    n_chunks = p.shape[0] // L
    d, out = {}, {}

    def shift():
        row = lax.broadcasted_iota(jnp.int32, (p.shape[0], 1), 0)
        p_prev = jnp.where(row == 0, before, pltpu.roll(p, 1, 0))
        ps = p + (p_prev - p) * mu
        d["r"], d["k"], out["v"] = ps[:, 0:width], ps[:, width:2 * width], ps[:, 2 * width:3 * width]
        lr = ps[:, 3 * width:3 * width + DECAY_RANK + ICLR_RANK]
        gl = ps[:, 3 * width + DECAY_RANK + ICLR_RANK:]
        d["lr"], d["tanh_lr"], d["sig_gl"] = lr.astype(BF16), jnp.tanh(lr).astype(BF16), jax.nn.sigmoid(gl)
        d["kk"] = d["k"] * k_k

    def project():
        d["z"] = w0 + _mm(d.pop("tanh_lr"), w2)
        d["ai"] = a0 + _mm(d.pop("lr"), a2)
        out["gate"] = _mm(d.pop("sig_gl"), g2)
        d["ss"] = _head_sum(d["kk"] * d["kk"], hsum)

    def rates():
        d["logw"] = (-math.exp(-0.5) * LOG2_E) * jax.nn.sigmoid(d.pop("z"))
        iclr = jax.nn.sigmoid(d.pop("ai"))
        d["kk"] = d["kk"] * lax.rsqrt(jnp.maximum(d.pop("ss"), 1e-24))
        d["b"] = d["kk"] * iclr
        d["k2"] = d.pop("k") * (1.0 + (iclr - 1.0) * k_a)
        d["rk"] = d["r"] * d["k2"] * r_k

    def sums():
        out["bonus"] = _head_sum(d.pop("rk"), hsum) * out["v"]
        d["c"] = _sel_mm(tri, d["logw"])

    def scale():
        c = d.pop("c")
        c_last = [c[i * L + L - 1:i * L + L, :] for i in range(n_chunks)]
        c_end = jnp.concatenate([jnp.broadcast_to(x, (L, width)) for x in c_last], 0)
        inv_g = jnp.exp2(-c)
        to_end = jnp.exp2(c_end - c)
        b, k2 = d.pop("b"), d.pop("k2")
        out.update(ah=-d.pop("kk") * jnp.exp2(c - d.pop("logw")), rh=d.pop("r") * jnp.exp2(c),
                   bt=b * inv_g, kt=k2 * inv_g, bb=b * to_end, kb=k2 * to_end,
                   gl=[jnp.exp2(x) for x in c_last])

    return (shift, project, rates, sums, scale), out


RWKV_FEATURES = ("ah", "rh", "bt", "kt", "bb", "kb", "v", "bonus", "gate")


def _inproj_kernel(x_ref, g_ref, b_ref, w_ref, mu_ref, w0_ref, w2_ref, a0_ref, a2_ref, g2_ref,
                   kk_ref, ka_ref, rk_ref, tri_ref, hsum_ref,
                   q_ref, k_ref, v_ref, ga_ref, gb_ref, *rest, splits, tiles_per_seq):
    feat_refs = dict(zip(RWKV_FEATURES, rest[:len(RWKV_FEATURES)]))
    gl_ref, h_even, h_odd, last_s = rest[len(RWKV_FEATURES):]
    i = pl.program_id(0)
    width = w0_ref.shape[1]

    @pl.when(i == 0)
    def _():
        h_even[...] = jnp.zeros_like(h_even)
        last_s[...] = jnp.zeros_like(last_s)

    def step(h_prev, h_next):
        h_next[...] = _layer_norm(x_ref[...], g_ref[...], b_ref[...]).astype(BF16)

        def proj(lo, hi):
            return jnp.dot(h_prev[...], w_ref[:, lo:hi], preferred_element_type=F32)

        s_q, s_k, s_v, s_rw, s_ga = splits
        rw = proj(s_v, s_rw)
        seq_start = ((i - 1) % tiles_per_seq) == 0
        before = jnp.where(seq_start, 0.0, last_s[0:1, :])
        last_s[0:1, :] = rw[rw.shape[0] - 1:, :]
        (shift, project, rates, sums, scale), f = _rwkv_feature_stages(
            rw, before, mu_ref[...], w0_ref[...], w2_ref[...], a0_ref[...], a2_ref[...],
            g2_ref[...], kk_ref[...], ka_ref[...], rk_ref[...], tri_ref[...], hsum_ref[...],
            width=width)
        shift()
        ga_ref[...] = jax.nn.sigmoid(proj(s_rw, s_ga)).astype(BF16)
        project()
        gb_ref[...] = jax.nn.sigmoid(proj(s_ga, w_ref.shape[1])).astype(BF16)
        rates()
        sums()
        q_ref[...] = (proj(0, s_q) * (DIFF_HEAD_DIM ** -0.5 * LOG2_E)).astype(BF16)
        scale()
        for name, ref in feat_refs.items():
            ref[...] = f[name].astype(ref.dtype)
        gl_ref[...] = jnp.zeros_like(gl_ref)
        for c, x in enumerate(f["gl"]):
            gl_ref[c:c + 1, :] = x
        k_ref[...] = proj(s_q, s_k).astype(BF16)
        v_ref[...] = proj(s_k, s_v).astype(BF16)

    @pl.when(i % 2 == 0)
    def _():
        step(h_even, h_odd)

    @pl.when(i % 2 == 1)
    def _():
        step(h_odd, h_even)


def _inproj(x2, seq_len, ln_g, ln_b, w_in_bf, sizes, mu, w0, w2, a0, a2, g2, k_k, k_a, r_k):
    T, D = x2.shape
    n_q, n_k, n_v, n_rw, n_ga, n_gb = sizes
    width = w0.shape[-1]
    acc, cuts = 0, []
    for s in sizes[:-1]:
        acc += s
        cuts.append(acc)
    splits = tuple(cuts)
    tm, L = ROW_TILE, SCAN_CHUNK
    n_tiles = T // tm
    assert seq_len % tm == 0 and tm % L == 0 and tm // L <= SUBLANES
    assert n_rw == 3 * width + DECAY_RANK + ICLR_RANK + GATE_RANK
    assert DECAY_RANK + ICLR_RANK == LANES
    ti = jnp.arange(tm)
    same_chunk = (ti[:, None] // L) == (ti[None, :] // L)
    tri = (same_chunk & (ti[None, :] <= ti[:, None])).astype(BF16)
    ch = jnp.arange(LANES) // RWKV_HEAD_DIM
    hsum = (ch[:, None] == ch[None, :]).astype(BF16)
    w2p = jnp.concatenate([w2, jnp.zeros((ICLR_RANK, width), w2.dtype)], 0).astype(BF16)
    a2p = jnp.concatenate([jnp.zeros((DECAY_RANK, width), a2.dtype), a2], 0).astype(BF16)
    vec = lambda a: a.reshape(1, -1).astype(F32)
    row_in = pl.BlockSpec((tm, D), lambda i: (jnp.minimum(i, n_tiles - 1), 0))
    row = lambda n: pl.BlockSpec((tm, n), lambda i: (jnp.maximum(i - 1, 0), 0))
    n_feat = len(RWKV_FEATURES)
    return pl.pallas_call(
        functools.partial(_inproj_kernel, splits=splits, tiles_per_seq=seq_len // tm),
        grid=(n_tiles + 1,),
        in_specs=[row_in, _const_spec((1, D)), _const_spec((1, D)), _const_spec(w_in_bf.shape),
                  _const_spec((1, n_rw)), _const_spec((1, width)), _const_spec((LANES, width)),
                  _const_spec((1, width)), _const_spec((LANES, width)), _const_spec((GATE_RANK, width)),
                  _const_spec((1, width)), _const_spec((1, width)), _const_spec((1, width)),
                  _const_spec((tm, tm)), _const_spec((LANES, LANES))],
        out_specs=[row(n_q), row(n_k), row(n_v), row(n_ga), row(n_gb)] + [row(width)] * n_feat
                  + [pl.BlockSpec((None, SUBLANES, width), lambda i: (jnp.maximum(i - 1, 0), 0, 0))],
        out_shape=[jax.ShapeDtypeStruct((T, n_q), BF16), jax.ShapeDtypeStruct((T, n_k), BF16),
                   jax.ShapeDtypeStruct((T, n_v), BF16), jax.ShapeDtypeStruct((T, n_ga), BF16),
                   jax.ShapeDtypeStruct((T, n_gb), BF16)]
                  + [jax.ShapeDtypeStruct((T, width), BF16)] * n_feat
                  + [jax.ShapeDtypeStruct((n_tiles, SUBLANES, width), F32)],
        scratch_shapes=[pltpu.VMEM((tm, D), BF16), pltpu.VMEM((tm, D), BF16),
                        pltpu.VMEM((SUBLANES, n_rw), F32)],
        compiler_params=pltpu.CompilerParams(dimension_semantics=("arbitrary",),
                                             vmem_limit_bytes=VMEM_LIMIT_BYTES),
        name="ln_inproj",
    )(x2, ln_g.reshape(1, D), ln_b.reshape(1, D), w_in_bf, vec(mu), vec(w0), w2p, vec(a0), a2p,
      g2.astype(BF16), vec(k_k), vec(k_a), vec(r_k), tri, hsum)


def _attn_stages(q_ref, k_ref, v_ref, bdiag_ref, bnear_ref, bfar_ref, lam_ref, g_ref, o_ref,
                 v1_s, *, nq, t):
    lane = lax.broadcasted_iota(jnp.int32, (t, HEAD_PAIR), 1)
    lo = lane < DIFF_HEAD_DIM
    lv = lam_ref[...]
    lam = (jnp.exp(jnp.sum(lv[0:1] * lv[1:2], keepdims=True))
           - jnp.exp(jnp.sum(lv[2:3] * lv[3:4], keepdims=True)) + LAMBDA_INIT)
    bfar = bfar_ref[...][:, 0:1]
    zero = jnp.zeros((), BF16)
    rowmax = lambda a: jnp.max(a, -1, keepdims=True)
    v1_s[:, :HEAD_PAIR] = v_ref[...]
    v1_s[:, HEAD_PAIR:] = jnp.ones((v_ref.shape[0], HEAD_PAIR), BF16)

    def logits(qi):
        q = q_ref[qi * t:(qi + 1) * t, :]
        qs = jnp.concatenate([jnp.where(lo, q, zero), jnp.where(lo, zero, q)], axis=0)
        return lax.dot_general(qs, k_ref[0:(qi + 1) * t, :], (((1,), (1,)), ((), ())),
                               preferred_element_type=F32)

    def softmax(qi, s):
        s_diag = s[:, qi * t:] + bdiag_ref[...]
        m = rowmax(s_diag)
        if qi >= 1:
            s_near = s[:, (qi - 1) * t:qi * t] + bnear_ref[...]
            m = jnp.maximum(m, rowmax(s_near))
        if qi >= 2:
            s_far = s[:, :(qi - 1) * t]
            m = jnp.maximum(m, rowmax(s_far) + bfar)
        parts = []
        if qi >= 2:
            parts.append(jnp.exp2(s_far - (m - bfar)))
        if qi >= 1:
            parts.append(jnp.exp2(s_near - m))
        parts.append(jnp.exp2(s_diag - m))
        return jnp.concatenate([x.astype(BF16) for x in parts], axis=1)

    def values(qi, p):
        acc = jnp.dot(p, v1_s[0:(qi + 1) * t, :], preferred_element_type=F32)
        l = acc[:, HEAD_PAIR:HEAD_PAIR + 1]
        acc = acc[:, :HEAD_PAIR]
        o = acc[:t] * (1.0 / l[:t]) - acc[t:] * (lam / l[t:])
        ms = jnp.mean(o * o, -1, keepdims=True)
        o = o * lax.rsqrt(ms + LN_EPS) * g_ref[...] * (1.0 - LAMBDA_INIT)
        o_ref[qi * t:(qi + 1) * t, :] = o.astype(o_ref.dtype)

    s_q, p_q = {}, {}

    def do_logits(qi):
        s_q[qi] = logits(qi)

    def do_softmax(qi):
        p_q[qi] = softmax(qi, s_q.pop(qi))

    def do_values(qi):
        values(qi, p_q.pop(qi))

    stages = []
    for step in range(nq + 2):
        if step < nq:
            stages.append(functools.partial(do_logits, step))
        if 0 <= step - 1 < nq:
            stages.append(functools.partial(do_softmax, step - 1))
        if 0 <= step - 2 < nq:
            stages.append(functools.partial(do_values, step - 2))
    return stages


def _t5_bucket(dist):
    max_exact = NUM_BUCKETS // 2
    d = jnp.maximum(dist, 1).astype(F32)
    large = max_exact + (jnp.log(d / max_exact) / math.log(MAX_DISTANCE / max_exact)
                         * (NUM_BUCKETS - max_exact)).astype(jnp.int32)
    large = jnp.minimum(large, NUM_BUCKETS - 1)
    return jnp.where(dist < max_exact, dist, large)


def _attention(q, k, v, rel_bias, lam_vecs, subln_g):
    B, S, _ = q.shape
    t = ATTN_TILE
    assert S % t == 0 and t >= MAX_DISTANCE
    nq = S // t
    by_dist = rel_bias.astype(F32)[_t5_bucket(jnp.arange(2 * t, dtype=jnp.int32))].T * LOG2_E
    dd = jnp.arange(t, dtype=jnp.int32)[:, None] - jnp.arange(t, dtype=jnp.int32)[None, :]

    def toeplitz(f_neg, f_pos):
        period = jnp.concatenate([f_neg, jnp.zeros_like(f_neg[:, :1]), f_pos[:, ::-1]], axis=1)
        rows = jnp.tile(period, (1, t))[:, :t * (2 * t - 1)].reshape(-1, t, 2 * t - 1)
        return rows[:, :, :t]

    bdiag = jnp.where(dd >= 0, toeplitz(jnp.broadcast_to(by_dist[:, :1], (DIFF_HEADS, t)),
                                        by_dist[:, 1:t]), NEG_BIG)
    bnear = toeplitz(by_dist[:, t:0:-1], by_dist[:, t + 1:])
    bdiag = jnp.concatenate([bdiag, bdiag], axis=1)
    bnear = jnp.concatenate([bnear, bnear], axis=1)
    bfar = jnp.broadcast_to((rel_bias[NUM_BUCKETS - 1] * LOG2_E)[:, None, None],
                            (DIFF_HEADS, 1, LANES))
    seq = pl.BlockSpec((None, S, HEAD_PAIR), lambda b, h: (b, 0, h))
    per_head = lambda r, c: pl.BlockSpec((None, r, c), lambda b, h: (h, 0, 0))
    return dict(
        kwargs=dict(nq=nq, t=t),
        operands=(q, k, v, bdiag, bnear, bfar, lam_vecs, subln_g.reshape(1, HEAD_PAIR)),
        in_specs=[seq, seq, seq, per_head(2 * t, t), per_head(2 * t, t), per_head(1, LANES),
                  pl.BlockSpec((4, DIFF_HEAD_DIM), lambda b, h: (0, 0)),
                  pl.BlockSpec((1, HEAD_PAIR), lambda b, h: (0, 0))],
        out_spec=seq,
        out_shape=jax.ShapeDtypeStruct((B, S, DIFF_HEADS * HEAD_PAIR), BF16),
        scratch_shapes=[pltpu.VMEM((S, 2 * HEAD_PAIR), BF16)],
        grid=(B, DIFF_HEADS))


def _rwkv_stages(ah_s, rh_s, bt_s, kt_s, bb_s, kb_s, v_s, bonus_s, gate_s, gl_s,
                 lg_ref, lb_ref, hsum_ref, o_ref, h_s, y_s, *, n_pairs):
    L = SCAN_CHUNK
    TL = ah_s.shape[0]

    @pl.when(pl.program_id(1) == 0)
    def _():
        h_s[...] = jnp.zeros_like(h_s)

    HT = SCAN_TASK
    n_tasks = TL // HT
    n_rounds = int(math.log2(L))
    n_chunks = HT // L
    n_blocks = n_chunks * n_pairs

    lane = lax.broadcasted_iota(jnp.int32, (L, LANES), 1)
    m0 = lane < RWKV_HEAD_DIM
    m1 = jnp.logical_not(m0)
    ri = lax.broadcasted_iota(jnp.int32, (2 * L, LANES), 0)
    ci = lax.broadcasted_iota(jnp.int32, (2 * L, LANES), 1)
    eye = ri == ci
    rl = lax.broadcasted_iota(jnp.int32, (L, LANES), 0)
    tril2 = (lane % L) <= rl
    stril2 = (lane % L) < rl
    eye2_f = jnp.where((lane % L) == rl, 1.0, 0.0).astype(F32)
    zeros = jnp.zeros((L, LANES), F32)

    sel = lambda m, x: jnp.where(m, x, jnp.zeros_like(x))
    cat0 = lambda *xs: jnp.concatenate(xs, 0)
    cat1 = lambda *xs: jnp.concatenate(xs, 1)
    block_diag = lambda x: cat0(sel(m0, x), sel(m1, x))
    swap_halves = lambda x: pltpu.roll(x, RWKV_HEAD_DIM, 1)
    st = [dict() for _ in range(n_tasks)]
    hs = [h_s[pi] for pi in range(n_pairs)]

    def rows_of(t):
        return slice(t * HT, (t + 1) * HT)

    def block_ld(ref, t, i):
        c, pi = divmod(i, n_pairs)
        r0 = t * HT + c * L
        return ref[r0:r0 + L, pi * LANES:(pi + 1) * LANES]

    def gram(t):
        d = st[t]
        ah = [block_ld(ah_s, t, i) for i in range(n_blocks)]
        rh = [block_ld(rh_s, t, i) for i in range(n_blocks)]
        bt = [block_ld(bt_s, t, i) for i in range(n_blocks)]
        kt = [block_ld(kt_s, t, i) for i in range(n_blocks)]
        d["ah0"] = [sel(m0, x) for x in ah]
        d["ah1"] = [sel(m1, x) for x in ah]
        d["rh"] = rh
        g = [_mm_nt(cat0(d["ah0"][i], sel(m0, rh[i]), d["ah1"][i], sel(m1, rh[i])), cat0(bt[i], kt[i]))
             for i in range(n_blocks)]
        pair = lambda x0, x1: jnp.where(m0, x0, swap_halves(x1))
        pair_r = lambda x0, x1: jnp.where(m0, swap_halves(x0), x1)
        d["ak"] = [sel(stril2, pair_r(x[:L], x[2 * L:3 * L])) for x in g]
        d["rb"] = [sel(tril2, pair(x[L:2 * L], x[3 * L:])) for x in g]
        d["rk"] = [sel(tril2, pair_r(x[L:2 * L], x[3 * L:])) for x in g]
        d["pw"] = [sel(stril2, pair(x[:L], x[2 * L:3 * L])) for x in g]
        d["tinv"] = [eye2_f + x for x in d["pw"]]
        d["round"] = 0

    def neumann(t):
        d = st[t]
        first, last = d["round"] == 0, d["round"] == n_rounds - 1
        d["round"] += 1
        if first:
            d["pw"] = [_mm(x, block_diag(x)) for x in d["pw"]]
            return
        out = [_mm(tm if last else cat0(tm, x), block_diag(x)) for x, tm in zip(d["pw"], d["tinv"])]
        d["tinv"] = [tm + o[:L] for tm, o in zip(d["tinv"], out)]
        if not last:
            d["pw"] = [o[L:] for o in out]

    def solve(t):
        d = st[t]
        d.pop("pw")
        vv = [block_ld(v_s, t, i) for i in range(n_blocks)]
        d["vv"] = vv
        kv = [_mm(cat0(ak, rk), block_diag(x)) for ak, rk, x in zip(d.pop("ak"), d.pop("rk"), vv)]
        d["rkv"] = [x[L:] for x in kv]
        w = [_mm(tm, cat1(cat0(a0_, a1_), block_diag(x[:L])))
             for tm, a0_, a1_, x in zip(d.pop("tinv"), d.pop("ah0"), d.pop("ah1"), kv)]
        d["w1"] = [x[:, :LANES] for x in w]
        d["w2"] = [x[:, LANES:] for x in w]

    def fold(t):
        d = st[t]
        lhs, cm, y0, gcol = [], [], [], []
        for i in range(n_blocks):
            w1, w2, vv = d["w1"][i], d["w2"][i], d["vv"][i]
            ac = _mm_tn(cat0(block_ld(bb_s, t, i), block_ld(kb_s, t, i)),
                        cat1(cat0(w1, zeros), cat0(w2, vv)))
            ry0 = _mm(d["rb"][i], cat1(block_diag(w1), block_diag(w2)))
            a_mat = jnp.where(m0, ac[:L, :LANES], ac[L:, :LANES])
            cm.append(jnp.where(m0, ac[:L, LANES:], ac[L:, LANES:]))
            lhs.append(cat0(a_mat, d["rh"][i] + ry0[:, :LANES]).astype(BF16))
            y0.append(ry0[:, LANES:] + d["rkv"][i])
            ch, pi = divmod(i, n_pairs)
            g_end = gl_s[t, ch:ch + 1, pi * LANES:(pi + 1) * LANES]
            g_rows = jnp.sum(jnp.where(eye, jnp.broadcast_to(g_end, (2 * L, LANES)), 0.0),
                             axis=1, keepdims=True)
            gcol.append(jnp.where(m0, g_rows[:L], g_rows[L:]))
        for key in ("w1", "w2", "vv", "rb", "rkv", "rh"):
            d.pop(key)
        d.update(lhs=lhs, cm=cm, y0=y0, gcol=gcol)

    def scan(t, c):
        d = st[t]
        r0 = t * HT + c * L
        for pi in range(n_pairs):
            i = c * n_pairs + pi
            h = hs[pi]
            out = _mm(d["lhs"][i], block_diag(h))
            hs[pi] = d["gcol"][i] * h + out[:L] + d["cm"][i]
            y_s[r0:r0 + L, pi * LANES:(pi + 1) * LANES] = out[L:] + d["y0"][i]

    def finish(t):
        rows = rows_of(t)
        y = y_s[rows, :]
        inv_n = 1.0 / RWKV_HEAD_DIM
        mean = _head_sum(y, hsum_ref[...]) * inv_n
        yc = y - mean
        var = _head_sum(yc * yc, hsum_ref[...]) * inv_n
        yn = yc * lax.rsqrt(var + LNX_EPS) * lg_ref[...] + lb_ref[...]
        o_ref[rows, :] = ((yn + bonus_s[rows, :].astype(F32))
                          * gate_s[rows, :].astype(F32)).astype(o_ref.dtype)

    assert n_rounds >= n_chunks

    def carry_out():
        for pi in range(n_pairs):
            h_s[pi] = hs[pi]

    stages = []
    add = lambda f, *a: stages.append(functools.partial(f, *a))
    for t in range(n_tasks):
        prv = t - 1 if t >= 1 else None
        add(gram, t)
        for rnd in range(n_rounds):
            add(neumann, t)
            if prv is not None and rnd < n_chunks:
                add(scan, prv, rnd)
        if prv is not None:
            add(finish, prv)
        add(solve, t); add(fold, t)
    for c in range(n_chunks):
        add(scan, n_tasks - 1, c)
    add(finish, n_tasks - 1)
    add(carry_out)
    return stages


def _rwkv(features, decay_rows, seq_len, lnx_g, lnx_b):
    T, width = features[0].shape
    B, S = T // seq_len, seq_len
    n_pairs = width // LANES
    L, TL = SCAN_CHUNK, min(SCAN_TILE, S)
    assert S % TL == 0 and TL % SCAN_TASK == 0 and SCAN_TASK == ROW_TILE
    ch = jnp.arange(LANES) // RWKV_HEAD_DIM
    hsum = (ch[:, None] == ch[None, :]).astype(BF16)
    vec = lambda a: a.reshape(1, -1).astype(F32)
    tile = pl.BlockSpec((None, TL, width), lambda b, j: (b, j, 0))
    tasks = TL // SCAN_TASK
    return dict(
        kwargs=dict(n_pairs=n_pairs),
        operands=tuple(f.reshape(B, S, width) for f in features)
                 + (decay_rows.reshape(B, S // SCAN_TASK, SUBLANES, width), vec(lnx_g), vec(lnx_b), hsum),
        in_specs=[tile] * len(features)
                 + [pl.BlockSpec((None, tasks, SUBLANES, width), lambda b, j: (b, j, 0, 0)),
                    _const_spec((1, width)), _const_spec((1, width)), _const_spec((LANES, LANES))],
        out_spec=tile,
        out_shape=jax.ShapeDtypeStruct((B, S, width), BF16),
        scratch_shapes=[pltpu.VMEM((n_pairs, L, LANES), F32), pltpu.VMEM((TL, width), F32)],
        grid=(B, S // TL))


def _run_stages(stages_fn, *refs, **kwargs):
    for stage in stages_fn(*refs, **kwargs):
        stage()


def _mixer_call(call, stages_fn, semantics, name):
    return pl.pallas_call(
        functools.partial(_run_stages, stages_fn, **call["kwargs"]),
        grid=call["grid"],
        in_specs=call["in_specs"],
        out_specs=call["out_spec"],
        out_shape=call["out_shape"],
        scratch_shapes=call.get("scratch_shapes", ()),
        compiler_params=pltpu.CompilerParams(dimension_semantics=semantics,
                                             vmem_limit_bytes=VMEM_LIMIT_BYTES),
        name=name,
    )(*call["operands"])


def _post_kernel(x_ref, lng_ref, lnb_ref, ya_ref, yb_ref, ga_ref, gb_ref, wua_ref, wub_ref, wo_ref,
                 l1g_ref, l1b_ref, wg_ref, wu_ref, wd_ref, l2g_ref, l2b_ref, o_ref,
                 z_even, z_odd, *, ffn_cuts):
    i = pl.program_id(0)
    dot = functools.partial(jnp.dot, preferred_element_type=F32)

    @pl.when(i == 0)
    def _():
        z_even[...] = jnp.zeros_like(z_even)

    def ffn_part(h1b, lo, hi):
        gate = dot(h1b, wg_ref[:, lo:hi])
        up = dot(h1b, wu_ref[:, lo:hi])
        act = (gate * jax.nn.sigmoid(gate) * up).astype(BF16)
        return dot(act, wd_ref[lo:hi, :])

    def step(z_prev, z_next):
        up_a = dot(ya_ref[...], wua_ref[...])
        up_b = dot(yb_ref[...], wub_ref[...])
        h1 = _layer_norm(z_prev[...], l1g_ref[...], l1b_ref[...])
        h1b = h1.astype(BF16)
        ffn = ffn_part(h1b, ffn_cuts[0], ffn_cuts[1])
        merged = ga_ref[...].astype(F32) * up_a + gb_ref[...].astype(F32) * up_b
        for lo, hi in zip(ffn_cuts[1:-1], ffn_cuts[2:]):
            ffn = ffn + ffn_part(h1b, lo, hi)
        mix = dot(merged.astype(BF16), wo_ref[...])
        o_ref[...] = _layer_norm(DEEPNORM_ALPHA * h1 + ffn, l2g_ref[...], l2b_ref[...])
        h = _layer_norm(x_ref[...], lng_ref[...], lnb_ref[...])
        z_next[...] = DEEPNORM_ALPHA * h + mix

    @pl.when(i % 2 == 0)
    def _():
        step(z_even, z_odd)

    @pl.when(i % 2 == 1)
    def _():
        step(z_odd, z_even)


def _post(x2, ln_g, ln_b, ya, yb, ga, gb, w_up_a, w_up_b, w_out, ln1_g, ln1_b,
          w_gate, w_up, w_down, ln2_g, ln2_b):
    T, D = x2.shape
    hidden = w_gate.shape[1]
    assert hidden % MXU_DEPTH == 0
    ffn_cuts = (0, (hidden // MXU_DEPTH // 2) * MXU_DEPTH, hidden)
    tm = ROW_TILE
    n_tiles = T // tm
    row = lambda n: pl.BlockSpec((tm, n), lambda i: (jnp.minimum(i, n_tiles - 1), 0))
    vec = lambda a: a.reshape(1, D)
    cs = lambda a: _const_spec(a.shape)
    wts = [w.astype(BF16) for w in (w_up_a, w_up_b, w_out, w_gate, w_up, w_down)]
    wua, wub, wo, wg, wu, wd = wts
    return pl.pallas_call(
        functools.partial(_post_kernel, ffn_cuts=ffn_cuts),
        grid=(n_tiles + 1,),
        in_specs=[row(D), _const_spec((1, D)), _const_spec((1, D)), row(ya.shape[1]), row(yb.shape[1]),
                  row(D), row(D), cs(wua), cs(wub), cs(wo), _const_spec((1, D)), _const_spec((1, D)),
                  cs(wg), cs(wu), cs(wd), _const_spec((1, D)), _const_spec((1, D))],
        out_specs=pl.BlockSpec((tm, D), lambda i: (jnp.maximum(i - 1, 0), 0)),
        out_shape=jax.ShapeDtypeStruct((T, D), F32),
        scratch_shapes=[pltpu.VMEM((tm, D), F32), pltpu.VMEM((tm, D), F32)],
        compiler_params=pltpu.CompilerParams(dimension_semantics=("arbitrary",),
                                             vmem_limit_bytes=VMEM_LIMIT_BYTES),
        name="merge_ffn",
    )(x2, vec(ln_g), vec(ln_b), ya, yb, ga, gb, wua, wub, wo, vec(ln1_g), vec(ln1_b),
      wg, wu, wd, vec(ln2_g), vec(ln2_b))


def kernel(x, ln_in_g, ln_in_b, rel_bias, w_in, diff_lam_q1, diff_lam_k1, diff_lam_q2, diff_lam_k2, diff_subln_g, rwkv_mu, rwkv_w0, rwkv_w2, rwkv_a0, rwkv_a2, rwkv_g2, rwkv_k_k, rwkv_k_a, rwkv_r_k, rwkv_lnx_g, rwkv_lnx_b, w_up_a, w_up_b, w_out, ln1_g, ln1_b, ffn_w_gate, ffn_w_up, ffn_w_down, ln2_g, ln2_b):
    B, S, D = x.shape
    T = B * S
    width = rwkv_w0.shape[-1]
    n_qk = DIFF_HEADS * HEAD_PAIR
    n_rw = 3 * width + DECAY_RANK + ICLR_RANK + GATE_RANK
    sizes = (n_qk, n_qk, n_qk, n_rw, D, D)
    assert w_in.shape[0] == 1 and w_in.shape[2] == sum(sizes), "single-layer model expected"
    x2 = x.reshape(T, D)
    q, k, v, ga, gb, *rwkv_features, decay_rows = _inproj(
        x2, S, ln_in_g, ln_in_b, w_in[0].astype(BF16), sizes, rwkv_mu[0], rwkv_w0[0], rwkv_w2[0],
        rwkv_a0[0], rwkv_a2[0], rwkv_g2[0], rwkv_k_k[0], rwkv_k_a[0], rwkv_r_k[0])
    lam_vecs = jnp.concatenate([diff_lam_q1, diff_lam_k1, diff_lam_q2, diff_lam_k2], 0).astype(F32)
    ya = _mixer_call(
        _attention(q.reshape(B, S, n_qk), k.reshape(B, S, n_qk), v.reshape(B, S, n_qk),
                   rel_bias, lam_vecs, diff_subln_g[0]),
        _attn_stages, ("parallel", "parallel"), "diff_attention")
    yb = _mixer_call(_rwkv(rwkv_features, decay_rows, S, rwkv_lnx_g[0], rwkv_lnx_b[0]),
                     _rwkv_stages, ("parallel", "arbitrary"), "rwkv7_chunked")
    out = _post(x2, ln_in_g, ln_in_b, ya.reshape(T, n_qk), yb.reshape(T, width), ga, gb,
                w_up_a[0], w_up_b[0], w_out[0], ln1_g[0], ln1_b[0],
                ffn_w_gate[0], ffn_w_up[0], ffn_w_down[0], ln2_g[0], ln2_b[0])
    return out.reshape(B, S, D)
```

```python
import functools
import math

import jax
import jax.numpy as jnp
from jax import lax
from jax.experimental import pallas as pl
from jax.experimental.pallas import tpu as pltpu

F32 = jnp.float32
BF16 = jnp.bfloat16

DIFF_HEADS = 4
DIFF_HEAD_DIM = 64
HEAD_PAIR = 2 * DIFF_HEAD_DIM
RWKV_HEAD_DIM = 64
DECAY_RANK = 64
ICLR_RANK = 64
GATE_RANK = 128
NUM_BUCKETS = 32
MAX_DISTANCE = 128
LN_EPS = 1e-5
LNX_EPS = 64e-5
NEG_BIG = -1e30
DEPTH = 1
DEEPNORM_ALPHA = (2.0 * DEPTH) ** 0.25
LAMBDA_INIT = 0.8 - 0.6 * math.exp(-0.3 * 0)
LOG2_E = math.log2(math.e)

LANES = 128
SUBLANES = 8
MXU_DEPTH = 256
VMEM_LIMIT_BYTES = 56 * 1024 * 1024
ROW_TILE = 256
ATTN_TILE = 256
SCAN_CHUNK = 64
SCAN_TASK = 256
SCAN_TILE = 512


def _mm(a, b):
    return jnp.dot(a.astype(BF16), b.astype(BF16), preferred_element_type=F32)


def _mm_nt(a, b):
    return lax.dot_general(a.astype(BF16), b.astype(BF16), (((1,), (1,)), ((), ())),
                           preferred_element_type=F32)


def _mm_tn(a, b):
    return lax.dot_general(a.astype(BF16), b.astype(BF16), (((0,), (0,)), ((), ())),
                           preferred_element_type=F32)


def _split2(x):
    hi = x.astype(BF16)
    lo = (x - hi.astype(F32)).astype(BF16)
    return hi, lo


def _sel_mm(sel, x):
    hi, lo = _split2(x)
    d = functools.partial(jnp.dot, preferred_element_type=F32)
    return d(sel, hi) + d(sel, lo)


def _head_sum(x, sel):
    xb = x.astype(BF16)
    return jnp.concatenate(
        [jnp.dot(xb[:, c0:c0 + LANES], sel, preferred_element_type=F32)
         for c0 in range(0, x.shape[1], LANES)], axis=1)


def _layer_norm(x, g, b, eps=LN_EPS):
    mu = jnp.mean(x, -1, keepdims=True)
    xc = x - mu
    var = jnp.mean(xc * xc, -1, keepdims=True)
    return xc * lax.rsqrt(var + eps) * g + b


def _const_spec(shape):
    return pl.BlockSpec(shape, lambda *_: (0,) * len(shape), pipeline_mode=pl.Buffered(1))


def _rwkv_feature_stages(p, before, mu, w0, w2, a0, a2, g2, k_k, k_a, r_k, tri, hsum, *, width):
    L = SCAN_CHUNK
    n_chunks = p.shape[0] // L
    d, out = {}, {}

    def shift():
        row = lax.broadcasted_iota(jnp.int32, (p.shape[0], 1), 0)
        p_prev = jnp.where(row == 0, before, pltpu.roll(p, 1, 0))
        ps = p + (p_prev - p) * mu
        d["r"], d["k"], out["v"] = ps[:, 0:width], ps[:, width:2 * width], ps[:, 2 * width:3 * width]
        lr = ps[:, 3 * width:3 * width + DECAY_RANK + ICLR_RANK]
        gl = ps[:, 3 * width + DECAY_RANK + ICLR_RANK:]
        d["lr"], d["tanh_lr"], d["sig_gl"] = lr.astype(BF16), jnp.tanh(lr).astype(BF16), jax.nn.sigmoid(gl)
        d["kk"] = d["k"] * k_k

    def project():
        d["z"] = w0 + _mm(d.pop("tanh_lr"), w2)
        d["ai"] = a0 + _mm(d.pop("lr"), a2)
        out["gate"] = _mm(d.pop("sig_gl"), g2)
        d["ss"] = _head_sum(d["kk"] * d["kk"], hsum)

    def rates():
        d["logw"] = (-math.exp(-0.5) * LOG2_E) * jax.nn.sigmoid(d.pop("z"))
        iclr = jax.nn.sigmoid(d.pop("ai"))
        d["kk"] = d["kk"] * lax.rsqrt(jnp.maximum(d.pop("ss"), 1e-24))
        d["b"] = d["kk"] * iclr
        d["k2"] = d.pop("k") * (1.0 + (iclr - 1.0) * k_a)
        d["rk"] = d["r"] * d["k2"] * r_k

    def sums():
        out["bonus"] = _head_sum(d.pop("rk"), hsum) * out["v"]
        d["c"] = _sel_mm(tri, d["logw"])

    def scale():
        c = d.pop("c")
        c_last = [c[i * L + L - 1:i * L + L, :] for i in range(n_chunks)]
        c_end = jnp.concatenate([jnp.broadcast_to(x, (L, width)) for x in c_last], 0)
        inv_g = jnp.exp2(-c)
        to_end = jnp.exp2(c_end - c)
        b, k2 = d.pop("b"), d.pop("k2")
        out.update(ah=-d.pop("kk") * jnp.exp2(c - d.pop("logw")), rh=d.pop("r") * jnp.exp2(c),
                   bt=b * inv_g, kt=k2 * inv_g, bb=b * to_end, kb=k2 * to_end,
                   gl=[jnp.exp2(x) for x in c_last])

    return (shift, project, rates, sums, scale), out


RWKV_FEATURES = ("ah", "rh", "bt", "kt", "bb", "kb", "v", "bonus", "gate")


def _inproj_kernel(x_ref, g_ref, b_ref, w_ref, mu_ref, w0_ref, w2_ref, a0_ref, a2_ref, g2_ref,
                   kk_ref, ka_ref, rk_ref, tri_ref, hsum_ref,
                   q_ref, k_ref, v_ref, ga_ref, gb_ref, *rest, splits, tiles_per_seq):
    feat_refs = dict(zip(RWKV_FEATURES, rest[:len(RWKV_FEATURES)]))
    gl_ref, h_even, h_odd, last_s = rest[len(RWKV_FEATURES):]
    i = pl.program_id(0)
    width = w0_ref.shape[1]

    @pl.when(i == 0)
    def _():
        h_even[...] = jnp.zeros_like(h_even)
        last_s[...] = jnp.zeros_like(last_s)

    def step(h_prev, h_next):
        h_next[...] = _layer_norm(x_ref[...], g_ref[...], b_ref[...]).astype(BF16)

        def proj(lo, hi):
            return jnp.dot(h_prev[...], w_ref[:, lo:hi], preferred_element_type=F32)

        s_q, s_k, s_v, s_rw, s_ga = splits
        rw = proj(s_v, s_rw)
        seq_start = ((i - 1) % tiles_per_seq) == 0
        before = jnp.where(seq_start, 0.0, last_s[0:1, :])
        last_s[0:1, :] = rw[rw.shape[0] - 1:, :]
        (shift, project, rates, sums, scale), f = _rwkv_feature_stages(
            rw, before, mu_ref[...], w0_ref[...], w2_ref[...], a0_ref[...], a2_ref[...],
            g2_ref[...], kk_ref[...], ka_ref[...], rk_ref[...], tri_ref[...], hsum_ref[...],
            width=width)
        shift()
        ga_ref[...] = jax.nn.sigmoid(proj(s_rw, s_ga)).astype(BF16)
        project()
        gb_ref[...] = jax.nn.sigmoid(proj(s_ga, w_ref.shape[1])).astype(BF16)
        rates()
        sums()
        q_ref[...] = (proj(0, s_q) * (DIFF_HEAD_DIM ** -0.5 * LOG2_E)).astype(BF16)
        scale()
        for name, ref in feat_refs.items():
            ref[...] = f[name].astype(ref.dtype)
        gl_ref[...] = jnp.zeros_like(gl_ref)
        for c, x in enumerate(f["gl"]):
            gl_ref[c:c + 1, :] = x
        k_ref[...] = proj(s_q, s_k).astype(BF16)
        v_ref[...] = proj(s_k, s_v).astype(BF16)

    @pl.when(i % 2 == 0)
    def _():
        step(h_even, h_odd)

    @pl.when(i % 2 == 1)
    def _():
        step(h_odd, h_even)


def _inproj(x2, seq_len, ln_g, ln_b, w_in_bf, sizes, mu, w0, w2, a0, a2, g2, k_k, k_a, r_k):
    T, D = x2.shape
    n_q, n_k, n_v, n_rw, n_ga, n_gb = sizes
    width = w0.shape[-1]
    acc, cuts = 0, []
    for s in sizes[:-1]:
        acc += s
        cuts.append(acc)
    splits = tuple(cuts)
    tm, L = ROW_TILE, SCAN_CHUNK
    n_tiles = T // tm
    assert seq_len % tm == 0 and tm % L == 0 and tm // L <= SUBLANES
    assert n_rw == 3 * width + DECAY_RANK + ICLR_RANK + GATE_RANK
    assert DECAY_RANK + ICLR_RANK == LANES
    ti = jnp.arange(tm)
    same_chunk = (ti[:, None] // L) == (ti[None, :] // L)
    tri = (same_chunk & (ti[None, :] <= ti[:, None])).astype(BF16)
    ch = jnp.arange(LANES) // RWKV_HEAD_DIM
    hsum = (ch[:, None] == ch[None, :]).astype(BF16)
    w2p = jnp.concatenate([w2, jnp.zeros((ICLR_RANK, width), w2.dtype)], 0).astype(BF16)
    a2p = jnp.concatenate([jnp.zeros((DECAY_RANK, width), a2.dtype), a2], 0).astype(BF16)
    vec = lambda a: a.reshape(1, -1).astype(F32)
    row_in = pl.BlockSpec((tm, D), lambda i: (jnp.minimum(i, n_tiles - 1), 0))
    row = lambda n: pl.BlockSpec((tm, n), lambda i: (jnp.maximum(i - 1, 0), 0))
    n_feat = len(RWKV_FEATURES)
    return pl.pallas_call(
        functools.partial(_inproj_kernel, splits=splits, tiles_per_seq=seq_len // tm),
        grid=(n_tiles + 1,),
        in_specs=[row_in, _const_spec((1, D)), _const_spec((1, D)), _const_spec(w_in_bf.shape),
                  _const_spec((1, n_rw)), _const_spec((1, width)), _const_spec((LANES, width)),
                  _const_spec((1, width)), _const_spec((LANES, width)), _const_spec((GATE_RANK, width)),
                  _const_spec((1, width)), _const_spec((1, width)), _const_spec((1, width)),
                  _const_spec((tm, tm)), _const_spec((LANES, LANES))],
        out_specs=[row(n_q), row(n_k), row(n_v), row(n_ga), row(n_gb)] + [row(width)] * n_feat
                  + [pl.BlockSpec((None, SUBLANES, width), lambda i: (jnp.maximum(i - 1, 0), 0, 0))],
        out_shape=[jax.ShapeDtypeStruct((T, n_q), BF16), jax.ShapeDtypeStruct((T, n_k), BF16),
                   jax.ShapeDtypeStruct((T, n_v), BF16), jax.ShapeDtypeStruct((T, n_ga), BF16),
                   jax.ShapeDtypeStruct((T, n_gb), BF16)]
                  + [jax.ShapeDtypeStruct((T, width), BF16)] * n_feat
                  + [jax.ShapeDtypeStruct((n_tiles, SUBLANES, width), F32)],
        scratch_shapes=[pltpu.VMEM((tm, D), BF16), pltpu.VMEM((tm, D), BF16),
                        pltpu.VMEM((SUBLANES, n_rw), F32)],
        compiler_params=pltpu.CompilerParams(dimension_semantics=("arbitrary",),
                                             vmem_limit_bytes=VMEM_LIMIT_BYTES),
        name="ln_inproj",
    )(x2, ln_g.reshape(1, D), ln_b.reshape(1, D), w_in_bf, vec(mu), vec(w0), w2p, vec(a0), a2p,
      g2.astype(BF16), vec(k_k), vec(k_a), vec(r_k), tri, hsum)


def _attn_stages(q_ref, k_ref, v_ref, bdiag_ref, bnear_ref, bfar_ref, lam_ref, g_ref, o_ref,
                 v1_s, *, nq, t):
    lane = lax.broadcasted_iota(jnp.int32, (t, HEAD_PAIR), 1)
    lo = lane < DIFF_HEAD_DIM
    lv = lam_ref[...]
    lam = (jnp.exp(jnp.sum(lv[0:1] * lv[1:2], keepdims=True))
           - jnp.exp(jnp.sum(lv[2:3] * lv[3:4], keepdims=True)) + LAMBDA_INIT)
    bfar = bfar_ref[...][:, 0:1]
    zero = jnp.zeros((), BF16)
    rowmax = lambda a: jnp.max(a, -1, keepdims=True)
    v1_s[:, :HEAD_PAIR] = v_ref[...]
    v1_s[:, HEAD_PAIR:] = jnp.ones((v_ref.shape[0], HEAD_PAIR), BF16)

    def logits(qi):
        q = q_ref[qi * t:(qi + 1) * t, :]
        qs = jnp.concatenate([jnp.where(lo, q, zero), jnp.where(lo, zero, q)], axis=0)
        return lax.dot_general(qs, k_ref[0:(qi + 1) * t, :], (((1,), (1,)), ((), ())),
                               preferred_element_type=F32)

    def softmax(qi, s):
        s_diag = s[:, qi * t:] + bdiag_ref[...]
        m = rowmax(s_diag)
        if qi >= 1:
            s_near = s[:, (qi - 1) * t:qi * t] + bnear_ref[...]
            m = jnp.maximum(m, rowmax(s_near))
        if qi >= 2:
            s_far = s[:, :(qi - 1) * t]
            m = jnp.maximum(m, rowmax(s_far) + bfar)
        parts = []
        if qi >= 2:
            parts.append(jnp.exp2(s_far - (m - bfar)))
        if qi >= 1:
            parts.append(jnp.exp2(s_near - m))
        parts.append(jnp.exp2(s_diag - m))
        return jnp.concatenate([x.astype(BF16) for x in parts], axis=1)

    def values(qi, p):
        acc = jnp.dot(p, v1_s[0:(qi + 1) * t, :], preferred_element_type=F32)
        l = acc[:, HEAD_PAIR:HEAD_PAIR + 1]
        acc = acc[:, :HEAD_PAIR]
        o = acc[:t] * (1.0 / l[:t]) - acc[t:] * (lam / l[t:])
        ms = jnp.mean(o * o, -1, keepdims=True)
        o = o * lax.rsqrt(ms + LN_EPS) * g_ref[...] * (1.0 - LAMBDA_INIT)
        o_ref[qi * t:(qi + 1) * t, :] = o.astype(o_ref.dtype)

    s_q, p_q = {}, {}

    def do_logits(qi):
        s_q[qi] = logits(qi)

    def do_softmax(qi):
        p_q[qi] = softmax(qi, s_q.pop(qi))

    def do_values(qi):
        values(qi, p_q.pop(qi))

    order = list(reversed(range(nq)))
    stages = []
    for step in range(nq + 2):
        if step < nq:
            stages.append(functools.partial(do_logits, order[step]))
        if 0 <= step - 1 < nq:
            stages.append(functools.partial(do_softmax, order[step - 1]))
        if 0 <= step - 2 < nq:
            stages.append(functools.partial(do_values, order[step - 2]))
    return stages


def _t5_bucket(dist):
    max_exact = NUM_BUCKETS // 2
    d = jnp.maximum(dist, 1).astype(F32)
    large = max_exact + (jnp.log(d / max_exact) / math.log(MAX_DISTANCE / max_exact)
                         * (NUM_BUCKETS - max_exact)).astype(jnp.int32)
    large = jnp.minimum(large, NUM_BUCKETS - 1)
    return jnp.where(dist < max_exact, dist, large)


def _attention(q, k, v, rel_bias, lam_vecs, subln_g):
    B, S, _ = q.shape
    t = ATTN_TILE
    assert S % t == 0 and t >= MAX_DISTANCE
    nq = S // t
    by_dist = rel_bias.astype(F32)[_t5_bucket(jnp.arange(2 * t, dtype=jnp.int32))].T * LOG2_E
    dd = jnp.arange(t, dtype=jnp.int32)[:, None] - jnp.arange(t, dtype=jnp.int32)[None, :]

    def toeplitz(f_neg, f_pos):
        period = jnp.concatenate([f_neg, jnp.zeros_like(f_neg[:, :1]), f_pos[:, ::-1]], axis=1)
        rows = jnp.tile(period, (1, t))[:, :t * (2 * t - 1)].reshape(-1, t, 2 * t - 1)
        return rows[:, :, :t]

    bdiag = jnp.where(dd >= 0, toeplitz(jnp.broadcast_to(by_dist[:, :1], (DIFF_HEADS, t)),
                                        by_dist[:, 1:t]), NEG_BIG)
    bnear = toeplitz(by_dist[:, t:0:-1], by_dist[:, t + 1:])
    bdiag = jnp.concatenate([bdiag, bdiag], axis=1)
    bnear = jnp.concatenate([bnear, bnear], axis=1)
    bfar = jnp.broadcast_to((rel_bias[NUM_BUCKETS - 1] * LOG2_E)[:, None, None],
                            (DIFF_HEADS, 1, LANES))
    seq = pl.BlockSpec((None, S, HEAD_PAIR), lambda b, h: (b, 0, h))
    per_head = lambda r, c: pl.BlockSpec((None, r, c), lambda b, h: (h, 0, 0))
    return dict(
        kwargs=dict(nq=nq, t=t),
        operands=(q, k, v, bdiag, bnear, bfar, lam_vecs, subln_g.reshape(1, HEAD_PAIR)),
        in_specs=[seq, seq, seq, per_head(2 * t, t), per_head(2 * t, t), per_head(1, LANES),
                  pl.BlockSpec((4, DIFF_HEAD_DIM), lambda b, h: (0, 0)),
                  pl.BlockSpec((1, HEAD_PAIR), lambda b, h: (0, 0))],
        out_spec=seq,
        out_shape=jax.ShapeDtypeStruct((B, S, DIFF_HEADS * HEAD_PAIR), BF16),
        scratch_shapes=[pltpu.VMEM((S, 2 * HEAD_PAIR), BF16)],
        grid=(B, DIFF_HEADS))


def _rwkv_stages(ah_s, rh_s, bt_s, kt_s, bb_s, kb_s, v_s, bonus_s, gate_s, gl_s,
                 lg_ref, lb_ref, hsum_ref, o_ref, h_s, y_s, *, n_pairs):
    L = SCAN_CHUNK
    TL = ah_s.shape[0]

    @pl.when(pl.program_id(1) == 0)
    def _():
        h_s[...] = jnp.zeros_like(h_s)

    HT = SCAN_TASK
    n_tasks = TL // HT
    n_rounds = int(math.log2(L))
    n_chunks = HT // L
    n_blocks = n_chunks * n_pairs

    lane = lax.broadcasted_iota(jnp.int32, (L, LANES), 1)
    m0 = lane < RWKV_HEAD_DIM
    m1 = jnp.logical_not(m0)
    ri = lax.broadcasted_iota(jnp.int32, (2 * L, LANES), 0)
    ci = lax.broadcasted_iota(jnp.int32, (2 * L, LANES), 1)
    eye = ri == ci
    rl = lax.broadcasted_iota(jnp.int32, (L, LANES), 0)
    tril2 = (lane % L) <= rl
    stril2 = (lane % L) < rl
    eye2_f = jnp.where((lane % L) == rl, 1.0, 0.0).astype(F32)
    zeros = jnp.zeros((L, LANES), F32)

    sel = lambda m, x: jnp.where(m, x, jnp.zeros_like(x))
    cat0 = lambda *xs: jnp.concatenate(xs, 0)
    cat1 = lambda *xs: jnp.concatenate(xs, 1)
    block_diag = lambda x: cat0(sel(m0, x), sel(m1, x))
    swap_halves = lambda x: pltpu.roll(x, RWKV_HEAD_DIM, 1)
    st = [dict() for _ in range(n_tasks)]
    hs = [h_s[pi] for pi in range(n_pairs)]

    def rows_of(t):
        return slice(t * HT, (t + 1) * HT)

    def block_ld(ref, t, i):
        c, pi = divmod(i, n_pairs)
        r0 = t * HT + c * L
        return ref[r0:r0 + L, pi * LANES:(pi + 1) * LANES]

    def gram(t):
        d = st[t]
        ah = [block_ld(ah_s, t, i) for i in range(n_blocks)]
        rh = [block_ld(rh_s, t, i) for i in range(n_blocks)]
        bt = [block_ld(bt_s, t, i) for i in range(n_blocks)]
        kt = [block_ld(kt_s, t, i) for i in range(n_blocks)]
        d["ah0"] = [sel(m0, x) for x in ah]
        d["ah1"] = [sel(m1, x) for x in ah]
        d["rh"] = rh
        g = [_mm_nt(cat0(d["ah0"][i], sel(m0, rh[i]), d["ah1"][i], sel(m1, rh[i])), cat0(bt[i], kt[i]))
             for i in range(n_blocks)]
        pair = lambda x0, x1: jnp.where(m0, x0, swap_halves(x1))
        pair_r = lambda x0, x1: jnp.where(m0, swap_halves(x0), x1)
        d["ak"] = [sel(stril2, pair_r(x[:L], x[2 * L:3 * L])) for x in g]
        d["rb"] = [sel(tril2, pair(x[L:2 * L], x[3 * L:])) for x in g]
        d["rk"] = [sel(tril2, pair_r(x[L:2 * L], x[3 * L:])) for x in g]
        d["pw"] = [sel(stril2, pair(x[:L], x[2 * L:3 * L])) for x in g]
        d["tinv"] = [eye2_f + x for x in d["pw"]]
        d["round"] = 0

    def neumann(t):
        d = st[t]
        first, last = d["round"] == 0, d["round"] == n_rounds - 1
        d["round"] += 1
        if first:
            d["pw"] = [_mm(x, block_diag(x)) for x in d["pw"]]
            return
        out = [_mm(tm if last else cat0(tm, x), block_diag(x)) for x, tm in zip(d["pw"], d["tinv"])]
        d["tinv"] = [tm + o[:L] for tm, o in zip(d["tinv"], out)]
        if not last:
            d["pw"] = [o[L:] for o in out]

    def solve(t):
        d = st[t]
        d.pop("pw")
        vv = [block_ld(v_s, t, i) for i in range(n_blocks)]
        d["vv"] = vv
        kv = [_mm(cat0(ak, rk), block_diag(x)) for ak, rk, x in zip(d.pop("ak"), d.pop("rk"), vv)]
        d["rkv"] = [x[L:] for x in kv]
        w = [_mm(tm, cat1(cat0(a0_, a1_), block_diag(x[:L])))
             for tm, a0_, a1_, x in zip(d.pop("tinv"), d.pop("ah0"), d.pop("ah1"), kv)]
        d["w1"] = [x[:, :LANES] for x in w]
        d["w2"] = [x[:, LANES:] for x in w]

    def fold(t):
        d = st[t]
        lhs, cm, y0, gcol = [], [], [], []
        for i in range(n_blocks):
            w1, w2, vv = d["w1"][i], d["w2"][i], d["vv"][i]
            ac = _mm_tn(cat0(block_ld(bb_s, t, i), block_ld(kb_s, t, i)),
                        cat1(cat0(w1, zeros), cat0(w2, vv)))
            ry0 = _mm(d["rb"][i], cat1(block_diag(w1), block_diag(w2)))
            a_mat = jnp.where(m0, ac[:L, :LANES], ac[L:, :LANES])
            cm.append(jnp.where(m0, ac[:L, LANES:], ac[L:, LANES:]))
            lhs.append(cat0(a_mat, d["rh"][i] + ry0[:, :LANES]).astype(BF16))
            y0.append(ry0[:, LANES:] + d["rkv"][i])
            ch, pi = divmod(i, n_pairs)
            g_end = gl_s[t, ch:ch + 1, pi * LANES:(pi + 1) * LANES]
            g_rows = jnp.sum(jnp.where(eye, jnp.broadcast_to(g_end, (2 * L, LANES)), 0.0),
                             axis=1, keepdims=True)
            gcol.append(jnp.where(m0, g_rows[:L], g_rows[L:]))
        for key in ("w1", "w2", "vv", "rb", "rkv", "rh"):
            d.pop(key)
        d.update(lhs=lhs, cm=cm, y0=y0, gcol=gcol)

    def scan(t, c):
        d = st[t]
        r0 = t * HT + c * L
        for pi in range(n_pairs):
            i = c * n_pairs + pi
            h = hs[pi]
            out = _mm(d["lhs"][i], block_diag(h))
            hs[pi] = d["gcol"][i] * h + out[:L] + d["cm"][i]
            y_s[r0:r0 + L, pi * LANES:(pi + 1) * LANES] = out[L:] + d["y0"][i]

    def finish(t):
        rows = rows_of(t)
        y = y_s[rows, :]
        inv_n = 1.0 / RWKV_HEAD_DIM
        mean = _head_sum(y, hsum_ref[...]) * inv_n
        yc = y - mean
        var = _head_sum(yc * yc, hsum_ref[...]) * inv_n
        yn = yc * lax.rsqrt(var + LNX_EPS) * lg_ref[...] + lb_ref[...]
        o_ref[rows, :] = ((yn + bonus_s[rows, :].astype(F32))
                          * gate_s[rows, :].astype(F32)).astype(o_ref.dtype)

    assert n_rounds >= n_chunks

    def carry_out():
        for pi in range(n_pairs):
            h_s[pi] = hs[pi]

    stages = []
    add = lambda f, *a: stages.append(functools.partial(f, *a))
    for t in range(n_tasks):
        prv = t - 1 if t >= 1 else None
        add(gram, t)
        for rnd in range(n_rounds):
            add(neumann, t)
            if prv is not None and rnd < n_chunks:
                add(scan, prv, rnd)
        if prv is not None:
            add(finish, prv)
        add(solve, t); add(fold, t)
    for c in range(n_chunks):
        add(scan, n_tasks - 1, c)
    add(finish, n_tasks - 1)
    add(carry_out)
    return stages


def _rwkv(features, decay_rows, seq_len, lnx_g, lnx_b):
    T, width = features[0].shape
    B, S = T // seq_len, seq_len
    n_pairs = width // LANES
    L, TL = SCAN_CHUNK, min(SCAN_TILE, S)
    assert S % TL == 0 and TL % SCAN_TASK == 0 and SCAN_TASK == ROW_TILE
    ch = jnp.arange(LANES) // RWKV_HEAD_DIM
    hsum = (ch[:, None] == ch[None, :]).astype(BF16)
    vec = lambda a: a.reshape(1, -1).astype(F32)
    tile = pl.BlockSpec((None, TL, width), lambda b, j: (b, j, 0))
    tasks = TL // SCAN_TASK
    return dict(
        kwargs=dict(n_pairs=n_pairs),
        operands=tuple(f.reshape(B, S, width) for f in features)
                 + (decay_rows.reshape(B, S // SCAN_TASK, SUBLANES, width), vec(lnx_g), vec(lnx_b), hsum),
        in_specs=[tile] * len(features)
                 + [pl.BlockSpec((None, tasks, SUBLANES, width), lambda b, j: (b, j, 0, 0)),
                    _const_spec((1, width)), _const_spec((1, width)), _const_spec((LANES, LANES))],
        out_spec=tile,
        out_shape=jax.ShapeDtypeStruct((B, S, width), BF16),
        scratch_shapes=[pltpu.VMEM((n_pairs, L, LANES), F32), pltpu.VMEM((TL, width), F32)],
        grid=(B, S // TL))


def _run_stages(stages_fn, *refs, **kwargs):
    for stage in stages_fn(*refs, **kwargs):
        stage()


def _mixer_call(call, stages_fn, semantics, name):
    return pl.pallas_call(
        functools.partial(_run_stages, stages_fn, **call["kwargs"]),
        grid=call["grid"],
        in_specs=call["in_specs"],
        out_specs=call["out_spec"],
        out_shape=call["out_shape"],
        scratch_shapes=call.get("scratch_shapes", ()),
        compiler_params=pltpu.CompilerParams(dimension_semantics=semantics,
                                             vmem_limit_bytes=VMEM_LIMIT_BYTES),
        name=name,
    )(*call["operands"])


def _post_kernel(x_ref, lng_ref, lnb_ref, ya_ref, yb_ref, ga_ref, gb_ref, wua_ref, wub_ref, wo_ref,
                 l1g_ref, l1b_ref, wg_ref, wu_ref, wd_ref, l2g_ref, l2b_ref, o_ref,
                 z_even, z_odd, *, ffn_cuts):
    i = pl.program_id(0)
    dot = functools.partial(jnp.dot, preferred_element_type=F32)

    @pl.when(i == 0)
    def _():
        z_even[...] = jnp.zeros_like(z_even)

    def ffn_part(h1b, lo, hi):
        gate = dot(h1b, wg_ref[:, lo:hi])
        up = dot(h1b, wu_ref[:, lo:hi])
        act = (gate * jax.nn.sigmoid(gate) * up).astype(BF16)
        return dot(act, wd_ref[lo:hi, :])

    def step(z_prev, z_next):
        up_a = dot(ya_ref[...], wua_ref[...])
        up_b = dot(yb_ref[...], wub_ref[...])
        h1 = _layer_norm(z_prev[...], l1g_ref[...], l1b_ref[...])
        h1b = h1.astype(BF16)
        ffn = ffn_part(h1b, ffn_cuts[0], ffn_cuts[1])
        merged = ga_ref[...].astype(F32) * up_a + gb_ref[...].astype(F32) * up_b
        for lo, hi in zip(ffn_cuts[1:-1], ffn_cuts[2:]):
            ffn = ffn + ffn_part(h1b, lo, hi)
        mix = dot(merged.astype(BF16), wo_ref[...])
        o_ref[...] = _layer_norm(DEEPNORM_ALPHA * h1 + ffn, l2g_ref[...], l2b_ref[...])
        h = _layer_norm(x_ref[...], lng_ref[...], lnb_ref[...])
        z_next[...] = DEEPNORM_ALPHA * h + mix

    @pl.when(i % 2 == 0)
    def _():
        step(z_even, z_odd)

    @pl.when(i % 2 == 1)
    def _():
        step(z_odd, z_even)


def _post(x2, ln_g, ln_b, ya, yb, ga, gb, w_up_a, w_up_b, w_out, ln1_g, ln1_b,
          w_gate, w_up, w_down, ln2_g, ln2_b):
    T, D = x2.shape
    hidden = w_gate.shape[1]
    assert hidden % MXU_DEPTH == 0
    ffn_cuts = (0, (hidden // MXU_DEPTH // 2) * MXU_DEPTH, hidden)
    tm = ROW_TILE
    n_tiles = T // tm
    row = lambda n: pl.BlockSpec((tm, n), lambda i: (jnp.minimum(i, n_tiles - 1), 0))
    vec = lambda a: a.reshape(1, D)
    cs = lambda a: _const_spec(a.shape)
    wts = [w.astype(BF16) for w in (w_up_a, w_up_b, w_out, w_gate, w_up, w_down)]
    wua, wub, wo, wg, wu, wd = wts
    return pl.pallas_call(
        functools.partial(_post_kernel, ffn_cuts=ffn_cuts),
        grid=(n_tiles + 1,),
        in_specs=[row(D), _const_spec((1, D)), _const_spec((1, D)), row(ya.shape[1]), row(yb.shape[1]),
                  row(D), row(D), cs(wua), cs(wub), cs(wo), _const_spec((1, D)), _const_spec((1, D)),
                  cs(wg), cs(wu), cs(wd), _const_spec((1, D)), _const_spec((1, D))],
        out_specs=pl.BlockSpec((tm, D), lambda i: (jnp.maximum(i - 1, 0), 0)),
        out_shape=jax.ShapeDtypeStruct((T, D), F32),
        scratch_shapes=[pltpu.VMEM((tm, D), F32), pltpu.VMEM((tm, D), F32)],
        compiler_params=pltpu.CompilerParams(dimension_semantics=("arbitrary",),
                                             vmem_limit_bytes=VMEM_LIMIT_BYTES),
        name="merge_ffn",
    )(x2, vec(ln_g), vec(ln_b), ya, yb, ga, gb, wua, wub, wo, vec(ln1_g), vec(ln1_b),
      wg, wu, wd, vec(ln2_g), vec(ln2_b))


def kernel(x, ln_in_g, ln_in_b, rel_bias, w_in, diff_lam_q1, diff_lam_k1, diff_lam_q2, diff_lam_k2, diff_subln_g, rwkv_mu, rwkv_w0, rwkv_w2, rwkv_a0, rwkv_a2, rwkv_g2, rwkv_k_k, rwkv_k_a, rwkv_r_k, rwkv_lnx_g, rwkv_lnx_b, w_up_a, w_up_b, w_out, ln1_g, ln1_b, ffn_w_gate, ffn_w_up, ffn_w_down, ln2_g, ln2_b):
    B, S, D = x.shape
    T = B * S
    width = rwkv_w0.shape[-1]
    n_qk = DIFF_HEADS * HEAD_PAIR
    n_rw = 3 * width + DECAY_RANK + ICLR_RANK + GATE_RANK
    sizes = (n_qk, n_qk, n_qk, n_rw, D, D)
    assert w_in.shape[0] == 1 and w_in.shape[2] == sum(sizes), "single-layer model expected"
    x2 = x.reshape(T, D)
    q, k, v, ga, gb, *rwkv_features, decay_rows = _inproj(
        x2, S, ln_in_g, ln_in_b, w_in[0].astype(BF16), sizes, rwkv_mu[0], rwkv_w0[0], rwkv_w2[0],
        rwkv_a0[0], rwkv_a2[0], rwkv_g2[0], rwkv_k_k[0], rwkv_k_a[0], rwkv_r_k[0])
    lam_vecs = jnp.concatenate([diff_lam_q1, diff_lam_k1, diff_lam_q2, diff_lam_k2], 0).astype(F32)
    ya = _mixer_call(
        _attention(q.reshape(B, S, n_qk), k.reshape(B, S, n_qk), v.reshape(B, S, n_qk),
                   rel_bias, lam_vecs, diff_subln_g[0]),
        _attn_stages, ("parallel", "parallel"), "diff_attention")
    yb = _mixer_call(_rwkv(rwkv_features, decay_rows, S, rwkv_lnx_g[0], rwkv_lnx_b[0]),
                     _rwkv_stages, ("parallel", "arbitrary"), "rwkv7_chunked")
    out = _post(x2, ln_in_g, ln_in_b, ya.reshape(T, n_qk), yb.reshape(T, width), ga, gb,
                w_up_a[0], w_up_b[0], w_out[0], ln1_g[0], ln1_b[0],
                ffn_w_gate[0], ffn_w_up[0], ffn_w_down[0], ln2_g[0], ln2_b[0])
    return out.reshape(B, S, D)
```

```python
import functools
import math

import jax
import jax.numpy as jnp
from jax import lax
from jax.experimental import pallas as pl
from jax.experimental.pallas import tpu as pltpu

F32 = jnp.float32
BF16 = jnp.bfloat16

DIFF_HEADS = 4
DIFF_HEAD_DIM = 64
HEAD_PAIR = 2 * DIFF_HEAD_DIM
RWKV_HEAD_DIM = 64
DECAY_RANK = 64
ICLR_RANK = 64
GATE_RANK = 128
NUM_BUCKETS = 32
MAX_DISTANCE = 128
LN_EPS = 1e-5
LNX_EPS = 64e-5
NEG_BIG = -1e30
DEPTH = 1
DEEPNORM_ALPHA = (2.0 * DEPTH) ** 0.25
LAMBDA_INIT = 0.8 - 0.6 * math.exp(-0.3 * 0)
LOG2_E = math.log2(math.e)

LANES = 128
SUBLANES = 8
MXU_DEPTH = 256
VMEM_LIMIT_BYTES = 56 * 1024 * 1024
ROW_TILE = 256
ATTN_TILE = 256
ATTN_HEADS_PER_STEP = 2
SCAN_CHUNK = 64
SCAN_TASK = 256
SCAN_TILE = 512


def _mm(a, b):
    return jnp.dot(a.astype(BF16), b.astype(BF16), preferred_element_type=F32)


def _mm_nt(a, b):
    return lax.dot_general(a.astype(BF16), b.astype(BF16), (((1,), (1,)), ((), ())),
                           preferred_element_type=F32)


def _mm_tn(a, b):
    return lax.dot_general(a.astype(BF16), b.astype(BF16), (((0,), (0,)), ((), ())),
                           preferred_element_type=F32)


def _split2(x):
    hi = x.astype(BF16)
    lo = (x - hi.astype(F32)).astype(BF16)
    return hi, lo


def _sel_mm(sel, x):
    hi, lo = _split2(x)
    d = functools.partial(jnp.dot, preferred_element_type=F32)
    return d(sel, hi) + d(sel, lo)


def _head_sum(x, sel):
    xb = x.astype(BF16)
    return jnp.concatenate(
        [jnp.dot(xb[:, c0:c0 + LANES], sel, preferred_element_type=F32)
         for c0 in range(0, x.shape[1], LANES)], axis=1)


def _layer_norm(x, g, b, eps=LN_EPS):
    mu = jnp.mean(x, -1, keepdims=True)
    xc = x - mu
    var = jnp.mean(xc * xc, -1, keepdims=True)
    return xc * lax.rsqrt(var + eps) * g + b


def _const_spec(shape):
    return pl.BlockSpec(shape, lambda *_: (0,) * len(shape), pipeline_mode=pl.Buffered(1))


def _rwkv_feature_stages(p, before, mu, w0, w2, a0, a2, g2, k_k, k_a, r_k, tri, hsum, *, width):
    L = SCAN_CHUNK
    n_chunks = p.shape[0] // L
    d, out = {}, {}

    def shift():
        row = lax.broadcasted_iota(jnp.int32, (p.shape[0], 1), 0)
        p_prev = jnp.where(row == 0, before, pltpu.roll(p, 1, 0))
        ps = p + (p_prev - p) * mu
        d["r"], d["k"], out["v"] = ps[:, 0:width], ps[:, width:2 * width], ps[:, 2 * width:3 * width]
        lr = ps[:, 3 * width:3 * width + DECAY_RANK + ICLR_RANK]
        gl = ps[:, 3 * width + DECAY_RANK + ICLR_RANK:]
        d["lr"], d["tanh_lr"], d["sig_gl"] = lr.astype(BF16), jnp.tanh(lr).astype(BF16), jax.nn.sigmoid(gl)
        d["kk"] = d["k"] * k_k

    def project():
        d["z"] = w0 + _mm(d.pop("tanh_lr"), w2)
        d["ai"] = a0 + _mm(d.pop("lr"), a2)
        out["gate"] = _mm(d.pop("sig_gl"), g2)
        d["ss"] = _head_sum(d["kk"] * d["kk"], hsum)

    def rates():
        d["logw"] = (-math.exp(-0.5) * LOG2_E) * jax.nn.sigmoid(d.pop("z"))
        iclr = jax.nn.sigmoid(d.pop("ai"))
        d["kk"] = d["kk"] * lax.rsqrt(jnp.maximum(d.pop("ss"), 1e-24))
        d["b"] = d["kk"] * iclr
        d["k2"] = d.pop("k") * (1.0 + (iclr - 1.0) * k_a)
        d["rk"] = d["r"] * d["k2"] * r_k

    def sums():
        out["bonus"] = _head_sum(d.pop("rk"), hsum) * out["v"]
        d["c"] = _sel_mm(tri, d["logw"])

    def scale():
        c = d.pop("c")
        c_last = [c[i * L + L - 1:i * L + L, :] for i in range(n_chunks)]
        c_end = jnp.concatenate([jnp.broadcast_to(x, (L, width)) for x in c_last], 0)
        inv_g = jnp.exp2(-c)
        to_end = jnp.exp2(c_end - c)
        b, k2 = d.pop("b"), d.pop("k2")
        out.update(ah=-d.pop("kk") * jnp.exp2(c - d.pop("logw")), rh=d.pop("r") * jnp.exp2(c),
                   bt=b * inv_g, kt=k2 * inv_g, bb=b * to_end, kb=k2 * to_end,
                   gl=[jnp.exp2(x) for x in c_last])

    return (shift, project, rates, sums, scale), out


RWKV_FEATURES = ("ah", "rh", "bt", "kt", "bb", "kb", "v", "bonus", "gate")


def _inproj_kernel(x_ref, g_ref, b_ref, w_ref, mu_ref, w0_ref, w2_ref, a0_ref, a2_ref, g2_ref,
                   kk_ref, ka_ref, rk_ref, tri_ref, hsum_ref,
                   q_ref, k_ref, v_ref, ga_ref, gb_ref, *rest, splits, tiles_per_seq):
    feat_refs = dict(zip(RWKV_FEATURES, rest[:len(RWKV_FEATURES)]))
    gl_ref, h_even, h_odd, last_s = rest[len(RWKV_FEATURES):]
    i = pl.program_id(0)
    width = w0_ref.shape[1]

    @pl.when(i == 0)
    def _():
        h_even[...] = jnp.zeros_like(h_even)
        last_s[...] = jnp.zeros_like(last_s)

    def step(h_prev, h_next):
        h_next[...] = _layer_norm(x_ref[...], g_ref[...], b_ref[...]).astype(BF16)

        def proj(lo, hi):
            return jnp.dot(h_prev[...], w_ref[:, lo:hi], preferred_element_type=F32)

        s_q, s_k, s_v, s_rw, s_ga = splits
        rw = proj(s_v, s_rw)
        seq_start = ((i - 1) % tiles_per_seq) == 0
        before = jnp.where(seq_start, 0.0, last_s[0:1, :])
        last_s[0:1, :] = rw[rw.shape[0] - 1:, :]
        (shift, project, rates, sums, scale), f = _rwkv_feature_stages(
            rw, before, mu_ref[...], w0_ref[...], w2_ref[...], a0_ref[...], a2_ref[...],
            g2_ref[...], kk_ref[...], ka_ref[...], rk_ref[...], tri_ref[...], hsum_ref[...],
            width=width)
        shift()
        ga_ref[...] = jax.nn.sigmoid(proj(s_rw, s_ga)).astype(BF16)
        project()
        gb_ref[...] = jax.nn.sigmoid(proj(s_ga, w_ref.shape[1])).astype(BF16)
        rates()
        sums()
        q_ref[...] = (proj(0, s_q) * (DIFF_HEAD_DIM ** -0.5 * LOG2_E)).astype(BF16)
        scale()
        for name, ref in feat_refs.items():
            ref[...] = f[name].astype(ref.dtype)
        gl_ref[...] = jnp.zeros_like(gl_ref)
        for c, x in enumerate(f["gl"]):
            gl_ref[c:c + 1, :] = x
        k_ref[...] = proj(s_q, s_k).astype(BF16)
        v_ref[...] = proj(s_k, s_v).astype(BF16)

    @pl.when(i % 2 == 0)
    def _():
        step(h_even, h_odd)

    @pl.when(i % 2 == 1)
    def _():
        step(h_odd, h_even)


def _inproj(x2, seq_len, ln_g, ln_b, w_in_bf, sizes, mu, w0, w2, a0, a2, g2, k_k, k_a, r_k):
    T, D = x2.shape
    n_q, n_k, n_v, n_rw, n_ga, n_gb = sizes
    width = w0.shape[-1]
    acc, cuts = 0, []
    for s in sizes[:-1]:
        acc += s
        cuts.append(acc)
    splits = tuple(cuts)
    tm, L = ROW_TILE, SCAN_CHUNK
    n_tiles = T // tm
    assert seq_len % tm == 0 and tm % L == 0 and tm // L <= SUBLANES
    assert n_rw == 3 * width + DECAY_RANK + ICLR_RANK + GATE_RANK
    assert DECAY_RANK + ICLR_RANK == LANES
    ti = jnp.arange(tm)
    same_chunk = (ti[:, None] // L) == (ti[None, :] // L)
    tri = (same_chunk & (ti[None, :] <= ti[:, None])).astype(BF16)
    ch = jnp.arange(LANES) // RWKV_HEAD_DIM
    hsum = (ch[:, None] == ch[None, :]).astype(BF16)
    w2p = jnp.concatenate([w2, jnp.zeros((ICLR_RANK, width), w2.dtype)], 0).astype(BF16)
    a2p = jnp.concatenate([jnp.zeros((DECAY_RANK, width), a2.dtype), a2], 0).astype(BF16)
    vec = lambda a: a.reshape(1, -1).astype(F32)
    row_in = pl.BlockSpec((tm, D), lambda i: (jnp.minimum(i, n_tiles - 1), 0))
    row = lambda n: pl.BlockSpec((tm, n), lambda i: (jnp.maximum(i - 1, 0), 0))
    n_feat = len(RWKV_FEATURES)
    return pl.pallas_call(
        functools.partial(_inproj_kernel, splits=splits, tiles_per_seq=seq_len // tm),
        grid=(n_tiles + 1,),
        in_specs=[row_in, _const_spec((1, D)), _const_spec((1, D)), _const_spec(w_in_bf.shape),
                  _const_spec((1, n_rw)), _const_spec((1, width)), _const_spec((LANES, width)),
                  _const_spec((1, width)), _const_spec((LANES, width)), _const_spec((GATE_RANK, width)),
                  _const_spec((1, width)), _const_spec((1, width)), _const_spec((1, width)),
                  _const_spec((tm, tm)), _const_spec((LANES, LANES))],
        out_specs=[row(n_q), row(n_k), row(n_v), row(n_ga), row(n_gb)] + [row(width)] * n_feat
                  + [pl.BlockSpec((None, SUBLANES, width), lambda i: (jnp.maximum(i - 1, 0), 0, 0))],
        out_shape=[jax.ShapeDtypeStruct((T, n_q), BF16), jax.ShapeDtypeStruct((T, n_k), BF16),
                   jax.ShapeDtypeStruct((T, n_v), BF16), jax.ShapeDtypeStruct((T, n_ga), BF16),
                   jax.ShapeDtypeStruct((T, n_gb), BF16)]
                  + [jax.ShapeDtypeStruct((T, width), BF16)] * n_feat
                  + [jax.ShapeDtypeStruct((n_tiles, SUBLANES, width), F32)],
        scratch_shapes=[pltpu.VMEM((tm, D), BF16), pltpu.VMEM((tm, D), BF16),
                        pltpu.VMEM((SUBLANES, n_rw), F32)],
        compiler_params=pltpu.CompilerParams(dimension_semantics=("arbitrary",),
                                             vmem_limit_bytes=VMEM_LIMIT_BYTES),
        name="ln_inproj",
    )(x2, ln_g.reshape(1, D), ln_b.reshape(1, D), w_in_bf, vec(mu), vec(w0), w2p, vec(a0), a2p,
      g2.astype(BF16), vec(k_k), vec(k_a), vec(r_k), tri, hsum)


def _attn_stages(q_ref, k_ref, v_ref, bdiag_ref, bnear_ref, bfar_ref, lam_ref, g_ref, o_ref,
                 v1_s, *, nq, t):
    lane = lax.broadcasted_iota(jnp.int32, (t, HEAD_PAIR), 1)
    lo = lane < DIFF_HEAD_DIM
    lv = lam_ref[...]
    lam = (jnp.exp(jnp.sum(lv[0:1] * lv[1:2], keepdims=True))
           - jnp.exp(jnp.sum(lv[2:3] * lv[3:4], keepdims=True)) + LAMBDA_INIT)
    bfar = bfar_ref[...][:, 0:1]
    zero = jnp.zeros((), BF16)
    rowmax = lambda a: jnp.max(a, -1, keepdims=True)
    v1_s[:, :HEAD_PAIR] = v_ref[...]
    v1_s[:, HEAD_PAIR:] = jnp.ones((v_ref.shape[0], HEAD_PAIR), BF16)

    def logits(qi):
        q = q_ref[qi * t:(qi + 1) * t, :]
        qs = jnp.concatenate([jnp.where(lo, q, zero), jnp.where(lo, zero, q)], axis=0)
        return lax.dot_general(qs, k_ref[0:(qi + 1) * t, :], (((1,), (1,)), ((), ())),
                               preferred_element_type=F32)

    def softmax(qi, s):
        s_diag = s[:, qi * t:] + bdiag_ref[...]
        m = rowmax(s_diag)
        if qi >= 1:
            s_near = s[:, (qi - 1) * t:qi * t] + bnear_ref[...]
            m = jnp.maximum(m, rowmax(s_near))
        if qi >= 2:
            s_far = s[:, :(qi - 1) * t]
            m = jnp.maximum(m, rowmax(s_far) + bfar)
        parts = []
        if qi >= 2:
            parts.append(jnp.exp2(s_far - (m - bfar)))
        if qi >= 1:
            parts.append(jnp.exp2(s_near - m))
        parts.append(jnp.exp2(s_diag - m))
        return jnp.concatenate([x.astype(BF16) for x in parts], axis=1)

    def values(qi, p):
        acc = jnp.dot(p, v1_s[0:(qi + 1) * t, :], preferred_element_type=F32)
        l = acc[:, HEAD_PAIR:HEAD_PAIR + 1]
        acc = acc[:, :HEAD_PAIR]
        o = acc[:t] * (1.0 / l[:t]) - acc[t:] * (lam / l[t:])
        ms = jnp.mean(o * o, -1, keepdims=True)
        o = o * lax.rsqrt(ms + LN_EPS) * g_ref[...] * (1.0 - LAMBDA_INIT)
        o_ref[qi * t:(qi + 1) * t, :] = o.astype(o_ref.dtype)

    s_q, p_q = {}, {}

    def do_logits(qi):
        s_q[qi] = logits(qi)

    def do_softmax(qi):
        p_q[qi] = softmax(qi, s_q.pop(qi))

    def do_values(qi):
        values(qi, p_q.pop(qi))

    order = list(reversed(range(nq)))
    stages = []
    for step in range(nq + 2):
        if step < nq:
            stages.append(functools.partial(do_logits, order[step]))
        if 0 <= step - 1 < nq:
            stages.append(functools.partial(do_softmax, order[step - 1]))
        if 0 <= step - 2 < nq:
            stages.append(functools.partial(do_values, order[step - 2]))
    return stages


def _attn_heads_stages(q_ref, k_ref, v_ref, bdiag_ref, bnear_ref, bfar_ref, lam_ref, g_ref, o_ref,
                       v1_s, *, nq, t, heads):
    lanes = lambda e: pl.ds(e * HEAD_PAIR, HEAD_PAIR)
    per_head = [
        _attn_stages(q_ref.at[:, lanes(e)], k_ref.at[:, lanes(e)], v_ref.at[:, lanes(e)],
                     bdiag_ref.at[e], bnear_ref.at[e], bfar_ref.at[e], lam_ref, g_ref,
                     o_ref.at[:, lanes(e)], v1_s.at[e], nq=nq, t=t)
        for e in range(heads)]
    return [stage for group in zip(*per_head) for stage in group]


def _t5_bucket(dist):
    max_exact = NUM_BUCKETS // 2
    d = jnp.maximum(dist, 1).astype(F32)
    large = max_exact + (jnp.log(d / max_exact) / math.log(MAX_DISTANCE / max_exact)
                         * (NUM_BUCKETS - max_exact)).astype(jnp.int32)
    large = jnp.minimum(large, NUM_BUCKETS - 1)
    return jnp.where(dist < max_exact, dist, large)


def _attention(q, k, v, rel_bias, lam_vecs, subln_g):
    B, S, _ = q.shape
    t = ATTN_TILE
    assert S % t == 0 and t >= MAX_DISTANCE
    nq = S // t
    by_dist = rel_bias.astype(F32)[_t5_bucket(jnp.arange(2 * t, dtype=jnp.int32))].T * LOG2_E
    dd = jnp.arange(t, dtype=jnp.int32)[:, None] - jnp.arange(t, dtype=jnp.int32)[None, :]

    def toeplitz(f_neg, f_pos):
        period = jnp.concatenate([f_neg, jnp.zeros_like(f_neg[:, :1]), f_pos[:, ::-1]], axis=1)
        rows = jnp.tile(period, (1, t))[:, :t * (2 * t - 1)].reshape(-1, t, 2 * t - 1)
        return rows[:, :, :t]

    bdiag = jnp.where(dd >= 0, toeplitz(jnp.broadcast_to(by_dist[:, :1], (DIFF_HEADS, t)),
                                        by_dist[:, 1:t]), NEG_BIG)
    bnear = toeplitz(by_dist[:, t:0:-1], by_dist[:, t + 1:])
    bdiag = jnp.concatenate([bdiag, bdiag], axis=1)
    bnear = jnp.concatenate([bnear, bnear], axis=1)
    bfar = jnp.broadcast_to((rel_bias[NUM_BUCKETS - 1] * LOG2_E)[:, None, None],
                            (DIFF_HEADS, 1, LANES))
    hp = ATTN_HEADS_PER_STEP
    assert DIFF_HEADS % hp == 0
    seq = pl.BlockSpec((None, S, hp * HEAD_PAIR), lambda b, h: (b, 0, h))
    per_head = lambda r, c: pl.BlockSpec((hp, r, c), lambda b, h: (h, 0, 0))
    return dict(
        kwargs=dict(nq=nq, t=t, heads=hp),
        operands=(q, k, v, bdiag, bnear, bfar, lam_vecs, subln_g.reshape(1, HEAD_PAIR)),
        in_specs=[seq, seq, seq, per_head(2 * t, t), per_head(2 * t, t), per_head(1, LANES),
                  pl.BlockSpec((4, DIFF_HEAD_DIM), lambda b, h: (0, 0)),
                  pl.BlockSpec((1, HEAD_PAIR), lambda b, h: (0, 0))],
        out_spec=seq,
        out_shape=jax.ShapeDtypeStruct((B, S, DIFF_HEADS * HEAD_PAIR), BF16),
        scratch_shapes=[pltpu.VMEM((hp, S, 2 * HEAD_PAIR), BF16)],
        grid=(B, DIFF_HEADS // hp))


def _rwkv_stages(ah_s, rh_s, bt_s, kt_s, bb_s, kb_s, v_s, bonus_s, gate_s, gl_s,
                 lg_ref, lb_ref, hsum_ref, o_ref, h_s, y_s, *, n_pairs):
    L = SCAN_CHUNK
    TL = ah_s.shape[0]

    @pl.when(pl.program_id(1) == 0)
    def _():
        h_s[...] = jnp.zeros_like(h_s)

    HT = SCAN_TASK
    n_tasks = TL // HT
    n_rounds = int(math.log2(L))
    n_chunks = HT // L
    n_blocks = n_chunks * n_pairs

    lane = lax.broadcasted_iota(jnp.int32, (L, LANES), 1)
    m0 = lane < RWKV_HEAD_DIM
    m1 = jnp.logical_not(m0)
    ri = lax.broadcasted_iota(jnp.int32, (2 * L, LANES), 0)
    ci = lax.broadcasted_iota(jnp.int32, (2 * L, LANES), 1)
    eye = ri == ci
    rl = lax.broadcasted_iota(jnp.int32, (L, LANES), 0)
    tril2 = (lane % L) <= rl
    stril2 = (lane % L) < rl
    eye2_f = jnp.where((lane % L) == rl, 1.0, 0.0).astype(F32)
    zeros = jnp.zeros((L, LANES), F32)

    sel = lambda m, x: jnp.where(m, x, jnp.zeros_like(x))
    cat0 = lambda *xs: jnp.concatenate(xs, 0)
    cat1 = lambda *xs: jnp.concatenate(xs, 1)
    block_diag = lambda x: cat0(sel(m0, x), sel(m1, x))
    swap_halves = lambda x: pltpu.roll(x, RWKV_HEAD_DIM, 1)
    st = [dict() for _ in range(n_tasks)]
    hs = [h_s[pi] for pi in range(n_pairs)]

    def rows_of(t):
        return slice(t * HT, (t + 1) * HT)

    def block_ld(ref, t, i):
        c, pi = divmod(i, n_pairs)
        r0 = t * HT + c * L
        return ref[r0:r0 + L, pi * LANES:(pi + 1) * LANES]

    def gram(t):
        d = st[t]
        ah = [block_ld(ah_s, t, i) for i in range(n_blocks)]
        rh = [block_ld(rh_s, t, i) for i in range(n_blocks)]
        bt = [block_ld(bt_s, t, i) for i in range(n_blocks)]
        kt = [block_ld(kt_s, t, i) for i in range(n_blocks)]
        d["ah0"] = [sel(m0, x) for x in ah]
        d["ah1"] = [sel(m1, x) for x in ah]
        d["rh"] = rh
        g = [_mm_nt(cat0(d["ah0"][i], sel(m0, rh[i]), d["ah1"][i], sel(m1, rh[i])), cat0(bt[i], kt[i]))
             for i in range(n_blocks)]
        pair = lambda x0, x1: jnp.where(m0, x0, swap_halves(x1))
        pair_r = lambda x0, x1: jnp.where(m0, swap_halves(x0), x1)
        d["ak"] = [sel(stril2, pair_r(x[:L], x[2 * L:3 * L])) for x in g]
        d["rb"] = [sel(tril2, pair(x[L:2 * L], x[3 * L:])) for x in g]
        d["rk"] = [sel(tril2, pair_r(x[L:2 * L], x[3 * L:])) for x in g]
        d["pw"] = [sel(stril2, pair(x[:L], x[2 * L:3 * L])) for x in g]
        d["tinv"] = [eye2_f + x for x in d["pw"]]
        d["round"] = 0

    def neumann(t):
        d = st[t]
        first, last = d["round"] == 0, d["round"] == n_rounds - 1
        d["round"] += 1
        if first:
            d["pw"] = [_mm(x, block_diag(x)) for x in d["pw"]]
            return
        out = [_mm(tm if last else cat0(tm, x), block_diag(x)) for x, tm in zip(d["pw"], d["tinv"])]
        d["tinv"] = [tm + o[:L] for tm, o in zip(d["tinv"], out)]
        if not last:
            d["pw"] = [o[L:] for o in out]

    def solve(t):
        d = st[t]
        d.pop("pw")
        vv = [block_ld(v_s, t, i) for i in range(n_blocks)]
        d["vv"] = vv
        kv = [_mm(cat0(ak, rk), block_diag(x)) for ak, rk, x in zip(d.pop("ak"), d.pop("rk"), vv)]
        d["rkv"] = [x[L:] for x in kv]
        w = [_mm(tm, cat1(cat0(a0_, a1_), block_diag(x[:L])))
             for tm, a0_, a1_, x in zip(d.pop("tinv"), d.pop("ah0"), d.pop("ah1"), kv)]
        d["w1"] = [x[:, :LANES] for x in w]
        d["w2"] = [x[:, LANES:] for x in w]

    def fold(t):
        d = st[t]
        lhs, cm, y0, gcol = [], [], [], []
        for i in range(n_blocks):
            w1, w2, vv = d["w1"][i], d["w2"][i], d["vv"][i]
            ac = _mm_tn(cat0(block_ld(bb_s, t, i), block_ld(kb_s, t, i)),
                        cat1(cat0(w1, zeros), cat0(w2, vv)))
            ry0 = _mm(d["rb"][i], cat1(block_diag(w1), block_diag(w2)))
            a_mat = jnp.where(m0, ac[:L, :LANES], ac[L:, :LANES])
            cm.append(jnp.where(m0, ac[:L, LANES:], ac[L:, LANES:]))
            lhs.append(cat0(a_mat, d["rh"][i] + ry0[:, :LANES]).astype(BF16))
            y0.append(ry0[:, LANES:] + d["rkv"][i])
            ch, pi = divmod(i, n_pairs)
            g_end = gl_s[t, ch:ch + 1, pi * LANES:(pi + 1) * LANES]
            g_rows = jnp.sum(jnp.where(eye, jnp.broadcast_to(g_end, (2 * L, LANES)), 0.0),
                             axis=1, keepdims=True)
            gcol.append(jnp.where(m0, g_rows[:L], g_rows[L:]))
        for key in ("w1", "w2", "vv", "rb", "rkv", "rh"):
            d.pop(key)
        d.update(lhs=lhs, cm=cm, y0=y0, gcol=gcol)

    def scan(t, c):
        d = st[t]
        r0 = t * HT + c * L
        for pi in range(n_pairs):
            i = c * n_pairs + pi
            h = hs[pi]
            out = _mm(d["lhs"][i], block_diag(h))
            hs[pi] = d["gcol"][i] * h + out[:L] + d["cm"][i]
            y_s[r0:r0 + L, pi * LANES:(pi + 1) * LANES] = out[L:] + d["y0"][i]

    def finish(t):
        rows = rows_of(t)
        y = y_s[rows, :]
        inv_n = 1.0 / RWKV_HEAD_DIM
        mean = _head_sum(y, hsum_ref[...]) * inv_n
        yc = y - mean
        var = _head_sum(yc * yc, hsum_ref[...]) * inv_n
        yn = yc * lax.rsqrt(var + LNX_EPS) * lg_ref[...] + lb_ref[...]
        o_ref[rows, :] = ((yn + bonus_s[rows, :].astype(F32))
                          * gate_s[rows, :].astype(F32)).astype(o_ref.dtype)

    assert n_rounds >= n_chunks

    def carry_out():
        for pi in range(n_pairs):
            h_s[pi] = hs[pi]

    stages = []
    add = lambda f, *a: stages.append(functools.partial(f, *a))
    for t in range(n_tasks):
        prv = t - 1 if t >= 1 else None
        add(gram, t)
        for rnd in range(n_rounds):
            add(neumann, t)
            if prv is not None and rnd < n_chunks:
                add(scan, prv, rnd)
        if prv is not None:
            add(finish, prv)
        add(solve, t); add(fold, t)
    for c in range(n_chunks):
        add(scan, n_tasks - 1, c)
    add(finish, n_tasks - 1)
    add(carry_out)
    return stages


def _rwkv(features, decay_rows, seq_len, lnx_g, lnx_b):
    T, width = features[0].shape
    B, S = T // seq_len, seq_len
    n_pairs = width // LANES
    L, TL = SCAN_CHUNK, min(SCAN_TILE, S)
    assert S % TL == 0 and TL % SCAN_TASK == 0 and SCAN_TASK == ROW_TILE
    ch = jnp.arange(LANES) // RWKV_HEAD_DIM
    hsum = (ch[:, None] == ch[None, :]).astype(BF16)
    vec = lambda a: a.reshape(1, -1).astype(F32)
    tile = pl.BlockSpec((None, TL, width), lambda b, j: (b, j, 0))
    tasks = TL // SCAN_TASK
    return dict(
        kwargs=dict(n_pairs=n_pairs),
        operands=tuple(f.reshape(B, S, width) for f in features)
                 + (decay_rows.reshape(B, S // SCAN_TASK, SUBLANES, width), vec(lnx_g), vec(lnx_b), hsum),
        in_specs=[tile] * len(features)
                 + [pl.BlockSpec((None, tasks, SUBLANES, width), lambda b, j: (b, j, 0, 0)),
                    _const_spec((1, width)), _const_spec((1, width)), _const_spec((LANES, LANES))],
        out_spec=tile,
        out_shape=jax.ShapeDtypeStruct((B, S, width), BF16),
        scratch_shapes=[pltpu.VMEM((n_pairs, L, LANES), F32), pltpu.VMEM((TL, width), F32)],
        grid=(B, S // TL))


def _run_stages(stages_fn, *refs, **kwargs):
    for stage in stages_fn(*refs, **kwargs):
        stage()


def _mixer_call(call, stages_fn, semantics, name):
    return pl.pallas_call(
        functools.partial(_run_stages, stages_fn, **call["kwargs"]),
        grid=call["grid"],
        in_specs=call["in_specs"],
        out_specs=call["out_spec"],
        out_shape=call["out_shape"],
        scratch_shapes=call.get("scratch_shapes", ()),
        compiler_params=pltpu.CompilerParams(dimension_semantics=semantics,
                                             vmem_limit_bytes=VMEM_LIMIT_BYTES),
        name=name,
    )(*call["operands"])


def _post_kernel(x_ref, lng_ref, lnb_ref, ya_ref, yb_ref, ga_ref, gb_ref, wua_ref, wub_ref, wo_ref,
                 l1g_ref, l1b_ref, wg_ref, wu_ref, wd_ref, l2g_ref, l2b_ref, o_ref,
                 z_even, z_odd, *, ffn_cuts):
    i = pl.program_id(0)
    dot = functools.partial(jnp.dot, preferred_element_type=F32)

    @pl.when(i == 0)
    def _():
        z_even[...] = jnp.zeros_like(z_even)

    def ffn_part(h1b, lo, hi):
        gate = dot(h1b, wg_ref[:, lo:hi])
        up = dot(h1b, wu_ref[:, lo:hi])
        act = (gate * jax.nn.sigmoid(gate) * up).astype(BF16)
        return dot(act, wd_ref[lo:hi, :])

    def step(z_prev, z_next):
        up_a = dot(ya_ref[...], wua_ref[...])
        up_b = dot(yb_ref[...], wub_ref[...])
        h1 = _layer_norm(z_prev[...], l1g_ref[...], l1b_ref[...])
        h1b = h1.astype(BF16)
        ffn = ffn_part(h1b, ffn_cuts[0], ffn_cuts[1])
        merged = ga_ref[...].astype(F32) * up_a + gb_ref[...].astype(F32) * up_b
        for lo, hi in zip(ffn_cuts[1:-1], ffn_cuts[2:]):
            ffn = ffn + ffn_part(h1b, lo, hi)
        mix = dot(merged.astype(BF16), wo_ref[...])
        o_ref[...] = _layer_norm(DEEPNORM_ALPHA * h1 + ffn, l2g_ref[...], l2b_ref[...])
        h = _layer_norm(x_ref[...], lng_ref[...], lnb_ref[...])
        z_next[...] = DEEPNORM_ALPHA * h + mix

    @pl.when(i % 2 == 0)
    def _():
        step(z_even, z_odd)

    @pl.when(i % 2 == 1)
    def _():
        step(z_odd, z_even)


def _post(x2, ln_g, ln_b, ya, yb, ga, gb, w_up_a, w_up_b, w_out, ln1_g, ln1_b,
          w_gate, w_up, w_down, ln2_g, ln2_b):
    T, D = x2.shape
    hidden = w_gate.shape[1]
    assert hidden % MXU_DEPTH == 0
    ffn_cuts = (0, (hidden // MXU_DEPTH // 2) * MXU_DEPTH, hidden)
    tm = ROW_TILE
    n_tiles = T // tm
    row = lambda n: pl.BlockSpec((tm, n), lambda i: (jnp.minimum(i, n_tiles - 1), 0))
    vec = lambda a: a.reshape(1, D)
    cs = lambda a: _const_spec(a.shape)
    wts = [w.astype(BF16) for w in (w_up_a, w_up_b, w_out, w_gate, w_up, w_down)]
    wua, wub, wo, wg, wu, wd = wts
    return pl.pallas_call(
        functools.partial(_post_kernel, ffn_cuts=ffn_cuts),
        grid=(n_tiles + 1,),
        in_specs=[row(D), _const_spec((1, D)), _const_spec((1, D)), row(ya.shape[1]), row(yb.shape[1]),
                  row(D), row(D), cs(wua), cs(wub), cs(wo), _const_spec((1, D)), _const_spec((1, D)),
                  cs(wg), cs(wu), cs(wd), _const_spec((1, D)), _const_spec((1, D))],
        out_specs=pl.BlockSpec((tm, D), lambda i: (jnp.maximum(i - 1, 0), 0)),
        out_shape=jax.ShapeDtypeStruct((T, D), F32),
        scratch_shapes=[pltpu.VMEM((tm, D), F32), pltpu.VMEM((tm, D), F32)],
        compiler_params=pltpu.CompilerParams(dimension_semantics=("arbitrary",),
                                             vmem_limit_bytes=VMEM_LIMIT_BYTES),
        name="merge_ffn",
    )(x2, vec(ln_g), vec(ln_b), ya, yb, ga, gb, wua, wub, wo, vec(ln1_g), vec(ln1_b),
      wg, wu, wd, vec(ln2_g), vec(ln2_b))


def kernel(x, ln_in_g, ln_in_b, rel_bias, w_in, diff_lam_q1, diff_lam_k1, diff_lam_q2, diff_lam_k2, diff_subln_g, rwkv_mu, rwkv_w0, rwkv_w2, rwkv_a0, rwkv_a2, rwkv_g2, rwkv_k_k, rwkv_k_a, rwkv_r_k, rwkv_lnx_g, rwkv_lnx_b, w_up_a, w_up_b, w_out, ln1_g, ln1_b, ffn_w_gate, ffn_w_up, ffn_w_down, ln2_g, ln2_b):
    B, S, D = x.shape
    T = B * S
    width = rwkv_w0.shape[-1]
    n_qk = DIFF_HEADS * HEAD_PAIR
    n_rw = 3 * width + DECAY_RANK + ICLR_RANK + GATE_RANK
    sizes = (n_qk, n_qk, n_qk, n_rw, D, D)
    assert w_in.shape[0] == 1 and w_in.shape[2] == sum(sizes), "single-layer model expected"
    x2 = x.reshape(T, D)
    q, k, v, ga, gb, *rwkv_features, decay_rows = _inproj(
        x2, S, ln_in_g, ln_in_b, w_in[0].astype(BF16), sizes, rwkv_mu[0], rwkv_w0[0], rwkv_w2[0],
        rwkv_a0[0], rwkv_a2[0], rwkv_g2[0], rwkv_k_k[0], rwkv_k_a[0], rwkv_r_k[0])
    lam_vecs = jnp.concatenate([diff_lam_q1, diff_lam_k1, diff_lam_q2, diff_lam_k2], 0).astype(F32)
    ya = _mixer_call(
        _attention(q.reshape(B, S, n_qk), k.reshape(B, S, n_qk), v.reshape(B, S, n_qk),
                   rel_bias, lam_vecs, diff_subln_g[0]),
        _attn_heads_stages, ("parallel", "parallel"), "diff_attention")
    yb = _mixer_call(_rwkv(rwkv_features, decay_rows, S, rwkv_lnx_g[0], rwkv_lnx_b[0]),
                     _rwkv_stages, ("parallel", "arbitrary"), "rwkv7_chunked")
    out = _post(x2, ln_in_g, ln_in_b, ya.reshape(T, n_qk), yb.reshape(T, width), ga, gb,
                w_up_a[0], w_up_b[0], w_out[0], ln1_g[0], ln1_b[0],
                ffn_w_gate[0], ffn_w_up[0], ffn_w_down[0], ln2_g[0], ln2_b[0])
    return out.reshape(B, S, D)
```
